```python
import math
import jax, jax.numpy as jnp
from jax import lax
import numpy as np


D_MODEL = 1024
BATCH = 2
SEQ = 8192
DEPTH = 2

CTX_LEN = 256
GRID_W = 64
N_HEADS = 8
QK_NOPE = 64
QK_ROPE = 32
QK_DIM = QK_NOPE + QK_ROPE
V_DIM = 64
Q_LORA = 384
KV_LORA = 256
ROPE_BASE = 10000.0
Q_BLOCK = 128
ATTN_W = N_HEADS * V_DIM
ATTN_SCALE = QK_DIM ** -0.5
POOL_WINDOWS = (2, 4, 8, 16)
POOL_GROUPS = 4
POOL_GC = 128
POOL_W = POOL_GROUPS * POOL_GC
FFT_GROUPS = 4
FFT_GC = 128
FFT_W = FFT_GROUPS * FFT_GC
N_BRANCH = 3
OFF_Q = 0
OFF_KV = OFF_Q + Q_LORA
OFF_KR = OFF_KV + KV_LORA
OFF_POOL = OFF_KR + QK_ROPE
OFF_FFT = OFF_POOL + POOL_W
OFF_GATE = OFF_FFT + FFT_W
IN_COLS = OFF_GATE + N_BRANCH * D_MODEL
D_FF = 2816
N_EXPERTS = 8
TOP_K = 2
D_FF_EXPERT = 2816
N_DENSE = (DEPTH + 1) // 2
N_MOE = DEPTH // 2
EPS = 1e-6

kernel_name = "hybrid_mla_pool_fourier_moe_dit"


def rmsnorm(x, g):
    xf = x.astype(jnp.float32)
    y = xf * lax.rsqrt(jnp.mean(xf * xf, axis=-1, keepdims=True) + EPS)
    return (y * g.astype(jnp.float32)).astype(x.dtype)


def modulate(h, shift, scale):
    return h * (1 + scale) + shift


def axial_rope_tables(n_rows, dtype):
    rows = jnp.repeat(jnp.arange(n_rows, dtype=jnp.float32), GRID_W)
    cols = jnp.tile(jnp.arange(GRID_W, dtype=jnp.float32), n_rows)
    half = QK_ROPE // 2
    inv = ROPE_BASE ** (-jnp.arange(0, half, 2, dtype=jnp.float32) / half)
    ang = jnp.stack([rows[:, None] * inv, cols[:, None] * inv], axis=1)
    return jnp.cos(ang).astype(dtype), jnp.sin(ang).astype(dtype)


def apply_axial_rope(x, cos, sin):
    nf = QK_ROPE // 4
    shp = x.shape
    xr = x.reshape(*shp[:-1], 2, 2, nf)
    mid = (1,) * (x.ndim - 3)
    c = cos.reshape(cos.shape[0], *mid, 2, nf)
    s = sin.reshape(sin.shape[0], *mid, 2, nf)
    x1 = xr[..., 0, :]
    x2 = xr[..., 1, :]
    out = jnp.stack([x1 * c - x2 * s, x2 * c + x1 * s], axis=-2)
    return out.reshape(shp)


def mla_queries(pq, g_q, w_uq):
    q = jnp.einsum('blr,rhd->blhd', rmsnorm(pq, g_q), w_uq)
    return q[..., :QK_NOPE], q[..., QK_NOPE:]


def mla_keys_values(pkv, g_kv, w_ukv):
    kv = jnp.einsum('blr,rhd->blhd', rmsnorm(pkv, g_kv), w_ukv)
    return kv[..., :QK_NOPE], kv[..., QK_NOPE:]


def attend(q_nope, q_rope, k_nope, k_rope, v):
    s = (jnp.einsum('bqhd,bkhd->bhqk', q_nope, k_nope)
         + jnp.einsum('bqhr,bkr->bhqk', q_rope, k_rope))
    p = jax.nn.softmax(s.astype(jnp.float32) * ATTN_SCALE, axis=-1).astype(v.dtype)
    return jnp.einsum('bhqk,bkhd->bqhd', p, v)


def blocked_attend(q_nope, q_rope, k_nope, k_rope, v):
    B, L = q_nope.shape[:2]
    nb = L // Q_BLOCK

    def to_blocks(t):
        return t.reshape(B, nb, Q_BLOCK, *t.shape[2:]).swapaxes(0, 1)

    out = lax.map(lambda qs: attend(qs[0], qs[1], k_nope, k_rope, v),
                  (to_blocks(q_nope), to_blocks(q_rope)))
    return out.swapaxes(0, 1).reshape(B, L, N_HEADS, V_DIM)


def multiscale_pool(p, w_pool, s_pool):
    B, L, _ = p.shape
    pg = p.reshape(B, L, POOL_GROUPS, POOL_GC).astype(jnp.float32)
    cs = jnp.cumsum(pg, axis=1)
    cs = jnp.concatenate([jnp.zeros_like(cs[:, :1]), cs], axis=1)
    win = jnp.array(POOL_WINDOWS, dtype=jnp.int32)
    left = win // 2
    right = win - 1 - left
    t = jnp.arange(L, dtype=jnp.int32)[:, None]
    hi = jnp.minimum(t + right + 1, L)
    lo = jnp.maximum(t - left, 0)
    g = jnp.arange(POOL_GROUPS, dtype=jnp.int32)[None, :]
    window_sum = cs[:, hi, g, :] - cs[:, lo, g, :]
    mean = window_sum / (hi - lo).astype(jnp.float32)[None, :, :, None]
    mixed = (mean - pg).astype(p.dtype)
    y = jnp.einsum('blgc,gcd->blgd', mixed, w_pool)
    return y.reshape(B, L, POOL_W) * s_pool


def fourier_mix(f):
    B, L, _ = f.shape
    fg = f.reshape(B, L, FFT_GROUPS, FFT_GC).astype(jnp.float32)
    fr = jnp.fft.fft2(fg, axes=(1, 3), norm='ortho').real
    return fr.reshape(B, L, FFT_W).astype(f.dtype)


def gated_merge(attn_out, pooled, fourier, gate_pre, w_oa, w_ob, w_oc, w_out):
    B, L = attn_out.shape[:2]
    g = jax.nn.sigmoid(gate_pre).reshape(B, L, N_BRANCH, D_MODEL)
    y = (g[:, :, 0] * (attn_out.reshape(B, L, ATTN_W) @ w_oa)
         + g[:, :, 1] * (pooled @ w_ob)
         + g[:, :, 2] * (fourier @ w_oc))
    return y @ w_out


def token_mixer(h_lat, h_ctx, w_in, b_gate, g_q, w_uq, g_kv, w_ukv, w_pool, s_pool,
                w_oa, w_ob, w_oc, w_out, cos, sin, with_ctx_out):
    p_lat = h_lat @ w_in
    q_nope, q_rope = mla_queries(p_lat[..., OFF_Q:OFF_KV], g_q, w_uq)
    q_rope = apply_axial_rope(q_rope, cos, sin)
    k_nope_l, v_l = mla_keys_values(p_lat[..., OFF_KV:OFF_KR], g_kv, w_ukv)
    k_rope_l = apply_axial_rope(p_lat[..., OFF_KR:OFF_POOL], cos, sin)
    p_ctx_kv = h_ctx @ w_in[:, OFF_KV:OFF_POOL]
    k_nope_c, v_c = mla_keys_values(p_ctx_kv[..., :KV_LORA], g_kv, w_ukv)
    k_rope_c = p_ctx_kv[..., KV_LORA:]
    k_nope = jnp.concatenate([k_nope_c, k_nope_l], axis=1)
    k_rope = jnp.concatenate([k_rope_c, k_rope_l], axis=1)
    v = jnp.concatenate([v_c, v_l], axis=1)
    attn_l = blocked_attend(q_nope, q_rope, k_nope, k_rope, v)
    pool_l = multiscale_pool(p_lat[..., OFF_POOL:OFF_FFT], w_pool, s_pool)
    fft_l = fourier_mix(p_lat[..., OFF_FFT:OFF_GATE])
    out_lat = gated_merge(attn_l, pool_l, fft_l, p_lat[..., OFF_GATE:] + b_gate,
                          w_oa, w_ob, w_oc, w_out)
    if not with_ctx_out:
        return out_lat, None
    p_ctx_q = h_ctx @ w_in[:, OFF_Q:OFF_KV]
    p_ctx_rest = h_ctx @ w_in[:, OFF_POOL:]
    qc_nope, qc_rope = mla_queries(p_ctx_q, g_q, w_uq)
    attn_c = attend(qc_nope, qc_rope, k_nope_c, k_rope_c, v_c)
    pool_c = multiscale_pool(p_ctx_rest[..., :POOL_W], w_pool, s_pool)
    fft_c = fourier_mix(p_ctx_rest[..., POOL_W:POOL_W + FFT_W])
    out_ctx = gated_merge(attn_c, pool_c, fft_c, p_ctx_rest[..., POOL_W + FFT_W:] + b_gate,
                          w_oa, w_ob, w_oc, w_out)
    return out_lat, out_ctx


def swiglu(h, w1, w3, w2):
    return (jax.nn.silu(h @ w1) * (h @ w3)) @ w2


def moe_swiglu(h, w_router, b_router, w1, w3, w2):
    logits = (h @ w_router).astype(jnp.float32) + b_router.astype(jnp.float32)
    top_v, top_i = lax.top_k(logits, TOP_K)
    top_w = jax.nn.softmax(top_v, axis=-1)
    combine = jnp.sum(jax.nn.one_hot(top_i, N_EXPERTS, dtype=jnp.float32) * top_w[..., None],
                      axis=-2).astype(h.dtype)
    out = jnp.zeros_like(h)
    for e in range(N_EXPERTS):
        out = out + combine[..., e:e + 1] * swiglu(h, w1[e], w3[e], w2[e])
    return out


def setup_inputs(seed: int = 0) -> dict:
    key = jax.random.key(seed)
    ks = iter(jax.random.split(key, 40))
    D = D_MODEL

    def nrm(shape, scale):
        return jax.random.normal(next(ks), shape, jnp.float32) * scale

    def gain(shape):
        return 1.0 + nrm(shape, 0.02)

    return {
        'x': nrm((BATCH, SEQ, D), 1.0),
        'c': nrm((BATCH, D), 1.0),
        'ctx': nrm((BATCH, CTX_LEN, D), 1.0),
        'c_ctx': nrm((D,), 1.0),
        'w_mod': nrm((DEPTH, D, 6 * D), 0.5 * D ** -0.5),
        'b_mod': nrm((DEPTH, 6 * D), 0.02),
        'g_norm1': gain((DEPTH, D)),
        'w_in': nrm((DEPTH, D, IN_COLS), D ** -0.5),
        'b_gate': nrm((DEPTH, N_BRANCH * D), 0.02),
        'g_q': gain((DEPTH, Q_LORA)),
        'w_uq': nrm((DEPTH, Q_LORA, N_HEADS, QK_DIM), Q_LORA ** -0.5),
        'g_kv': gain((DEPTH, KV_LORA)),
        'w_ukv': nrm((DEPTH, KV_LORA, N_HEADS, QK_NOPE + V_DIM), KV_LORA ** -0.5),
        'w_pool': nrm((DEPTH, POOL_GROUPS, POOL_GC, POOL_GC), POOL_GC ** -0.5),
        's_pool': 1.0 + nrm((DEPTH, POOL_W), 0.1),
        'w_oa': nrm((DEPTH, ATTN_W, D), ATTN_W ** -0.5),
        'w_ob': nrm((DEPTH, POOL_W, D), POOL_W ** -0.5),
        'w_oc': nrm((DEPTH, FFT_W, D), FFT_W ** -0.5),
        'w_out': nrm((DEPTH, D, D), D ** -0.5),
        'g_norm2': gain((DEPTH, D)),
        'ffn_w1': nrm((N_DENSE, D, D_FF), D ** -0.5),
        'ffn_w3': nrm((N_DENSE, D, D_FF), D ** -0.5),
        'ffn_w2': nrm((N_DENSE, D_FF, D), D_FF ** -0.5),
        'moe_router': nrm((N_MOE, D, N_EXPERTS), D ** -0.5),
        'moe_bias': nrm((N_MOE, N_EXPERTS), 0.01),
        'moe_w1': nrm((N_MOE, N_EXPERTS, D, D_FF_EXPERT), D ** -0.5),
        'moe_w3': nrm((N_MOE, N_EXPERTS, D, D_FF_EXPERT), D ** -0.5),
        'moe_w2': nrm((N_MOE, N_EXPERTS, D_FF_EXPERT, D), D_FF_EXPERT ** -0.5),
        'g_final': gain((D,)),
    }


def reference(x, c, ctx, c_ctx, w_mod, b_mod, g_norm1, w_in, b_gate, g_q, w_uq, g_kv, w_ukv,
              w_pool, s_pool, w_oa, w_ob, w_oc, w_out, g_norm2, ffn_w1, ffn_w3, ffn_w2,
              moe_router, moe_bias, moe_w1, moe_w3, moe_w2, g_final):
    B, L, D = x.shape
    n_ctx = ctx.shape[1]
    rows = L // GRID_W
    cos, sin = axial_rope_tables(rows, x.dtype)
    xc = ctx
    cond_lat = jax.nn.silu(c)
    cond_ctx = jax.nn.silu(c_ctx)
    for layer in range(DEPTH):
        last = layer == DEPTH - 1
        mod_l = (cond_lat @ w_mod[layer] + b_mod[layer])[:, None, :]
        sh1, sc1, gt1, sh2, sc2, gt2 = jnp.split(mod_l, 6, axis=-1)
        n_mc = 2 if last else 6
        mod_c = cond_ctx @ w_mod[layer][:, :n_mc * D] + b_mod[layer][:n_mc * D]
        mods_c = jnp.split(mod_c, n_mc)
        h_l = modulate(rmsnorm(x, g_norm1[layer]), sh1, sc1)
        h_c = modulate(rmsnorm(xc, g_norm1[layer]), mods_c[0], mods_c[1])
        out_l, out_c = token_mixer(h_l, h_c, w_in[layer], b_gate[layer], g_q[layer], w_uq[layer],
                                   g_kv[layer], w_ukv[layer], w_pool[layer], s_pool[layer],
                                   w_oa[layer], w_ob[layer], w_oc[layer], w_out[layer],
                                   cos, sin, not last)
        x = x + gt1 * out_l
        if not last:
            xc = xc + mods_c[2] * out_c
        h_l = modulate(rmsnorm(x, g_norm2[layer]), sh2, sc2)
        if last:
            h = h_l
        else:
            h_c = modulate(rmsnorm(xc, g_norm2[layer]), mods_c[3], mods_c[4])
            h = jnp.concatenate([h_c, h_l], axis=1)
        if layer % 2 == 0:
            y = swiglu(h, ffn_w1[layer // 2], ffn_w3[layer // 2], ffn_w2[layer // 2])
        else:
            y = moe_swiglu(h, moe_router[layer // 2], moe_bias[layer // 2],
                           moe_w1[layer // 2], moe_w3[layer // 2], moe_w2[layer // 2])
        if last:
            x = x + gt2 * y
        else:
            x = x + gt2 * y[:, n_ctx:]
            xc = xc + mods_c[5] * y[:, :n_ctx]
    return rmsnorm(x, g_final)
```

```python
import functools
import math

import numpy as np
import jax
import jax.numpy as jnp
from jax import lax
from jax.experimental import pallas as pl
from jax.experimental.pallas import tpu as pltpu

GRID_WIDTH = 64
NUM_HEADS = 8
NOPE_DIM = 64
ROPE_DIM = 32
VAL_DIM = 64
ROPE_THETA = 10000.0
POOL_WINS = (2, 4, 8, 16)
GROUP_CH = 128
NUM_BRANCH = 3
TOPK = 2
NORM_EPS = 1e-6

LANES = 128
HEAD_PAD = 128
VT_ROWS = 80
POOL_HALO = 16
DFT_N2 = 128
VMEM_LIMIT = 56 * 1024 * 1024

BF16 = jnp.bfloat16
F32 = jnp.float32


def _cparams(sem):
    return pltpu.CompilerParams(dimension_semantics=sem, vmem_limit_bytes=VMEM_LIMIT)


def _full(shape):
    n = len(shape)
    return pl.BlockSpec(shape, lambda *_: (0,) * n)


def _dot(a, b):
    return jnp.dot(a, b, preferred_element_type=F32)


def _dot_nt(a, b):
    return lax.dot_general(a, b, (((1,), (1,)), ((), ())), preferred_element_type=F32)


def _rms(x, g):
    return x * lax.rsqrt(jnp.mean(x * x, axis=-1, keepdims=True) + NORM_EPS) * g


def _mod_kernel(c_ref, w_ref, b_ref, o_ref):
    c = c_ref[...]
    s = c * jax.nn.sigmoid(c)
    o_ref[0] = jnp.dot(s, w_ref[0], preferred_element_type=F32,
                       precision=lax.Precision.HIGHEST) + b_ref[0]


def modulation(cvec, w_mod, b_mod):
    depth, d, n = w_mod.shape
    tn = 1536
    return pl.pallas_call(
        _mod_kernel,
        grid=(depth, n // tn),
        in_specs=[pl.BlockSpec((8, d), lambda l, j: (0, 0)),
                  pl.BlockSpec((1, d, tn), lambda l, j: (l, 0, j)),
                  pl.BlockSpec((1, 1, tn), lambda l, j: (l, 0, j))],
        out_specs=pl.BlockSpec((1, 8, tn), lambda l, j: (l, 0, j)),
        out_shape=jax.ShapeDtypeStruct((depth, 8, n), F32),
        compiler_params=_cparams(("parallel", "parallel")),
        name="modulation",
    )(cvec, w_mod, b_mod.reshape(depth, 1, n))


def _proj_in_kernel(x_ref, mod_ref, g1_ref, tabq_ref, tabk_ref,
                    wa_ref, wpool_ref, wfft_ref, wgate_ref, bgate_ref,
                    gq_ref, wqn_ref, wqr_ref, eq_ref, gkv_ref, wkn_ref, ek_ref, wvt_ref, vone_ref,
                    *out_refs, q_lora, kv_lora, kv_only):
    if kv_only:
        k_ref, vt_ref = out_refs
    else:
        q_ref, k_ref, vt_ref, pool_ref, fft_ref, gate_ref = out_refs
    x = x_ref[...]
    mod = mod_ref[0]
    h = (_rms(x, g1_ref[...]) * (1.0 + mod[1:2]) + mod[0:1]).astype(BF16)
    pa = _dot(h, wa_ref[...])
    kvn = _rms(pa[:, q_lora:q_lora + kv_lora], gkv_ref[...]).astype(BF16)
    kr = (pa[:, q_lora + kv_lora:] * tabk_ref[...]).astype(BF16)
    k_all = (_dot(kvn, wkn_ref[...]) + _dot(kr, ek_ref[...])).astype(BF16)
    for hd in range(NUM_HEADS):
        k_ref[hd] = k_all[:, hd * HEAD_PAD:(hd + 1) * HEAD_PAD]
    vt_ref[0] = (_dot_nt(wvt_ref[...], kvn) + vone_ref[...]).astype(BF16)
    if kv_only:
        return
    qn = _rms(pa[:, :q_lora], gq_ref[...]).astype(BF16)
    qr = (_dot(qn, wqr_ref[...]) * tabq_ref[...]).astype(BF16)
    q_all = (_dot(qn, wqn_ref[...]) + _dot(qr, eq_ref[...])).astype(BF16)
    for hd in range(NUM_HEADS):
        q_ref[hd] = q_all[:, hd * HEAD_PAD:(hd + 1) * HEAD_PAD]
    pool_ref[...] = _dot(h, wpool_ref[...])
    fft_ref[...] = _dot(h, wfft_ref[...]).astype(BF16)
    gate_ref[...] = jax.nn.sigmoid(_dot(h, wgate_ref[...]) + bgate_ref[...]).astype(BF16)


def proj_in(x, modv, g1, tabq, tabk, lw, batch, kv_only, tm=256):
    t, d = x.shape
    seq = t // batch
    npb = seq // tm
    q_lora = lw["gq"].shape[1]
    kv_lora = lw["gkv"].shape[1]
    hw = NUM_HEADS * HEAD_PAD
    weights = [lw["wa"], lw["wpool"], lw["wfft"], lw["wgate"], lw["bgate"],
               lw["gq"], lw["wqn"], lw["wqr"], lw["eq"], lw["gkv"], lw["wkn"], lw["ek"],
               lw["wvt"], lw["vone"]]
    in_specs = [pl.BlockSpec((tm, d), lambda i: (i, 0)),
                pl.BlockSpec((1, 2, d), lambda i: (i // npb, 0, 0)),
                _full(g1.shape),
                pl.BlockSpec((tm, tabq.shape[1]), lambda i: (i % npb, 0)),
                pl.BlockSpec((tm, tabk.shape[1]), lambda i: (i % npb, 0))]
    in_specs += [_full(w.shape) for w in weights]
    head_spec = pl.BlockSpec((NUM_HEADS, tm, HEAD_PAD), lambda i: (0, i, 0))
    head_shape = jax.ShapeDtypeStruct((NUM_HEADS, t, HEAD_PAD), BF16)
    vt_spec = pl.BlockSpec((1, NUM_HEADS * VT_ROWS, tm), lambda i: (i // npb, 0, i % npb))
    vt_shape = jax.ShapeDtypeStruct((batch, NUM_HEADS * VT_ROWS, seq), BF16)
    if kv_only:
        out_specs = [head_spec, vt_spec]
        out_shape = [head_shape, vt_shape]
    else:
        pw, fw, gw = lw["wpool"].shape[1], lw["wfft"].shape[1], lw["wgate"].shape[1]
        out_specs = [head_spec, head_spec, vt_spec,
                     pl.BlockSpec((tm, pw), lambda i: (i, 0)),
                     pl.BlockSpec((tm, fw), lambda i: (i, 0)),
                     pl.BlockSpec((tm, gw), lambda i: (i, 0))]
        out_shape = [head_shape, head_shape, vt_shape,
                     jax.ShapeDtypeStruct((t, pw), F32),
                     jax.ShapeDtypeStruct((t, fw), BF16),
                     jax.ShapeDtypeStruct((t, gw), BF16)]
    return pl.pallas_call(
        functools.partial(_proj_in_kernel, q_lora=q_lora, kv_lora=kv_lora, kv_only=kv_only),
        grid=(t // tm,),
        in_specs=in_specs, out_specs=out_specs, out_shape=out_shape,
        compiler_params=_cparams(("parallel",)),
        name="proj_in_kv" if kv_only else "proj_in",
    )(x, modv, g1, tabq, tabk, *weights)


def _attn_kernel(q_ref, kc_ref, vtc_ref, *rest, tk, n_lat_tiles):
    if n_lat_tiles:
        kl_ref, vtl_ref, o_ref = rest
    else:
        (o_ref,) = rest
    tq = q_ref.shape[1]
    outs = []
    for j in range(2):
        q = q_ref[j]

        def step(k, vt, carry):
            m, acc = carry
            s = _dot_nt(k, q)
            m_new = jnp.maximum(m, jnp.max(s, axis=0, keepdims=True))
            p = jnp.exp2(s - m_new).astype(BF16)
            alpha = jnp.exp2(m - m_new)
            return m_new, acc * alpha + _dot(vt, p)

        carry = (jnp.full((1, tq), -1e30, F32), jnp.zeros((VT_ROWS, tq), F32))
        carry = step(kc_ref[j], vtc_ref[0, j * VT_ROWS:(j + 1) * VT_ROWS, :], carry)
        if n_lat_tiles:
            def body(i, c, j=j):
                off = pl.multiple_of(i * tk, tk)
                return step(kl_ref[j, pl.ds(off, tk), :],
                            vtl_ref[0, j * VT_ROWS:(j + 1) * VT_ROWS, pl.ds(off, tk)], c)
            carry = lax.fori_loop(0, n_lat_tiles, body, carry)
        _, acc = carry
        outs.append(acc[:VAL_DIM] / acc[VAL_DIM:VAL_DIM + 1])
    o_ref[...] = jnp.concatenate(outs, axis=0).T.astype(BF16)


def attention(q, kc, vtc, kl, vtl, batch, tq=256, tk=512):
    h, t, _ = q.shape
    lq = t // batch
    nq = lq // tq
    lc = kc.shape[1] // batch
    in_specs = [pl.BlockSpec((2, tq, HEAD_PAD), lambda b, hp, i: (hp, b * nq + i, 0)),
                pl.BlockSpec((2, lc, HEAD_PAD), lambda b, hp, i: (hp, b, 0)),
                pl.BlockSpec((1, 2 * VT_ROWS, lc), lambda b, hp, i: (b, hp, 0))]
    args = [q, kc, vtc]
    n_lat_tiles = 0
    if kl is not None:
        ll = kl.shape[1] // batch
        tk = min(tk, ll)
        n_lat_tiles = ll // tk
        in_specs += [pl.BlockSpec((2, ll, HEAD_PAD), lambda b, hp, i: (hp, b, 0)),
                     pl.BlockSpec((1, 2 * VT_ROWS, ll), lambda b, hp, i: (b, hp, 0))]
        args += [kl, vtl]
    return pl.pallas_call(
        functools.partial(_attn_kernel, tk=tk, n_lat_tiles=n_lat_tiles),
        grid=(batch, h // 2, nq),
        in_specs=in_specs,
        out_specs=pl.BlockSpec((tq, 2 * VAL_DIM), lambda b, hp, i: (b * nq + i, hp)),
        out_shape=jax.ShapeDtypeStruct((t, h * VAL_DIM), BF16),
        compiler_params=_cparams(("parallel", "parallel", "arbitrary")),
        name="attention",
    )(*args)


def _pool_kernel(x_ref, prev_ref, next_ref, band_ref, wp_ref, sp_ref, o_ref, *, seq, tm):
    i = pl.program_id(0)
    npb = seq // tm
    pos0 = (i % npb) * tm
    has_prev = (pos0 > 0).astype(F32)
    has_next = (pos0 + tm < seq).astype(F32)
    x = x_ref[...]
    ext = jnp.concatenate([prev_ref[...] * has_prev, x, next_ref[...] * has_next], axis=0).astype(BF16)
    t = pos0 + lax.broadcasted_iota(jnp.int32, (tm, GROUP_CH), 0)
    outs = []
    for g, win in enumerate(POOL_WINS):
        left = win // 2
        right = win - 1 - left
        cnt = (jnp.minimum(t + right + 1, seq) - jnp.maximum(t - left, 0)).astype(F32)
        sl = slice(g * GROUP_CH, (g + 1) * GROUP_CH)
        wsum = _dot(band_ref[g], ext[:, sl])
        mixed = (wsum / cnt - x[:, sl]).astype(BF16)
        outs.append(_dot(mixed, wp_ref[g]))
    o_ref[...] = (jnp.concatenate(outs, axis=1) * sp_ref[...]).astype(BF16)


def _pool_bands(tm):
    r = np.arange(tm)[:, None]
    c = np.arange(tm + 2 * POOL_HALO)[None, :]
    d = c - POOL_HALO - r
    bands = [((d >= -(w // 2)) & (d <= w - 1 - w // 2)) for w in POOL_WINS]
    return jnp.asarray(np.stack(bands).astype(np.float32), dtype=BF16)


def pool_mixer(p, wp, sp, batch, tm=256):
    t, w = p.shape
    seq = t // batch
    tm = min(tm, seq)
    hb = tm // POOL_HALO
    nh = t // POOL_HALO
    return pl.pallas_call(
        functools.partial(_pool_kernel, seq=seq, tm=tm),
        grid=(t // tm,),
        in_specs=[pl.BlockSpec((tm, w), lambda i: (i, 0)),
                  pl.BlockSpec((POOL_HALO, w), lambda i: (jnp.maximum(i * hb - 1, 0), 0)),
                  pl.BlockSpec((POOL_HALO, w), lambda i: (jnp.minimum((i + 1) * hb, nh - 1), 0)),
                  _full((len(POOL_WINS), tm, tm + 2 * POOL_HALO)),
                  _full(wp.shape), _full(sp.shape)],
        out_specs=pl.BlockSpec((tm, w), lambda i: (i, 0)),
        out_shape=jax.ShapeDtypeStruct((t, w), BF16),
        compiler_params=_cparams(("parallel",)),
        name="pool_mixer",
    )(p, p, p, _pool_bands(tm), wp, sp)


def _dft_outer_kernel(f_ref, x_ref, o_ref):
    o_ref[0] = _dot(f_ref[...], x_ref[0]).astype(BF16)


def _dft_inner_kernel(a_ref, g_ref, cs_ref, o_ref, *, scale):
    n2 = g_ref.shape[1] // 2
    rhs = a_ref[0].reshape(2 * n2, a_ref.shape[-1])
    p = _dot(g_ref[0], rhs)
    outs = []
    for g in range(p.shape[1] // GROUP_CH):
        sl = slice(g * GROUP_CH, (g + 1) * GROUP_CH)
        lhs = jnp.concatenate([p[:n2, sl], p[n2:, sl]], axis=1).astype(BF16)
        outs.append(_dot(lhs, cs_ref[...]))
    o_ref[0] = (jnp.concatenate(outs, axis=1) * scale).astype(BF16)


def _dft_tables(seq):
    n2 = seq if seq <= 2 * DFT_N2 else DFT_N2
    n1 = seq // n2
    k2 = jnp.arange(n2, dtype=jnp.int32)[:, None]
    t2 = jnp.arange(n2, dtype=jnp.int32)[None, :]
    k1 = jnp.arange(n1, dtype=jnp.int32)[:, None, None]
    ang = ((t2 * (k2 * n1 + k1)) % seq).astype(F32) * (2.0 * math.pi / seq)
    gr, gi = jnp.cos(ang), -jnp.sin(ang)
    gmat = jnp.concatenate([jnp.concatenate([gr, -gi], axis=2),
                            jnp.concatenate([gi, gr], axis=2)], axis=1).astype(BF16)
    a1 = jnp.arange(n1, dtype=jnp.int32)
    ang1 = ((a1[:, None] * a1[None, :]) % n1).astype(F32) * (2.0 * math.pi / n1)
    f1 = jnp.concatenate([jnp.cos(ang1), -jnp.sin(ang1)], axis=0).astype(BF16)
    c = jnp.arange(GROUP_CH, dtype=jnp.int32)
    angc = ((c[:, None] * c[None, :]) % GROUP_CH).astype(F32) * (2.0 * math.pi / GROUP_CH)
    cs = jnp.concatenate([jnp.cos(angc), jnp.sin(angc)], axis=0).astype(BF16)
    return gmat, f1, cs


def fourier_mixer(f, batch, tables):
    gmat, f1, cs = tables
    t, w = f.shape
    seq = t // batch
    n1, n2x2, _ = gmat.shape
    n2 = n2x2 // 2
    scale = 1.0 / math.sqrt(seq * GROUP_CH)
    if n1 > 1:
        cols = n2 * w
        tc = min(cols, 4096)
        a = pl.pallas_call(
            _dft_outer_kernel,
            grid=(batch, cols // tc),
            in_specs=[_full(f1.shape), pl.BlockSpec((1, n1, tc), lambda b, j: (b, 0, j))],
            out_specs=pl.BlockSpec((1, 2 * n1, tc), lambda b, j: (b, 0, j)),
            out_shape=jax.ShapeDtypeStruct((batch, 2 * n1, cols), BF16),
            compiler_params=_cparams(("parallel", "parallel")),
            name="dft_outer",
        )(f1, f.reshape(batch, n1, cols))
        a = a.reshape(batch, 2, n1, n2, w)
    else:
        fr = f.reshape(batch, 1, 1, n2, w)
        a = jnp.concatenate([fr, jnp.zeros_like(fr)], axis=1)
    out = pl.pallas_call(
        functools.partial(_dft_inner_kernel, scale=scale),
        grid=(batch, n1),
        in_specs=[pl.BlockSpec((1, 2, 1, n2, w), lambda b, k: (b, 0, k, 0, 0)),
                  pl.BlockSpec((1, 2 * n2, 2 * n2), lambda b, k: (k, 0, 0)),
                  _full(cs.shape)],
        out_specs=pl.BlockSpec((1, n2, w), lambda b, k: (b, 0, k)),
        out_shape=jax.ShapeDtypeStruct((batch, n2, n1 * w), BF16),
        compiler_params=_cparams(("parallel", "parallel")),
        name="dft_inner",
    )(a, gmat, cs)
    return out.reshape(t, w)


def _merge_kernel(a_ref, p_ref, f_ref, g_ref, x_ref, mod_ref, woa_ref, wob_ref, woc_ref, wout_ref,
                  g2_ref, xo_ref, h_ref):
    d = x_ref.shape[1]
    gate = g_ref[...].astype(F32)
    y = (gate[:, :d] * _dot(a_ref[...], woa_ref[...])
         + gate[:, d:2 * d] * _dot(p_ref[...], wob_ref[...])
         + gate[:, 2 * d:] * _dot(f_ref[...], woc_ref[...]))
    mod = mod_ref[0]
    xn = x_ref[...] + mod[0:1] * _dot(y.astype(BF16), wout_ref[...])
    xo_ref[...] = xn
    h_ref[...] = (_rms(xn, g2_ref[...]) * (1.0 + mod[2:3]) + mod[1:2]).astype(BF16)


def merge(attn, pooled, four, gate, x, modv, lw, g2, batch, tm=256):
    t, d = x.shape
    npb = (t // batch) // tm
    row = lambda w: pl.BlockSpec((tm, w), lambda i: (i, 0))
    weights = [lw["woa"], lw["wob"], lw["woc"], lw["wout"], g2]
    return pl.pallas_call(
        _merge_kernel,
        grid=(t // tm,),
        in_specs=[row(attn.shape[1]), row(pooled.shape[1]), row(four.shape[1]), row(gate.shape[1]),
                  row(d), pl.BlockSpec((1, 3, d), lambda i: (i // npb, 0, 0))]
                 + [_full(w.shape) for w in weights],
        out_specs=[row(d), row(d)],
        out_shape=[jax.ShapeDtypeStruct((t, d), F32), jax.ShapeDtypeStruct((t, d), BF16)],
        compiler_params=_cparams(("parallel",)),
        name="merge",
    )(attn, pooled, four, gate, x, modv, *weights)


def _ffn_kernel(h_ref, x_ref, gt_ref, comb_ref, w1_ref, w3_ref, w2_ref, o_ref):
    h = h_ref[...]
    a = _dot(h, w1_ref[...])
    b = _dot(h, w3_ref[...])
    g = (a * jax.nn.sigmoid(a) * b).astype(BF16)
    y = _dot(g, w2_ref[...])
    o_ref[...] = x_ref[...] + (gt_ref[0] * comb_ref[...]) * y


def ffn(h, x, gt, comb, w1, w3, w2, batch, tm=256):
    t, d = x.shape
    npb = (t // batch) // tm
    return pl.pallas_call(
        _ffn_kernel,
        grid=(t // tm,),
        in_specs=[pl.BlockSpec((tm, d), lambda i: (i, 0)),
                  pl.BlockSpec((tm, d), lambda i: (i, 0)),
                  pl.BlockSpec((1, 1, d), lambda i: (i // npb, 0, 0)),
                  pl.BlockSpec((tm, 1), lambda i: (i, 0)),
                  _full(w1.shape), _full(w3.shape), _full(w2.shape)],
        out_specs=pl.BlockSpec((tm, d), lambda i: (i, 0)),
        out_shape=jax.ShapeDtypeStruct((t, d), F32),
        compiler_params=_cparams(("parallel",)),
        name="ffn",
    )(h, x, gt, comb, w1, w3, w2)


def _router_kernel(x_ref, mod_ref, g2_ref, wr_ref, br_ref, o_ref):
    mod = mod_ref[0]
    h = _rms(x_ref[...], g2_ref[...]) * (1.0 + mod[1:2]) + mod[0:1]
    logits = lax.dot_general(wr_ref[...], h, (((1,), (1,)), ((), ())), preferred_element_type=F32,
                             precision=lax.Precision.HIGHEST) + br_ref[...]
    ne = logits.shape[0]
    eidx = lax.broadcasted_iota(jnp.int32, logits.shape, 0)
    m1 = jnp.max(logits, axis=0, keepdims=True)
    i1 = jnp.min(jnp.where(logits == m1, eidx, ne), axis=0, keepdims=True)
    sel1 = eidx == i1
    rest = jnp.where(sel1, -jnp.inf, logits)
    m2 = jnp.max(rest, axis=0, keepdims=True)
    i2 = jnp.min(jnp.where(rest == m2, eidx, ne), axis=0, keepdims=True)
    sel2 = eidx == i2
    e2 = jnp.exp(m2 - m1)
    w1 = 1.0 / (1.0 + e2)
    o_ref[...] = jnp.where(sel1, w1, 0.0) + jnp.where(sel2, e2 * w1, 0.0)


def router(x, modv, g2, wr_t, br, batch, tm=256):
    t, d = x.shape
    ne = wr_t.shape[0]
    npb = (t // batch) // tm
    return pl.pallas_call(
        _router_kernel,
        grid=(t // tm,),
        in_specs=[pl.BlockSpec((tm, d), lambda i: (i, 0)),
                  pl.BlockSpec((1, 2, d), lambda i: (i // npb, 0, 0)),
                  _full(g2.shape), _full(wr_t.shape), _full(br.shape)],
        out_specs=pl.BlockSpec((ne, tm), lambda i: (0, i)),
        out_shape=jax.ShapeDtypeStruct((ne, t), F32),
        compiler_params=_cparams(("parallel",)),
        name="router",
    )(x, modv, g2, wr_t, br)


def _final_norm_kernel(x_ref, g_ref, o_ref):
    o_ref[...] = _rms(x_ref[...], g_ref[...])


def final_norm(x, g, tm=512):
    t, d = x.shape
    return pl.pallas_call(
        _final_norm_kernel,
        grid=(t // tm,),
        in_specs=[pl.BlockSpec((tm, d), lambda i: (i, 0)), _full(g.shape)],
        out_specs=pl.BlockSpec((tm, d), lambda i: (i, 0)),
        out_shape=jax.ShapeDtypeStruct((t, d), F32),
        compiler_params=_cparams(("parallel",)),
        name="final_norm",
    )(x, g)


def _rope_tables(seq, rotate):
    half = ROPE_DIM // 2
    nf = ROPE_DIM // 4
    if rotate:
        t = jnp.arange(seq, dtype=jnp.int32)
        rows = (t // GRID_WIDTH).astype(F32)
        cols = (t % GRID_WIDTH).astype(F32)
        inv = ROPE_THETA ** (-jnp.arange(0, half, 2, dtype=F32) / half)
        ang = jnp.stack([rows[:, None] * inv, cols[:, None] * inv], axis=1)
        cos, sin = jnp.cos(ang), jnp.sin(ang)
    else:
        cos = jnp.ones((seq, 2, nf), F32)
        sin = jnp.zeros((seq, 2, nf), F32)
    cpat = jnp.concatenate([cos, cos], axis=2).reshape(seq, ROPE_DIM)
    spat = jnp.concatenate([-sin, sin], axis=2).reshape(seq, ROPE_DIM)
    return cpat, spat


def _rope_inputs(seq, rotate):
    cpat, spat = _rope_tables(seq, rotate)
    tabq = jnp.concatenate([jnp.tile(cpat, (1, NUM_HEADS)), jnp.tile(spat, (1, NUM_HEADS))], axis=1)
    tabk = jnp.concatenate([cpat, spat, jnp.zeros((seq, LANES - 2 * ROPE_DIM), F32)], axis=1)
    return tabq, tabk


def _swap_perm():
    j = np.arange(ROPE_DIM)
    half = ROPE_DIM // 4
    return np.where((j % (2 * half)) < half, j + half, j - half)


def _placement(shared):
    nsrc = ROPE_DIM if shared else NUM_HEADS * ROPE_DIM
    e = np.zeros((LANES if shared else 2 * nsrc, NUM_HEADS * HEAD_PAD), np.float32)
    for hd in range(NUM_HEADS):
        for j in range(ROPE_DIM):
            col = hd * HEAD_PAD + NOPE_DIM + j
            src = j if shared else hd * ROPE_DIM + j
            e[src, col] = 1.0
            e[nsrc + src, col] = 1.0
    return jnp.asarray(e, dtype=BF16)


def _prep_layer(w_in, b_gate, g_q, w_uq, g_kv, w_ukv, w_oa, w_ob, w_oc, w_out, w_pool, s_pool):
    d = w_in.shape[0]
    q_lora, kv_lora = g_q.shape[0], g_kv.shape[0]
    perm = _swap_perm()
    off_kr = q_lora + kv_lora
    off_pool = off_kr + ROPE_DIM
    pool_w = w_pool.shape[0] * w_pool.shape[1]
    off_fft = off_pool + pool_w
    off_gate = w_in.shape[1] - NUM_BRANCH * d
    kr = w_in[:, off_kr:off_pool]
    wa = jnp.concatenate([w_in[:, :off_kr], kr, kr[:, perm],
                          jnp.zeros((d, LANES - 2 * ROPE_DIM), F32)], axis=1)
    qscale = (NOPE_DIM + ROPE_DIM) ** -0.5 * math.log2(math.e)
    wq = w_uq * qscale
    wqn = jnp.pad(wq[:, :, :NOPE_DIM], ((0, 0), (0, 0), (0, HEAD_PAD - NOPE_DIM)))
    wqr = wq[:, :, NOPE_DIM:]
    wqr = jnp.concatenate([wqr.reshape(q_lora, -1), wqr[:, :, perm].reshape(q_lora, -1)], axis=1)
    wkn = jnp.pad(w_ukv[:, :, :NOPE_DIM], ((0, 0), (0, 0), (0, HEAD_PAD - NOPE_DIM)))
    wv = jnp.pad(w_ukv[:, :, NOPE_DIM:], ((0, 0), (0, 0), (0, VT_ROWS - VAL_DIM)))
    vone = np.zeros((NUM_HEADS, VT_ROWS, 1), np.float32)
    vone[:, VAL_DIM] = 1.0
    return dict(
        wa=wa.astype(BF16), wpool=w_in[:, off_pool:off_fft].astype(BF16),
        wfft=w_in[:, off_fft:off_gate].astype(BF16), wgate=w_in[:, off_gate:].astype(BF16),
        bgate=b_gate.reshape(1, -1),
        gq=g_q.reshape(1, -1), wqn=wqn.reshape(q_lora, -1).astype(BF16), wqr=wqr.astype(BF16),
        eq=_placement(False), gkv=g_kv.reshape(1, -1),
        wkn=wkn.reshape(kv_lora, -1).astype(BF16), ek=_placement(True),
        wvt=wv.reshape(kv_lora, -1).T.astype(BF16), vone=jnp.asarray(vone.reshape(-1, 1)),
        woa=w_oa.astype(BF16), wob=w_ob.astype(BF16), woc=w_oc.astype(BF16), wout=w_out.astype(BF16),
        wp=w_pool.astype(BF16), sp=s_pool.reshape(1, -1),
    )


def _token_mixer(xs, modv1, modv2, lw, g1, g2, ropes, dft, batch, kv_ctx, tm):
    tabq, tabk = ropes
    q, k, vt, pool_in, fft_in, gate = proj_in(xs, modv1, g1, tabq, tabk, lw, batch, False, tm)
    if kv_ctx is None:
        attn = attention(q, k, vt, None, None, batch)
    else:
        attn = attention(q, kv_ctx[0], kv_ctx[1], k, vt, batch)
    pooled = pool_mixer(pool_in, lw["wp"], lw["sp"], batch)
    four = fourier_mixer(fft_in, batch, dft)
    x_new, h2 = merge(attn, pooled, four, gate, xs, modv2, lw, g2, batch, tm)
    return x_new, h2, (k, vt)


def kernel(x, c, ctx, c_ctx, w_mod, b_mod, g_norm1, w_in, b_gate, g_q, w_uq, g_kv, w_ukv, w_pool, s_pool,
           w_oa, w_ob, w_oc, w_out, g_norm2, ffn_w1, ffn_w3, ffn_w2, moe_router, moe_bias,
           moe_w1, moe_w3, moe_w2, g_final):
    batch, seq, d = x.shape
    n_ctx = ctx.shape[1]
    depth = w_mod.shape[0]
    n_exp = moe_router.shape[-1] if moe_router.shape[0] else 0
    xl = x.reshape(batch * seq, d)
    xc = ctx.reshape(batch * n_ctx, d)

    cvec = jnp.concatenate([c, c_ctx[None], jnp.zeros((8 - batch - 1, d), F32)], axis=0)
    mods = modulation(cvec, w_mod, b_mod)

    rope_l = _rope_inputs(seq, True)
    rope_c = _rope_inputs(n_ctx, False)
    dft_l = _dft_tables(seq)
    dft_c = _dft_tables(n_ctx)
    ones_l = jnp.ones((batch * seq, 1), F32)
    ones_c = jnp.ones((batch * n_ctx, 1), F32)

    for layer in range(depth):
        last = layer == depth - 1
        lw = _prep_layer(w_in[layer], b_gate[layer], g_q[layer], w_uq[layer], g_kv[layer], w_ukv[layer],
                         w_oa[layer], w_ob[layer], w_oc[layer], w_out[layer], w_pool[layer], s_pool[layer])
        g1 = g_norm1[layer].reshape(1, d)
        g2 = g_norm2[layer].reshape(1, d)
        ml = mods[layer, :batch].reshape(batch, 6, d)
        mc = jnp.broadcast_to(mods[layer, batch].reshape(1, 6, d), (batch, 6, d))

        if last:
            k_c, vt_c = proj_in(xc, mc[:, 0:2], g1, rope_c[0], rope_c[1], lw, batch, True)
            kv_c = (k_c, vt_c)
        else:
            xc_new, h2_c, kv_c = _token_mixer(xc, mc[:, 0:2], mc[:, 2:5], lw, g1, g2, rope_c, dft_c,
                                              batch, None, 256)
        xl, h2_l, _ = _token_mixer(xl, ml[:, 0:2], ml[:, 2:5], lw, g1, g2, rope_l, dft_l, batch, kv_c, 256)
        if not last:
            xc = xc_new

        streams = [(xl, h2_l, ml, ones_l)]
        if not last:
            streams.append((xc, h2_c, mc, ones_c))
        new = []
        for xs, h2, mm, ones in streams:
            gt2 = mm[:, 5:6]
            if layer % 2 == 0:
                li = layer // 2
                xs = ffn(h2, xs, gt2, ones, ffn_w1[li].astype(BF16), ffn_w3[li].astype(BF16),
                         ffn_w2[li].astype(BF16), batch)
            else:
                li = layer // 2
                comb = router(xs, mm[:, 3:5], g2, moe_router[li].T, moe_bias[li].reshape(n_exp, 1), batch)
                comb = comb.T
                for e in range(n_exp):
                    xs = ffn(h2, xs, gt2, comb[:, e:e + 1], moe_w1[li, e].astype(BF16),
                             moe_w3[li, e].astype(BF16), moe_w2[li, e].astype(BF16), batch)
            new.append(xs)
        xl = new[0]
        if not last:
            xc = new[1]

    return final_norm(xl, g_final.reshape(1, d)).reshape(batch, seq, d)
```

```python
import functools
import math

import numpy as np
import jax
import jax.numpy as jnp
from jax import lax
from jax.experimental import pallas as pl
from jax.experimental.pallas import tpu as pltpu

GRID_WIDTH = 64
NUM_HEADS = 8
NOPE_DIM = 64
ROPE_DIM = 32
VAL_DIM = 64
ROPE_THETA = 10000.0
POOL_WINS = (2, 4, 8, 16)
GROUP_CH = 128
NUM_BRANCH = 3
TOPK = 2
NORM_EPS = 1e-6

LANES = 128
HEAD_PAD = 128
VT_ROWS = 80
POOL_HALO = 16
DFT_N2 = 128
VMEM_LIMIT = 56 * 1024 * 1024

BF16 = jnp.bfloat16
F32 = jnp.float32


def _cparams(sem):
    return pltpu.CompilerParams(dimension_semantics=sem, vmem_limit_bytes=VMEM_LIMIT)


def _full(shape):
    n = len(shape)
    return pl.BlockSpec(shape, lambda *_: (0,) * n)


def _dot(a, b):
    return jnp.dot(a, b, preferred_element_type=F32)


def _dot_nt(a, b):
    return lax.dot_general(a, b, (((1,), (1,)), ((), ())), preferred_element_type=F32)


def _rms(x, g):
    return x * lax.rsqrt(jnp.mean(x * x, axis=-1, keepdims=True) + NORM_EPS) * g


def _mod_kernel(c_ref, w_ref, b_ref, o_ref):
    c = c_ref[...]
    s = c * jax.nn.sigmoid(c)
    o_ref[0] = jnp.dot(s, w_ref[0], preferred_element_type=F32,
                       precision=lax.Precision.HIGHEST) + b_ref[0]


def modulation(cvec, w_mod, b_mod):
    depth, d, n = w_mod.shape
    tn = 1536
    return pl.pallas_call(
        _mod_kernel,
        grid=(depth, n // tn),
        in_specs=[pl.BlockSpec((8, d), lambda l, j: (0, 0)),
                  pl.BlockSpec((1, d, tn), lambda l, j: (l, 0, j)),
                  pl.BlockSpec((1, 1, tn), lambda l, j: (l, 0, j))],
        out_specs=pl.BlockSpec((1, 8, tn), lambda l, j: (l, 0, j)),
        out_shape=jax.ShapeDtypeStruct((depth, 8, n), F32),
        compiler_params=_cparams(("parallel", "parallel")),
        name="modulation",
    )(cvec, w_mod, b_mod.reshape(depth, 1, n))


def _proj_in_kernel(x_ref, mod_ref, g1_ref, tabq_ref, tabk_ref,
                    wa_ref, wpool_ref, wfft_ref, wgate_ref, bgate_ref,
                    gq_ref, wqn_ref, wqr_ref, eq_ref, gkv_ref, wkn_ref, ek_ref, wvt_ref, vone_ref,
                    *out_refs, q_lora, kv_lora, kv_only):
    if kv_only:
        k_ref, vt_ref = out_refs
    else:
        q_ref, k_ref, vt_ref, pool_ref, fft_ref, gate_ref = out_refs
    x = x_ref[...]
    mod = mod_ref[0]
    h = (_rms(x, g1_ref[...]) * (1.0 + mod[1:2]) + mod[0:1]).astype(BF16)
    pa = _dot(h, wa_ref[...])
    kvn = _rms(pa[:, q_lora:q_lora + kv_lora], gkv_ref[...]).astype(BF16)
    kr = (pa[:, q_lora + kv_lora:] * tabk_ref[...]).astype(BF16)
    k_all = (_dot(kvn, wkn_ref[...]) + _dot(kr, ek_ref[...])).astype(BF16)
    for hd in range(NUM_HEADS):
        k_ref[hd] = k_all[:, hd * HEAD_PAD:(hd + 1) * HEAD_PAD]
    vt_ref[0] = (_dot_nt(wvt_ref[...], kvn) + vone_ref[...]).astype(BF16)
    if kv_only:
        return
    qn = _rms(pa[:, :q_lora], gq_ref[...]).astype(BF16)
    qr = (_dot(qn, wqr_ref[...]) * tabq_ref[...]).astype(BF16)
    q_all = (_dot(qn, wqn_ref[...]) + _dot(qr, eq_ref[...])).astype(BF16)
    for hd in range(NUM_HEADS):
        q_ref[hd] = q_all[:, hd * HEAD_PAD:(hd + 1) * HEAD_PAD]
    pool_ref[...] = _dot(h, wpool_ref[...])
    fft_ref[...] = _dot(h, wfft_ref[...]).astype(BF16)
    gate_ref[...] = jax.nn.sigmoid(_dot(h, wgate_ref[...]) + bgate_ref[...]).astype(BF16)


def proj_in(x, modv, g1, tabq, tabk, lw, batch, kv_only, tm=256):
    t, d = x.shape
    seq = t // batch
    npb = seq // tm
    q_lora = lw["gq"].shape[1]
    kv_lora = lw["gkv"].shape[1]
    hw = NUM_HEADS * HEAD_PAD
    weights = [lw["wa"], lw["wpool"], lw["wfft"], lw["wgate"], lw["bgate"],
               lw["gq"], lw["wqn"], lw["wqr"], lw["eq"], lw["gkv"], lw["wkn"], lw["ek"],
               lw["wvt"], lw["vone"]]
    in_specs = [pl.BlockSpec((tm, d), lambda i: (i, 0)),
                pl.BlockSpec((1, 2, d), lambda i: (i // npb, 0, 0)),
                _full(g1.shape),
                pl.BlockSpec((tm, tabq.shape[1]), lambda i: (i % npb, 0)),
                pl.BlockSpec((tm, tabk.shape[1]), lambda i: (i % npb, 0))]
    in_specs += [_full(w.shape) for w in weights]
    head_spec = pl.BlockSpec((NUM_HEADS, tm, HEAD_PAD), lambda i: (0, i, 0))
    head_shape = jax.ShapeDtypeStruct((NUM_HEADS, t, HEAD_PAD), BF16)
    vt_spec = pl.BlockSpec((1, NUM_HEADS * VT_ROWS, tm), lambda i: (i // npb, 0, i % npb))
    vt_shape = jax.ShapeDtypeStruct((batch, NUM_HEADS * VT_ROWS, seq), BF16)
    if kv_only:
        out_specs = [head_spec, vt_spec]
        out_shape = [head_shape, vt_shape]
    else:
        pw, fw, gw = lw["wpool"].shape[1], lw["wfft"].shape[1], lw["wgate"].shape[1]
        out_specs = [head_spec, head_spec, vt_spec,
                     pl.BlockSpec((tm, pw), lambda i: (i, 0)),
                     pl.BlockSpec((tm, fw), lambda i: (i, 0)),
                     pl.BlockSpec((tm, gw), lambda i: (i, 0))]
        out_shape = [head_shape, head_shape, vt_shape,
                     jax.ShapeDtypeStruct((t, pw), F32),
                     jax.ShapeDtypeStruct((t, fw), BF16),
                     jax.ShapeDtypeStruct((t, gw), BF16)]
    return pl.pallas_call(
        functools.partial(_proj_in_kernel, q_lora=q_lora, kv_lora=kv_lora, kv_only=kv_only),
        grid=(t // tm,),
        in_specs=in_specs, out_specs=out_specs, out_shape=out_shape,
        compiler_params=_cparams(("parallel",)),
        name="proj_in_kv" if kv_only else "proj_in",
    )(x, modv, g1, tabq, tabk, *weights)


def _attn_kernel(q_ref, kc_ref, vtc_ref, *rest, tk, n_lat_tiles):
    if n_lat_tiles:
        kl_ref, vtl_ref, o_ref = rest
    else:
        (o_ref,) = rest
    nh = q_ref.shape[0]

    def rows(hd):
        return slice(hd * VT_ROWS, (hd + 1) * VT_ROWS)

    def tile(ks, vts, state):
        ss = [_dot_nt(ks[hd], q_ref[hd]) for hd in range(nh)]
        out = []
        for hd in range(nh):
            smax = jnp.max(ss[hd], axis=0, keepdims=True)
            if state is None:
                m_new = smax
                acc = _dot(vts[hd], jnp.exp2(ss[hd] - m_new).astype(BF16))
            else:
                m_old, acc_old = state[hd]
                m_new = jnp.maximum(m_old, smax)
                pv = _dot(vts[hd], jnp.exp2(ss[hd] - m_new).astype(BF16))
                acc = acc_old * jnp.exp2(m_old - m_new) + pv
            out.append((m_new, acc))
        return tuple(out)

    state = tile([kc_ref[hd] for hd in range(nh)],
                 [vtc_ref[0, rows(hd), :] for hd in range(nh)], None)
    if n_lat_tiles:
        def body(i, st):
            off = pl.multiple_of(i * tk, tk)
            return tile([kl_ref[hd, pl.ds(off, tk), :] for hd in range(nh)],
                        [vtl_ref[0, rows(hd), pl.ds(off, tk)] for hd in range(nh)], st)
        state = lax.fori_loop(0, n_lat_tiles, body, state)
    for pair in range(nh // 2):
        outs = []
        for hd in (2 * pair, 2 * pair + 1):
            acc = state[hd][1]
            outs.append(acc[:VAL_DIM] / acc[VAL_DIM:VAL_DIM + 1])
        o_ref[:, pair * 2 * VAL_DIM:(pair + 1) * 2 * VAL_DIM] = (
            jnp.concatenate(outs, axis=0).T.astype(BF16))


def attention(q, kc, vtc, kl, vtl, batch, tq=256, tk=512):
    h, t, _ = q.shape
    lq = t // batch
    nq = lq // tq
    lc = kc.shape[1] // batch
    once = pl.Buffered(1)
    in_specs = [pl.BlockSpec((h, tq, HEAD_PAD), lambda b, i: (0, b * nq + i, 0)),
                pl.BlockSpec((h, lc, HEAD_PAD), lambda b, i: (0, b, 0), pipeline_mode=once),
                pl.BlockSpec((1, h * VT_ROWS, lc), lambda b, i: (b, 0, 0), pipeline_mode=once)]
    args = [q, kc, vtc]
    n_lat_tiles = 0
    if kl is not None:
        ll = kl.shape[1] // batch
        tk = min(tk, ll)
        n_lat_tiles = ll // tk
        in_specs += [pl.BlockSpec((h, ll, HEAD_PAD), lambda b, i: (0, b, 0), pipeline_mode=once),
                     pl.BlockSpec((1, h * VT_ROWS, ll), lambda b, i: (b, 0, 0), pipeline_mode=once)]
        args += [kl, vtl]
    return pl.pallas_call(
        functools.partial(_attn_kernel, tk=tk, n_lat_tiles=n_lat_tiles),
        grid=(batch, nq),
        in_specs=in_specs,
        out_specs=pl.BlockSpec((tq, h * VAL_DIM), lambda b, i: (b * nq + i, 0)),
        out_shape=jax.ShapeDtypeStruct((t, h * VAL_DIM), BF16),
        compiler_params=_cparams(("parallel", "arbitrary")),
        name="attention",
    )(*args)


def _pool_kernel(x_ref, prev_ref, next_ref, band_ref, wp_ref, sp_ref, o_ref, *, seq, tm):
    i = pl.program_id(0)
    npb = seq // tm
    pos0 = (i % npb) * tm
    has_prev = (pos0 > 0).astype(F32)
    has_next = (pos0 + tm < seq).astype(F32)
    x = x_ref[...]
    ext = jnp.concatenate([prev_ref[...] * has_prev, x, next_ref[...] * has_next], axis=0).astype(BF16)
    t = pos0 + lax.broadcasted_iota(jnp.int32, (tm, GROUP_CH), 0)
    outs = []
    for g, win in enumerate(POOL_WINS):
        left = win // 2
        right = win - 1 - left
        cnt = (jnp.minimum(t + right + 1, seq) - jnp.maximum(t - left, 0)).astype(F32)
        sl = slice(g * GROUP_CH, (g + 1) * GROUP_CH)
        wsum = _dot(band_ref[g], ext[:, sl])
        mixed = (wsum / cnt - x[:, sl]).astype(BF16)
        outs.append(_dot(mixed, wp_ref[g]))
    o_ref[...] = (jnp.concatenate(outs, axis=1) * sp_ref[...]).astype(BF16)


def _pool_bands(tm):
    r = np.arange(tm)[:, None]
    c = np.arange(tm + 2 * POOL_HALO)[None, :]
    d = c - POOL_HALO - r
    bands = [((d >= -(w // 2)) & (d <= w - 1 - w // 2)) for w in POOL_WINS]
    return jnp.asarray(np.stack(bands).astype(np.float32), dtype=BF16)


def pool_mixer(p, wp, sp, batch, tm=256):
    t, w = p.shape
    seq = t // batch
    tm = min(tm, seq)
    hb = tm // POOL_HALO
    nh = t // POOL_HALO
    return pl.pallas_call(
        functools.partial(_pool_kernel, seq=seq, tm=tm),
        grid=(t // tm,),
        in_specs=[pl.BlockSpec((tm, w), lambda i: (i, 0)),
                  pl.BlockSpec((POOL_HALO, w), lambda i: (jnp.maximum(i * hb - 1, 0), 0)),
                  pl.BlockSpec((POOL_HALO, w), lambda i: (jnp.minimum((i + 1) * hb, nh - 1), 0)),
                  _full((len(POOL_WINS), tm, tm + 2 * POOL_HALO)),
                  _full(wp.shape), _full(sp.shape)],
        out_specs=pl.BlockSpec((tm, w), lambda i: (i, 0)),
        out_shape=jax.ShapeDtypeStruct((t, w), BF16),
        compiler_params=_cparams(("parallel",)),
        name="pool_mixer",
    )(p, p, p, _pool_bands(tm), wp, sp)


def _dft_outer_kernel(f_ref, x_ref, o_ref):
    o_ref[0] = _dot(f_ref[...], x_ref[0]).astype(BF16)


def _dft_inner_kernel(a_ref, g_ref, cs_ref, o_ref, *, scale):
    n2 = g_ref.shape[1] // 2
    rhs = a_ref[0].reshape(2 * n2, a_ref.shape[-1])
    p = _dot(g_ref[0], rhs)
    outs = []
    for g in range(p.shape[1] // GROUP_CH):
        sl = slice(g * GROUP_CH, (g + 1) * GROUP_CH)
        lhs = jnp.concatenate([p[:n2, sl], p[n2:, sl]], axis=1).astype(BF16)
        outs.append(_dot(lhs, cs_ref[...]))
    o_ref[0] = (jnp.concatenate(outs, axis=1) * scale).astype(BF16)


def _dft_tables(seq):
    n2 = seq if seq <= 2 * DFT_N2 else DFT_N2
    n1 = seq // n2
    k2 = jnp.arange(n2, dtype=jnp.int32)[:, None]
    t2 = jnp.arange(n2, dtype=jnp.int32)[None, :]
    k1 = jnp.arange(n1, dtype=jnp.int32)[:, None, None]
    ang_a = ((t2 * k2) % n2).astype(F32) * (2.0 * math.pi / n2)
    ang_b = ((t2 * k1) % seq).astype(F32) * (2.0 * math.pi / seq)
    ca, sa, cb, sb = jnp.cos(ang_a), jnp.sin(ang_a), jnp.cos(ang_b), jnp.sin(ang_b)
    gr = ca * cb - sa * sb
    gi = -(sa * cb + ca * sb)
    gmat = jnp.concatenate([jnp.concatenate([gr, -gi], axis=2),
                            jnp.concatenate([gi, gr], axis=2)], axis=1).astype(BF16)
    a1 = jnp.arange(n1, dtype=jnp.int32)
    ang1 = ((a1[:, None] * a1[None, :]) % n1).astype(F32) * (2.0 * math.pi / n1)
    f1 = jnp.concatenate([jnp.cos(ang1), -jnp.sin(ang1)], axis=0).astype(BF16)
    c = jnp.arange(GROUP_CH, dtype=jnp.int32)
    angc = ((c[:, None] * c[None, :]) % GROUP_CH).astype(F32) * (2.0 * math.pi / GROUP_CH)
    cs = jnp.concatenate([jnp.cos(angc), jnp.sin(angc)], axis=0).astype(BF16)
    return gmat, f1, cs


def fourier_mixer(f, batch, tables):
    gmat, f1, cs = tables
    t, w = f.shape
    seq = t // batch
    n1, n2x2, _ = gmat.shape
    n2 = n2x2 // 2
    scale = 1.0 / math.sqrt(seq * GROUP_CH)
    if n1 > 1:
        cols = n2 * w
        tc = min(cols, 4096)
        a = pl.pallas_call(
            _dft_outer_kernel,
            grid=(batch, cols // tc),
            in_specs=[_full(f1.shape), pl.BlockSpec((1, n1, tc), lambda b, j: (b, 0, j))],
            out_specs=pl.BlockSpec((1, 2 * n1, tc), lambda b, j: (b, 0, j)),
            out_shape=jax.ShapeDtypeStruct((batch, 2 * n1, cols), BF16),
            compiler_params=_cparams(("parallel", "parallel")),
            name="dft_outer",
        )(f1, f.reshape(batch, n1, cols))
        a = a.reshape(batch, 2, n1, n2, w)
    else:
        fr = f.reshape(batch, 1, 1, n2, w)
        a = jnp.concatenate([fr, jnp.zeros_like(fr)], axis=1)
    out = pl.pallas_call(
        functools.partial(_dft_inner_kernel, scale=scale),
        grid=(batch, n1),
        in_specs=[pl.BlockSpec((1, 2, 1, n2, w), lambda b, k: (b, 0, k, 0, 0)),
                  pl.BlockSpec((1, 2 * n2, 2 * n2), lambda b, k: (k, 0, 0)),
                  _full(cs.shape)],
        out_specs=pl.BlockSpec((1, n2, w), lambda b, k: (b, 0, k)),
        out_shape=jax.ShapeDtypeStruct((batch, n2, n1 * w), BF16),
        compiler_params=_cparams(("parallel", "parallel")),
        name="dft_inner",
    )(a, gmat, cs)
    return out.reshape(t, w)


def _merge_kernel(a_ref, p_ref, f_ref, g_ref, x_ref, mod_ref, woa_ref, wob_ref, woc_ref, wout_ref,
                  g2_ref, xo_ref, h_ref):
    d = x_ref.shape[1]
    gate = g_ref[...].astype(F32)
    y = (gate[:, :d] * _dot(a_ref[...], woa_ref[...])
         + gate[:, d:2 * d] * _dot(p_ref[...], wob_ref[...])
         + gate[:, 2 * d:] * _dot(f_ref[...], woc_ref[...]))
    mod = mod_ref[0]
    xn = x_ref[...] + mod[0:1] * _dot(y.astype(BF16), wout_ref[...])
    xo_ref[...] = xn
    h_ref[...] = (_rms(xn, g2_ref[...]) * (1.0 + mod[2:3]) + mod[1:2]).astype(BF16)


def merge(attn, pooled, four, gate, x, modv, lw, g2, batch, tm=256):
    t, d = x.shape
    npb = (t // batch) // tm
    row = lambda w: pl.BlockSpec((tm, w), lambda i: (i, 0))
    weights = [lw["woa"], lw["wob"], lw["woc"], lw["wout"], g2]
    return pl.pallas_call(
        _merge_kernel,
        grid=(t // tm,),
        in_specs=[row(attn.shape[1]), row(pooled.shape[1]), row(four.shape[1]), row(gate.shape[1]),
                  row(d), pl.BlockSpec((1, 3, d), lambda i: (i // npb, 0, 0))]
                 + [_full(w.shape) for w in weights],
        out_specs=[row(d), row(d)],
        out_shape=[jax.ShapeDtypeStruct((t, d), F32), jax.ShapeDtypeStruct((t, d), BF16)],
        compiler_params=_cparams(("parallel",)),
        name="merge",
    )(attn, pooled, four, gate, x, modv, *weights)


def _ffn_kernel(h_ref, x_ref, gt_ref, comb_ref, w1_ref, w3_ref, w2_ref, o_ref):
    h = h_ref[...]
    a = _dot(h, w1_ref[...])
    b = _dot(h, w3_ref[...])
    g = (a * jax.nn.sigmoid(a) * b).astype(BF16)
    y = _dot(g, w2_ref[...])
    o_ref[...] = x_ref[...] + (gt_ref[0] * comb_ref[...]) * y


def ffn(h, x, gt, comb, w1, w3, w2, batch, tm=256):
    t, d = x.shape
    npb = (t // batch) // tm
    return pl.pallas_call(
        _ffn_kernel,
        grid=(t // tm,),
        in_specs=[pl.BlockSpec((tm, d), lambda i: (i, 0)),
                  pl.BlockSpec((tm, d), lambda i: (i, 0)),
                  pl.BlockSpec((1, 1, d), lambda i: (i // npb, 0, 0)),
                  pl.BlockSpec((tm, 1), lambda i: (i, 0)),
                  _full(w1.shape), _full(w3.shape), _full(w2.shape)],
        out_specs=pl.BlockSpec((tm, d), lambda i: (i, 0)),
        out_shape=jax.ShapeDtypeStruct((t, d), F32),
        compiler_params=_cparams(("parallel",)),
        name="ffn",
    )(h, x, gt, comb, w1, w3, w2)


def _router_kernel(x_ref, mod_ref, g2_ref, wr_ref, br_ref, o_ref):
    mod = mod_ref[0]
    h = _rms(x_ref[...], g2_ref[...]) * (1.0 + mod[1:2]) + mod[0:1]
    logits = lax.dot_general(wr_ref[...], h, (((1,), (1,)), ((), ())), preferred_element_type=F32,
                             precision=lax.Precision.HIGHEST) + br_ref[...]
    ne = logits.shape[0]
    eidx = lax.broadcasted_iota(jnp.int32, logits.shape, 0)
    m1 = jnp.max(logits, axis=0, keepdims=True)
    i1 = jnp.min(jnp.where(logits == m1, eidx, ne), axis=0, keepdims=True)
    sel1 = eidx == i1
    rest = jnp.where(sel1, -jnp.inf, logits)
    m2 = jnp.max(rest, axis=0, keepdims=True)
    i2 = jnp.min(jnp.where(rest == m2, eidx, ne), axis=0, keepdims=True)
    sel2 = eidx == i2
    e2 = jnp.exp(m2 - m1)
    w1 = 1.0 / (1.0 + e2)
    o_ref[...] = jnp.where(sel1, w1, 0.0) + jnp.where(sel2, e2 * w1, 0.0)


def router(x, modv, g2, wr_t, br, batch, tm=256):
    t, d = x.shape
    ne = wr_t.shape[0]
    npb = (t // batch) // tm
    return pl.pallas_call(
        _router_kernel,
        grid=(t // tm,),
        in_specs=[pl.BlockSpec((tm, d), lambda i: (i, 0)),
                  pl.BlockSpec((1, 2, d), lambda i: (i // npb, 0, 0)),
                  _full(g2.shape), _full(wr_t.shape), _full(br.shape)],
        out_specs=pl.BlockSpec((ne, tm), lambda i: (0, i)),
        out_shape=jax.ShapeDtypeStruct((ne, t), F32),
        compiler_params=_cparams(("parallel",)),
        name="router",
    )(x, modv, g2, wr_t, br)


def _final_norm_kernel(x_ref, g_ref, o_ref):
    o_ref[...] = _rms(x_ref[...], g_ref[...])


def final_norm(x, g, tm=512):
    t, d = x.shape
    return pl.pallas_call(
        _final_norm_kernel,
        grid=(t // tm,),
        in_specs=[pl.BlockSpec((tm, d), lambda i: (i, 0)), _full(g.shape)],
        out_specs=pl.BlockSpec((tm, d), lambda i: (i, 0)),
        out_shape=jax.ShapeDtypeStruct((t, d), F32),
        compiler_params=_cparams(("parallel",)),
        name="final_norm",
    )(x, g)


def _rope_tables(seq, rotate):
    half = ROPE_DIM // 2
    nf = ROPE_DIM // 4
    if rotate:
        t = jnp.arange(seq, dtype=jnp.int32)
        rows = (t // GRID_WIDTH).astype(F32)
        cols = (t % GRID_WIDTH).astype(F32)
        inv = ROPE_THETA ** (-jnp.arange(0, half, 2, dtype=F32) / half)
        ang = jnp.stack([rows[:, None] * inv, cols[:, None] * inv], axis=1)
        cos, sin = jnp.cos(ang), jnp.sin(ang)
    else:
        cos = jnp.ones((seq, 2, nf), F32)
        sin = jnp.zeros((seq, 2, nf), F32)
    cpat = jnp.concatenate([cos, cos], axis=2).reshape(seq, ROPE_DIM)
    spat = jnp.concatenate([-sin, sin], axis=2).reshape(seq, ROPE_DIM)
    return cpat, spat


def _rope_inputs(seq, rotate):
    cpat, spat = _rope_tables(seq, rotate)
    tabq = jnp.concatenate([jnp.tile(cpat, (1, NUM_HEADS)), jnp.tile(spat, (1, NUM_HEADS))], axis=1)
    tabk = jnp.concatenate([cpat, spat, jnp.zeros((seq, LANES - 2 * ROPE_DIM), F32)], axis=1)
    return tabq, tabk


def _swap_perm():
    j = np.arange(ROPE_DIM)
    half = ROPE_DIM // 4
    return np.where((j % (2 * half)) < half, j + half, j - half)


def _placement(shared):
    nsrc = ROPE_DIM if shared else NUM_HEADS * ROPE_DIM
    e = np.zeros((LANES if shared else 2 * nsrc, NUM_HEADS * HEAD_PAD), np.float32)
    for hd in range(NUM_HEADS):
        for j in range(ROPE_DIM):
            col = hd * HEAD_PAD + NOPE_DIM + j
            src = j if shared else hd * ROPE_DIM + j
            e[src, col] = 1.0
            e[nsrc + src, col] = 1.0
    return jnp.asarray(e, dtype=BF16)


def _prep_layer(w_in, b_gate, g_q, w_uq, g_kv, w_ukv, w_oa, w_ob, w_oc, w_out, w_pool, s_pool):
    d = w_in.shape[0]
    q_lora, kv_lora = g_q.shape[0], g_kv.shape[0]
    perm = _swap_perm()
    off_kr = q_lora + kv_lora
    off_pool = off_kr + ROPE_DIM
    pool_w = w_pool.shape[0] * w_pool.shape[1]
    off_fft = off_pool + pool_w
    off_gate = w_in.shape[1] - NUM_BRANCH * d
    kr = w_in[:, off_kr:off_pool]
    wa = jnp.concatenate([w_in[:, :off_kr], kr, kr[:, perm],
                          jnp.zeros((d, LANES - 2 * ROPE_DIM), F32)], axis=1)
    qscale = (NOPE_DIM + ROPE_DIM) ** -0.5 * math.log2(math.e)
    wq = w_uq * qscale
    wqn = jnp.pad(wq[:, :, :NOPE_DIM], ((0, 0), (0, 0), (0, HEAD_PAD - NOPE_DIM)))
    wqr = wq[:, :, NOPE_DIM:]
    wqr = jnp.concatenate([wqr.reshape(q_lora, -1), wqr[:, :, perm].reshape(q_lora, -1)], axis=1)
    wkn = jnp.pad(w_ukv[:, :, :NOPE_DIM], ((0, 0), (0, 0), (0, HEAD_PAD - NOPE_DIM)))
    wv = jnp.pad(w_ukv[:, :, NOPE_DIM:], ((0, 0), (0, 0), (0, VT_ROWS - VAL_DIM)))
    vone = np.zeros((NUM_HEADS, VT_ROWS, 1), np.float32)
    vone[:, VAL_DIM] = 1.0
    return dict(
        wa=wa.astype(BF16), wpool=w_in[:, off_pool:off_fft].astype(BF16),
        wfft=w_in[:, off_fft:off_gate].astype(BF16), wgate=w_in[:, off_gate:].astype(BF16),
        bgate=b_gate.reshape(1, -1),
        gq=g_q.reshape(1, -1), wqn=wqn.reshape(q_lora, -1).astype(BF16), wqr=wqr.astype(BF16),
        eq=_placement(False), gkv=g_kv.reshape(1, -1),
        wkn=wkn.reshape(kv_lora, -1).astype(BF16), ek=_placement(True),
        wvt=wv.reshape(kv_lora, -1).T.astype(BF16), vone=jnp.asarray(vone.reshape(-1, 1)),
        woa=w_oa.astype(BF16), wob=w_ob.astype(BF16), woc=w_oc.astype(BF16), wout=w_out.astype(BF16),
        wp=w_pool.astype(BF16), sp=s_pool.reshape(1, -1),
    )


def _token_mixer(xs, modv1, modv2, lw, g1, g2, ropes, dft, batch, kv_ctx, tm):
    tabq, tabk = ropes
    q, k, vt, pool_in, fft_in, gate = proj_in(xs, modv1, g1, tabq, tabk, lw, batch, False, tm)
    if kv_ctx is None:
        attn = attention(q, k, vt, None, None, batch)
    else:
        attn = attention(q, kv_ctx[0], kv_ctx[1], k, vt, batch)
    pooled = pool_mixer(pool_in, lw["wp"], lw["sp"], batch)
    four = fourier_mixer(fft_in, batch, dft)
    x_new, h2 = merge(attn, pooled, four, gate, xs, modv2, lw, g2, batch, tm)
    return x_new, h2, (k, vt)


def kernel(x, c, ctx, c_ctx, w_mod, b_mod, g_norm1, w_in, b_gate, g_q, w_uq, g_kv, w_ukv, w_pool, s_pool,
           w_oa, w_ob, w_oc, w_out, g_norm2, ffn_w1, ffn_w3, ffn_w2, moe_router, moe_bias,
           moe_w1, moe_w3, moe_w2, g_final):
    batch, seq, d = x.shape
    n_ctx = ctx.shape[1]
    depth = w_mod.shape[0]
    n_exp = moe_router.shape[-1] if moe_router.shape[0] else 0
    xl = x.reshape(batch * seq, d)
    xc = ctx.reshape(batch * n_ctx, d)

    cvec = jnp.concatenate([c, c_ctx[None], jnp.zeros((8 - batch - 1, d), F32)], axis=0)
    mods = modulation(cvec, w_mod, b_mod)

    rope_l = _rope_inputs(seq, True)
    rope_c = _rope_inputs(n_ctx, False)
    dft_l = _dft_tables(seq)
    dft_c = _dft_tables(n_ctx)
    ones_l = jnp.ones((batch * seq, 1), F32)
    ones_c = jnp.ones((batch * n_ctx, 1), F32)

    for layer in range(depth):
        last = layer == depth - 1
        lw = _prep_layer(w_in[layer], b_gate[layer], g_q[layer], w_uq[layer], g_kv[layer], w_ukv[layer],
                         w_oa[layer], w_ob[layer], w_oc[layer], w_out[layer], w_pool[layer], s_pool[layer])
        g1 = g_norm1[layer].reshape(1, d)
        g2 = g_norm2[layer].reshape(1, d)
        ml = mods[layer, :batch].reshape(batch, 6, d)
        mc = jnp.broadcast_to(mods[layer, batch].reshape(1, 6, d), (batch, 6, d))

        if last:
            k_c, vt_c = proj_in(xc, mc[:, 0:2], g1, rope_c[0], rope_c[1], lw, batch, True)
            kv_c = (k_c, vt_c)
        else:
            xc_new, h2_c, kv_c = _token_mixer(xc, mc[:, 0:2], mc[:, 2:5], lw, g1, g2, rope_c, dft_c,
                                              batch, None, 256)
        xl, h2_l, _ = _token_mixer(xl, ml[:, 0:2], ml[:, 2:5], lw, g1, g2, rope_l, dft_l, batch, kv_c, 256)
        if not last:
            xc = xc_new

        streams = [(xl, h2_l, ml, ones_l)]
        if not last:
            streams.append((xc, h2_c, mc, ones_c))
        new = []
        for xs, h2, mm, ones in streams:
            gt2 = mm[:, 5:6]
            if layer % 2 == 0:
                li = layer // 2
                xs = ffn(h2, xs, gt2, ones, ffn_w1[li].astype(BF16), ffn_w3[li].astype(BF16),
                         ffn_w2[li].astype(BF16), batch)
            else:
                li = layer // 2
                comb = router(xs, mm[:, 3:5], g2, moe_router[li].T, moe_bias[li].reshape(n_exp, 1), batch)
                comb = comb.T
                for e in range(n_exp):
                    xs = ffn(h2, xs, gt2, comb[:, e:e + 1], moe_w1[li, e].astype(BF16),
                             moe_w3[li, e].astype(BF16), moe_w2[li, e].astype(BF16), batch)
            new.append(xs)
        xl = new[0]
        if not last:
            xc = new[1]

    return final_norm(xl, g_final.reshape(1, d)).reshape(batch, seq, d)
```

```python
import functools
import math

import numpy as np
import jax
import jax.numpy as jnp
from jax import lax
from jax.experimental import pallas as pl
from jax.experimental.pallas import tpu as pltpu

GRID_WIDTH = 64
NUM_HEADS = 8
NOPE_DIM = 64
ROPE_DIM = 32
VAL_DIM = 64
ROPE_THETA = 10000.0
POOL_WINS = (2, 4, 8, 16)
GROUP_CH = 128
NUM_BRANCH = 3
TOPK = 2
NORM_EPS = 1e-6

LANES = 128
HEAD_PAD = 128
VT_ROWS = 80
POOL_HALO = 16
DFT_N2 = 128
VMEM_LIMIT = 56 * 1024 * 1024
MOE_ROWS = 256
MOE_SUB = 256
MOE_CHUNK = 2048
MOE_VMEM_LIMIT = 60 * 1024 * 1024
STALE_MAX_JUMP = 64.0

BF16 = jnp.bfloat16
F32 = jnp.float32


def _cparams(sem):
    return pltpu.CompilerParams(dimension_semantics=sem, vmem_limit_bytes=VMEM_LIMIT)


def _full(shape):
    n = len(shape)
    return pl.BlockSpec(shape, lambda *_: (0,) * n)


def _dot(a, b):
    return jnp.dot(a, b, preferred_element_type=F32)


def _dot_nt(a, b):
    return lax.dot_general(a, b, (((1,), (1,)), ((), ())), preferred_element_type=F32)


def _rms(x, g):
    return x * lax.rsqrt(jnp.mean(x * x, axis=-1, keepdims=True) + NORM_EPS) * g


def _mod_kernel(c_ref, w_ref, b_ref, o_ref):
    c = c_ref[...]
    s = c * jax.nn.sigmoid(c)
    o_ref[0] = jnp.dot(s, w_ref[0], preferred_element_type=F32,
                       precision=lax.Precision.HIGHEST) + b_ref[0]


def modulation(cvec, w_mod, b_mod):
    depth, d, n = w_mod.shape
    tn = 1536
    return pl.pallas_call(
        _mod_kernel,
        grid=(depth, n // tn),
        in_specs=[pl.BlockSpec((8, d), lambda l, j: (0, 0)),
                  pl.BlockSpec((1, d, tn), lambda l, j: (l, 0, j)),
                  pl.BlockSpec((1, 1, tn), lambda l, j: (l, 0, j))],
        out_specs=pl.BlockSpec((1, 8, tn), lambda l, j: (l, 0, j)),
        out_shape=jax.ShapeDtypeStruct((depth, 8, n), F32),
        compiler_params=_cparams(("parallel", "parallel")),
        name="modulation",
    )(cvec, w_mod, b_mod.reshape(depth, 1, n))


def _proj_in_kernel(x_ref, mod_ref, g1_ref, tabq_ref, tabk_ref,
                    wa_ref, wpool_ref, wfft_ref, wgate_ref, bgate_ref,
                    gq_ref, wqn_ref, wqr_ref, eq_ref, gkv_ref, wkn_ref, ek_ref, wvt_ref, vone_ref,
                    *out_refs, q_lora, kv_lora, kv_only):
    if kv_only:
        k_ref, vt_ref = out_refs
    else:
        q_ref, k_ref, vt_ref, pool_ref, fft_ref, gate_ref = out_refs
    x = x_ref[...]
    mod = mod_ref[0]
    h = (_rms(x, g1_ref[...]) * (1.0 + mod[1:2]) + mod[0:1]).astype(BF16)
    pa = _dot(h, wa_ref[...])
    kvn = _rms(pa[:, q_lora:q_lora + kv_lora], gkv_ref[...]).astype(BF16)
    kr = (pa[:, q_lora + kv_lora:] * tabk_ref[...]).astype(BF16)
    k_all = (_dot(kvn, wkn_ref[...]) + _dot(kr, ek_ref[...])).astype(BF16)
    for hd in range(NUM_HEADS):
        k_ref[hd] = k_all[:, hd * HEAD_PAD:(hd + 1) * HEAD_PAD]
    vt_ref[0] = (_dot_nt(wvt_ref[...], kvn) + vone_ref[...]).astype(BF16)
    if kv_only:
        return
    qn = _rms(pa[:, :q_lora], gq_ref[...]).astype(BF16)
    qr = (_dot(qn, wqr_ref[...]) * tabq_ref[...]).astype(BF16)
    q_all = (_dot(qn, wqn_ref[...]) + _dot(qr, eq_ref[...])).astype(BF16)
    for hd in range(NUM_HEADS):
        q_ref[hd] = q_all[:, hd * HEAD_PAD:(hd + 1) * HEAD_PAD]
    pool_ref[...] = _dot(h, wpool_ref[...])
    fft_ref[...] = _dot(h, wfft_ref[...]).astype(BF16)
    gate_ref[...] = jax.nn.sigmoid(_dot(h, wgate_ref[...]) + bgate_ref[...]).astype(BF16)


def proj_in(x, modv, g1, tabq, tabk, lw, batch, kv_only, tm=256):
    t, d = x.shape
    seq = t // batch
    npb = seq // tm
    q_lora = lw["gq"].shape[1]
    kv_lora = lw["gkv"].shape[1]
    hw = NUM_HEADS * HEAD_PAD
    weights = [lw["wa"], lw["wpool"], lw["wfft"], lw["wgate"], lw["bgate"],
               lw["gq"], lw["wqn"], lw["wqr"], lw["eq"], lw["gkv"], lw["wkn"], lw["ek"],
               lw["wvt"], lw["vone"]]
    in_specs = [pl.BlockSpec((tm, d), lambda i: (i, 0)),
                pl.BlockSpec((1, 2, d), lambda i: (i // npb, 0, 0)),
                _full(g1.shape),
                pl.BlockSpec((tm, tabq.shape[1]), lambda i: (i % npb, 0)),
                pl.BlockSpec((tm, tabk.shape[1]), lambda i: (i % npb, 0))]
    in_specs += [_full(w.shape) for w in weights]
    head_spec = pl.BlockSpec((NUM_HEADS, tm, HEAD_PAD), lambda i: (0, i, 0))
    head_shape = jax.ShapeDtypeStruct((NUM_HEADS, t, HEAD_PAD), BF16)
    vt_spec = pl.BlockSpec((1, NUM_HEADS * VT_ROWS, tm), lambda i: (i // npb, 0, i % npb))
    vt_shape = jax.ShapeDtypeStruct((batch, NUM_HEADS * VT_ROWS, seq), BF16)
    if kv_only:
        out_specs = [head_spec, vt_spec]
        out_shape = [head_shape, vt_shape]
    else:
        pw, fw, gw = lw["wpool"].shape[1], lw["wfft"].shape[1], lw["wgate"].shape[1]
        out_specs = [head_spec, head_spec, vt_spec,
                     pl.BlockSpec((tm, pw), lambda i: (i, 0)),
                     pl.BlockSpec((tm, fw), lambda i: (i, 0)),
                     pl.BlockSpec((tm, gw), lambda i: (i, 0))]
        out_shape = [head_shape, head_shape, vt_shape,
                     jax.ShapeDtypeStruct((t, pw), F32),
                     jax.ShapeDtypeStruct((t, fw), BF16),
                     jax.ShapeDtypeStruct((t, gw), BF16)]
    return pl.pallas_call(
        functools.partial(_proj_in_kernel, q_lora=q_lora, kv_lora=kv_lora, kv_only=kv_only),
        grid=(t // tm,),
        in_specs=in_specs, out_specs=out_specs, out_shape=out_shape,
        compiler_params=_cparams(("parallel",)),
        name="proj_in_kv" if kv_only else "proj_in",
    )(x, modv, g1, tabq, tabk, *weights)


def _attn_kernel(q_ref, kc_ref, vtc_ref, *rest, tk, n_lat_tiles):
    if n_lat_tiles:
        kl_ref, vtl_ref, o_ref = rest
    else:
        (o_ref,) = rest
    nh = q_ref.shape[0]

    def rows(hd):
        return slice(hd * VT_ROWS, (hd + 1) * VT_ROWS)

    def scores(ks):
        return tuple(_dot_nt(ks[hd], q_ref[hd]) for hd in range(nh))

    def update(ss, vts, state):
        out = []
        for hd in range(nh):
            smax = jnp.max(ss[hd], axis=0, keepdims=True)
            if state is None:
                m_new = smax
                acc = _dot(vts[hd], jnp.exp2(ss[hd] - m_new).astype(BF16))
            else:
                m_old, acc_old = state[hd]
                m_new = jnp.maximum(m_old, smax)
                pv = _dot(vts[hd], jnp.exp2(ss[hd] - m_new).astype(BF16))
                acc = acc_old * jnp.exp2(m_old - m_new) + pv
            out.append((m_new, acc))
        return tuple(out)

    def lat_k(off):
        return [kl_ref[hd, pl.ds(off, tk), :] for hd in range(nh)]

    def lat_vt(off):
        return [vtl_ref[0, rows(hd), pl.ds(off, tk)] for hd in range(nh)]

    def update_stale(ss, vts, state):
        out = []
        jump = None
        for hd in range(nh):
            m_old, acc_old = state[hd]
            pv = _dot(vts[hd], jnp.exp2(ss[hd] - m_old).astype(BF16))
            smax = jnp.max(ss[hd], axis=0, keepdims=True)
            m_new = jnp.maximum(m_old, smax)
            out.append((m_new, (acc_old + pv) * jnp.exp2(m_old - m_new)))
            jump = smax - m_old if jump is None else jnp.maximum(jump, smax - m_old)
        return tuple(out), jnp.max(jump)

    state = update(scores([kc_ref[hd] for hd in range(nh)]),
                   [vtc_ref[0, rows(hd), :] for hd in range(nh)], None)
    if n_lat_tiles:
        def body(i, st):
            off = pl.multiple_of(i * tk, tk)
            ks, vts = lat_k(off), lat_vt(off)
            fast, jump = update_stale(scores(ks), vts, st)
            return lax.cond(jump > STALE_MAX_JUMP, lambda: update(scores(ks), vts, st), lambda: fast)
        state = lax.fori_loop(0, n_lat_tiles, body, state)
    for pair in range(nh // 2):
        outs = []
        for hd in (2 * pair, 2 * pair + 1):
            acc = state[hd][1]
            outs.append(acc[:VAL_DIM] / acc[VAL_DIM:VAL_DIM + 1])
        o_ref[:, pair * 2 * VAL_DIM:(pair + 1) * 2 * VAL_DIM] = (
            jnp.concatenate(outs, axis=0).T.astype(BF16))


def attention(q, kc, vtc, kl, vtl, batch, tq=256, tk=1024):
    h, t, _ = q.shape
    lq = t // batch
    nq = lq // tq
    lc = kc.shape[1] // batch
    once = pl.Buffered(1)
    in_specs = [pl.BlockSpec((h, tq, HEAD_PAD), lambda b, i: (0, b * nq + i, 0)),
                pl.BlockSpec((h, lc, HEAD_PAD), lambda b, i: (0, b, 0), pipeline_mode=once),
                pl.BlockSpec((1, h * VT_ROWS, lc), lambda b, i: (b, 0, 0), pipeline_mode=once)]
    args = [q, kc, vtc]
    n_lat_tiles = 0
    if kl is not None:
        ll = kl.shape[1] // batch
        tk = min(tk, ll)
        n_lat_tiles = ll // tk
        in_specs += [pl.BlockSpec((h, ll, HEAD_PAD), lambda b, i: (0, b, 0), pipeline_mode=once),
                     pl.BlockSpec((1, h * VT_ROWS, ll), lambda b, i: (b, 0, 0), pipeline_mode=once)]
        args += [kl, vtl]
    return pl.pallas_call(
        functools.partial(_attn_kernel, tk=tk, n_lat_tiles=n_lat_tiles),
        grid=(batch, nq),
        in_specs=in_specs,
        out_specs=pl.BlockSpec((tq, h * VAL_DIM), lambda b, i: (b * nq + i, 0)),
        out_shape=jax.ShapeDtypeStruct((t, h * VAL_DIM), BF16),
        compiler_params=_cparams(("parallel", "arbitrary")),
        name="attention",
    )(*args)


def _pool_kernel(x_ref, prev_ref, next_ref, band_ref, wp_ref, sp_ref, o_ref, *, seq, tm):
    i = pl.program_id(0)
    npb = seq // tm
    pos0 = (i % npb) * tm
    has_prev = (pos0 > 0).astype(F32)
    has_next = (pos0 + tm < seq).astype(F32)
    x = x_ref[...]
    ext = jnp.concatenate([prev_ref[...] * has_prev, x, next_ref[...] * has_next], axis=0).astype(BF16)
    t = pos0 + lax.broadcasted_iota(jnp.int32, (tm, GROUP_CH), 0)
    outs = []
    for g, win in enumerate(POOL_WINS):
        left = win // 2
        right = win - 1 - left
        cnt = (jnp.minimum(t + right + 1, seq) - jnp.maximum(t - left, 0)).astype(F32)
        sl = slice(g * GROUP_CH, (g + 1) * GROUP_CH)
        wsum = _dot(band_ref[g], ext[:, sl])
        mixed = (wsum / cnt - x[:, sl]).astype(BF16)
        outs.append(_dot(mixed, wp_ref[g]))
    o_ref[...] = (jnp.concatenate(outs, axis=1) * sp_ref[...]).astype(BF16)


def _pool_bands(tm):
    r = np.arange(tm)[:, None]
    c = np.arange(tm + 2 * POOL_HALO)[None, :]
    d = c - POOL_HALO - r
    bands = [((d >= -(w // 2)) & (d <= w - 1 - w // 2)) for w in POOL_WINS]
    return jnp.asarray(np.stack(bands).astype(np.float32), dtype=BF16)


def pool_mixer(p, wp, sp, batch, tm=256):
    t, w = p.shape
    seq = t // batch
    tm = min(tm, seq)
    hb = tm // POOL_HALO
    nh = t // POOL_HALO
    return pl.pallas_call(
        functools.partial(_pool_kernel, seq=seq, tm=tm),
        grid=(t // tm,),
        in_specs=[pl.BlockSpec((tm, w), lambda i: (i, 0)),
                  pl.BlockSpec((POOL_HALO, w), lambda i: (jnp.maximum(i * hb - 1, 0), 0)),
                  pl.BlockSpec((POOL_HALO, w), lambda i: (jnp.minimum((i + 1) * hb, nh - 1), 0)),
                  _full((len(POOL_WINS), tm, tm + 2 * POOL_HALO)),
                  _full(wp.shape), _full(sp.shape)],
        out_specs=pl.BlockSpec((tm, w), lambda i: (i, 0)),
        out_shape=jax.ShapeDtypeStruct((t, w), BF16),
        compiler_params=_cparams(("parallel",)),
        name="pool_mixer",
    )(p, p, p, _pool_bands(tm), wp, sp)


def _dft_outer_kernel(f_ref, x_ref, o_ref):
    o_ref[0] = _dot(f_ref[...], x_ref[0]).astype(BF16)


def _dft_inner_kernel(a_ref, g_ref, cs_ref, o_ref, *, scale):
    n2 = g_ref.shape[1] // 2
    rhs = a_ref[0].reshape(2 * n2, a_ref.shape[-1])
    p = _dot(g_ref[0], rhs)
    outs = []
    for g in range(p.shape[1] // GROUP_CH):
        sl = slice(g * GROUP_CH, (g + 1) * GROUP_CH)
        lhs = jnp.concatenate([p[:n2, sl], p[n2:, sl]], axis=1).astype(BF16)
        outs.append(_dot(lhs, cs_ref[...]))
    o_ref[0] = (jnp.concatenate(outs, axis=1) * scale).astype(BF16)


def _dft_tables(seq):
    n2 = seq if seq <= 2 * DFT_N2 else DFT_N2
    n1 = seq // n2
    k2 = jnp.arange(n2, dtype=jnp.int32)[:, None]
    t2 = jnp.arange(n2, dtype=jnp.int32)[None, :]
    k1 = jnp.arange(n1, dtype=jnp.int32)[:, None, None]
    ang_a = ((t2 * k2) % n2).astype(F32) * (2.0 * math.pi / n2)
    ang_b = ((t2 * k1) % seq).astype(F32) * (2.0 * math.pi / seq)
    ca, sa, cb, sb = jnp.cos(ang_a), jnp.sin(ang_a), jnp.cos(ang_b), jnp.sin(ang_b)
    gr = ca * cb - sa * sb
    gi = -(sa * cb + ca * sb)
    gmat = jnp.concatenate([jnp.concatenate([gr, -gi], axis=2),
                            jnp.concatenate([gi, gr], axis=2)], axis=1).astype(BF16)
    a1 = jnp.arange(n1, dtype=jnp.int32)
    ang1 = ((a1[:, None] * a1[None, :]) % n1).astype(F32) * (2.0 * math.pi / n1)
    f1 = jnp.concatenate([jnp.cos(ang1), -jnp.sin(ang1)], axis=0).astype(BF16)
    c = jnp.arange(GROUP_CH, dtype=jnp.int32)
    angc = ((c[:, None] * c[None, :]) % GROUP_CH).astype(F32) * (2.0 * math.pi / GROUP_CH)
    cs = jnp.concatenate([jnp.cos(angc), jnp.sin(angc)], axis=0).astype(BF16)
    return gmat, f1, cs


def fourier_mixer(f, batch, tables):
    gmat, f1, cs = tables
    t, w = f.shape
    seq = t // batch
    n1, n2x2, _ = gmat.shape
    n2 = n2x2 // 2
    scale = 1.0 / math.sqrt(seq * GROUP_CH)
    if n1 > 1:
        cols = n2 * w
        tc = min(cols, 4096)
        a = pl.pallas_call(
            _dft_outer_kernel,
            grid=(batch, cols // tc),
            in_specs=[_full(f1.shape), pl.BlockSpec((1, n1, tc), lambda b, j: (b, 0, j))],
            out_specs=pl.BlockSpec((1, 2 * n1, tc), lambda b, j: (b, 0, j)),
            out_shape=jax.ShapeDtypeStruct((batch, 2 * n1, cols), BF16),
            compiler_params=_cparams(("parallel", "parallel")),
            name="dft_outer",
        )(f1, f.reshape(batch, n1, cols))
        a = a.reshape(batch, 2, n1, n2, w)
    else:
        fr = f.reshape(batch, 1, 1, n2, w)
        a = jnp.concatenate([fr, jnp.zeros_like(fr)], axis=1)
    out = pl.pallas_call(
        functools.partial(_dft_inner_kernel, scale=scale),
        grid=(batch, n1),
        in_specs=[pl.BlockSpec((1, 2, 1, n2, w), lambda b, k: (b, 0, k, 0, 0)),
                  pl.BlockSpec((1, 2 * n2, 2 * n2), lambda b, k: (k, 0, 0)),
                  _full(cs.shape)],
        out_specs=pl.BlockSpec((1, n2, w), lambda b, k: (b, 0, k)),
        out_shape=jax.ShapeDtypeStruct((batch, n2, n1 * w), BF16),
        compiler_params=_cparams(("parallel", "parallel")),
        name="dft_inner",
    )(a, gmat, cs)
    return out.reshape(t, w)


def _merge_kernel(a_ref, p_ref, f_ref, g_ref, x_ref, mod_ref, woa_ref, wob_ref, woc_ref, wout_ref,
                  g2_ref, xo_ref, h_ref):
    d = x_ref.shape[1]
    gate = g_ref[...].astype(F32)
    y = (gate[:, :d] * _dot(a_ref[...], woa_ref[...])
         + gate[:, d:2 * d] * _dot(p_ref[...], wob_ref[...])
         + gate[:, 2 * d:] * _dot(f_ref[...], woc_ref[...]))
    mod = mod_ref[0]
    xn = x_ref[...] + mod[0:1] * _dot(y.astype(BF16), wout_ref[...])
    xo_ref[...] = xn
    h_ref[...] = (_rms(xn, g2_ref[...]) * (1.0 + mod[2:3]) + mod[1:2]).astype(BF16)


def merge(attn, pooled, four, gate, x, modv, lw, g2, batch, tm=256):
    t, d = x.shape
    npb = (t // batch) // tm
    row = lambda w: pl.BlockSpec((tm, w), lambda i: (i, 0))
    weights = [lw["woa"], lw["wob"], lw["woc"], lw["wout"], g2]
    return pl.pallas_call(
        _merge_kernel,
        grid=(t // tm,),
        in_specs=[row(attn.shape[1]), row(pooled.shape[1]), row(four.shape[1]), row(gate.shape[1]),
                  row(d), pl.BlockSpec((1, 3, d), lambda i: (i // npb, 0, 0))]
                 + [_full(w.shape) for w in weights],
        out_specs=[row(d), row(d)],
        out_shape=[jax.ShapeDtypeStruct((t, d), F32), jax.ShapeDtypeStruct((t, d), BF16)],
        compiler_params=_cparams(("parallel",)),
        name="merge",
    )(attn, pooled, four, gate, x, modv, *weights)


def _ffn_kernel(h_ref, x_ref, gt_ref, w1_ref, w3_ref, w2_ref, o_ref):
    h = h_ref[...]
    a = _dot(h, w1_ref[...])
    b = _dot(h, w3_ref[...])
    g = (a * jax.nn.sigmoid(a) * b).astype(BF16)
    o_ref[...] = x_ref[...] + gt_ref[0] * _dot(g, w2_ref[...])


def ffn(h, x, gt, w1, w3, w2, batch, tm=256):
    t, d = x.shape
    npb = (t // batch) // tm
    return pl.pallas_call(
        _ffn_kernel,
        grid=(t // tm,),
        in_specs=[pl.BlockSpec((tm, d), lambda i: (i, 0)),
                  pl.BlockSpec((tm, d), lambda i: (i, 0)),
                  pl.BlockSpec((1, 1, d), lambda i: (i // npb, 0, 0)),
                  _full(w1.shape), _full(w3.shape), _full(w2.shape)],
        out_specs=pl.BlockSpec((tm, d), lambda i: (i, 0)),
        out_shape=jax.ShapeDtypeStruct((t, d), F32),
        compiler_params=_cparams(("parallel",)),
        name="ffn",
    )(h, x, gt, w1, w3, w2)


def _router_kernel(x_ref, mod_ref, g2_ref, wr_ref, br_ref, tri_ref, comb_ref, pos_ref, cum_ref, cnt_sc):
    @pl.when(pl.program_id(0) == 0)
    def _():
        cnt_sc[...] = jnp.zeros_like(cnt_sc)

    mod = mod_ref[0]
    h = _rms(x_ref[...], g2_ref[...]) * (1.0 + mod[1:2]) + mod[0:1]
    logits = lax.dot_general(wr_ref[...], h, (((1,), (1,)), ((), ())), preferred_element_type=F32,
                             precision=lax.Precision.HIGHEST) + br_ref[...]
    ne = logits.shape[0]
    eidx = lax.broadcasted_iota(jnp.int32, logits.shape, 0)
    m1 = jnp.max(logits, axis=0, keepdims=True)
    i1 = jnp.min(jnp.where(logits == m1, eidx, ne), axis=0, keepdims=True)
    sel1 = eidx == i1
    rest = jnp.where(sel1, -jnp.inf, logits)
    m2 = jnp.max(rest, axis=0, keepdims=True)
    i2 = jnp.min(jnp.where(rest == m2, eidx, ne), axis=0, keepdims=True)
    sel2 = eidx == i2
    e2 = jnp.exp(m2 - m1)
    w1 = 1.0 / (1.0 + e2)
    comb_ref[...] = jnp.where(sel1, w1, 0.0) + jnp.where(sel2, e2 * w1, 0.0)
    self = jnp.where(sel1 | sel2, 1.0, 0.0)
    before = cnt_sc[...][:, 0:1]
    rank = before + _dot(self.astype(BF16), tri_ref[...])
    pos_ref[...] = jnp.where(self > 0.0, rank, -1.0).astype(jnp.int32)
    after = before + jnp.sum(self, axis=1, keepdims=True)
    cnt_sc[...] = jnp.broadcast_to(after, cnt_sc.shape)
    cum_ref[...] = jnp.broadcast_to(after, cum_ref.shape).astype(jnp.int32)


def router(x, modv, g2, wr_t, br, batch, tm=256):
    t, d = x.shape
    ne = wr_t.shape[0]
    npb = (t // batch) // tm
    nt = t // tm
    tri = jnp.asarray(np.triu(np.ones((tm, tm), np.float32), 1), dtype=BF16)
    return pl.pallas_call(
        _router_kernel,
        grid=(nt,),
        in_specs=[pl.BlockSpec((tm, d), lambda i: (i, 0)),
                  pl.BlockSpec((1, 2, d), lambda i: (i // npb, 0, 0)),
                  _full(g2.shape), _full(wr_t.shape), _full(br.shape), _full(tri.shape)],
        out_specs=[pl.BlockSpec((ne, tm), lambda i: (0, i)),
                   pl.BlockSpec((ne, tm), lambda i: (0, i)),
                   pl.BlockSpec((ne, LANES), lambda i: (0, i))],
        out_shape=[jax.ShapeDtypeStruct((ne, t), F32), jax.ShapeDtypeStruct((ne, t), jnp.int32),
                   jax.ShapeDtypeStruct((ne, nt * LANES), jnp.int32)],
        scratch_shapes=[pltpu.VMEM((ne, LANES), F32)],
        compiler_params=_cparams(("arbitrary",)),
        name="router",
    )(x, modv, g2, wr_t, br, tri)


def _moe_kernel(cum_ref, h_ref, posr_ref, pc_ref, *rest, nsub, cum0, first):
    if first:
        w1_ref, w3_ref, w2_ref, o_ref, hc_sc, ys_sc, acc_sc = rest
    else:
        yin_ref, w1_ref, w3_ref, w2_ref, o_ref, hc_sc, ys_sc, acc_sc = rest
    c = pl.program_id(0)

    def cum(i):
        return cum_ref[cum0 + c * nsub + i]

    base = cum(0)
    nb = (cum(nsub) - base + MOE_ROWS - 1) // MOE_ROWS
    riota = lax.broadcasted_iota(jnp.int32, (MOE_ROWS, MOE_SUB), 0)

    def expert_block(j, carry):
        off = base + j * MOE_ROWS
        hc_sc[...] = jnp.zeros_like(hc_sc)
        for s in range(nsub):
            sub = slice(s * MOE_SUB, (s + 1) * MOE_SUB)

            @pl.when((cum(s + 1) > off) & (cum(s) < off + MOE_ROWS))
            def _():
                sel = jnp.where(posr_ref[0, :, sub] - off == riota, 1.0, 0.0).astype(BF16)
                hc_sc[...] += _dot(sel, h_ref[sub, :])
        hc = hc_sc[...].astype(BF16)
        a = _dot(hc, w1_ref[0])
        b = _dot(hc, w3_ref[0])
        g = (a * jax.nn.sigmoid(a) * b).astype(BF16)
        ys_sc[j] = _dot(g, w2_ref[0]).astype(BF16)
        return carry

    lax.fori_loop(0, nb, expert_block, 0)

    liota = lax.broadcasted_iota(jnp.int32, (MOE_SUB, MOE_ROWS), 1).astype(F32)
    for s in range(nsub):
        sub = slice(s * MOE_SUB, (s + 1) * MOE_SUB)
        if first:
            acc_sc[...] = jnp.zeros_like(acc_sc)
        else:
            acc_sc[...] = yin_ref[sub, :].astype(F32)
        pos = pc_ref[0, sub, 0:1]
        wgt = pc_ref[0, sub, 1:2]
        lo = cum(s) - base
        hi = cum(s + 1) - base

        def add_block(j, carry):
            off = (base + j * MOE_ROWS).astype(F32)
            sel = jnp.where(pos - off == liota, 1.0, 0.0).astype(BF16)
            acc_sc[...] += wgt * _dot(sel, ys_sc[j])
            return carry

        lax.fori_loop(lo // MOE_ROWS, (hi + MOE_ROWS - 1) // MOE_ROWS, add_block, 0)
        o_ref[sub, :] = acc_sc[...].astype(BF16)


def moe(h, comb_t, pos_t, cum_incl, w1, w3, w2):
    t, d = h.shape
    ne = comb_t.shape[0]
    tc = min(MOE_CHUNK, t)
    assert t % tc == 0 and tc % MOE_SUB == 0
    nsub = tc // MOE_SUB
    nt = t // MOE_SUB
    cum = jnp.concatenate([jnp.zeros((ne, 1), jnp.int32), cum_incl[:, ::LANES]], axis=1)
    cum = cum.reshape(-1)
    posr = pos_t.reshape(ne, 1, t)
    pc = jnp.stack([pos_t.astype(F32), comb_t], axis=-1)
    once = pl.Buffered(1)
    y = None
    for e in range(ne):
        first = y is None
        chunk = pl.BlockSpec((tc, d), lambda c, cum: (c, 0))
        in_specs = [chunk,
                    pl.BlockSpec((1, 1, tc), lambda c, cum, e=e: (e, 0, c)),
                    pl.BlockSpec((1, tc, 2), lambda c, cum, e=e: (e, c, 0))]
        in_specs += [] if first else [chunk]
        in_specs += [pl.BlockSpec((1,) + w.shape[1:], lambda c, cum, e=e: (e, 0, 0), pipeline_mode=once)
                     for w in (w1, w3, w2)]
        grid_spec = pltpu.PrefetchScalarGridSpec(
            num_scalar_prefetch=1, grid=(t // tc,), in_specs=in_specs, out_specs=chunk,
            scratch_shapes=[pltpu.VMEM((MOE_ROWS, d), F32),
                            pltpu.VMEM((tc // MOE_ROWS, MOE_ROWS, d), BF16),
                            pltpu.VMEM((MOE_SUB, d), F32)])
        y = pl.pallas_call(
            functools.partial(_moe_kernel, nsub=nsub, cum0=e * (nt + 1), first=first),
            grid_spec=grid_spec,
            out_shape=jax.ShapeDtypeStruct((t, d), BF16),
            input_output_aliases={} if first else {4: 0},
            compiler_params=pltpu.CompilerParams(dimension_semantics=("arbitrary",),
                                                 vmem_limit_bytes=MOE_VMEM_LIMIT),
            name="moe_expert",
        )(*([cum, h, posr, pc] + ([] if first else [y]) + [w1, w3, w2]))
    return y


def _residual_kernel(x_ref, y_ref, gt_ref, g_ref, o_ref, *, norm):
    x = x_ref[...] + gt_ref[0] * y_ref[...].astype(F32)
    o_ref[...] = _rms(x, g_ref[...]) if norm else x


def residual(x, y, gt, g, batch, norm, tm=512):
    t, d = x.shape
    tm = min(tm, t // batch)
    npb = (t // batch) // tm
    return pl.pallas_call(
        functools.partial(_residual_kernel, norm=norm),
        grid=(t // tm,),
        in_specs=[pl.BlockSpec((tm, d), lambda i: (i, 0)), pl.BlockSpec((tm, d), lambda i: (i, 0)),
                  pl.BlockSpec((1, 1, d), lambda i: (i // npb, 0, 0)), _full(g.shape)],
        out_specs=pl.BlockSpec((tm, d), lambda i: (i, 0)),
        out_shape=jax.ShapeDtypeStruct((t, d), F32),
        compiler_params=_cparams(("parallel",)),
        name="residual_norm" if norm else "residual",
    )(x, y, gt, g)


def _final_norm_kernel(x_ref, g_ref, o_ref):
    o_ref[...] = _rms(x_ref[...], g_ref[...])


def final_norm(x, g, tm=512):
    t, d = x.shape
    return pl.pallas_call(
        _final_norm_kernel,
        grid=(t // tm,),
        in_specs=[pl.BlockSpec((tm, d), lambda i: (i, 0)), _full(g.shape)],
        out_specs=pl.BlockSpec((tm, d), lambda i: (i, 0)),
        out_shape=jax.ShapeDtypeStruct((t, d), F32),
        compiler_params=_cparams(("parallel",)),
        name="final_norm",
    )(x, g)


def _rope_tables(seq, rotate):
    half = ROPE_DIM // 2
    nf = ROPE_DIM // 4
    if rotate:
        t = jnp.arange(seq, dtype=jnp.int32)
        rows = (t // GRID_WIDTH).astype(F32)
        cols = (t % GRID_WIDTH).astype(F32)
        inv = ROPE_THETA ** (-jnp.arange(0, half, 2, dtype=F32) / half)
        ang = jnp.stack([rows[:, None] * inv, cols[:, None] * inv], axis=1)
        cos, sin = jnp.cos(ang), jnp.sin(ang)
    else:
        cos = jnp.ones((seq, 2, nf), F32)
        sin = jnp.zeros((seq, 2, nf), F32)
    cpat = jnp.concatenate([cos, cos], axis=2).reshape(seq, ROPE_DIM)
    spat = jnp.concatenate([-sin, sin], axis=2).reshape(seq, ROPE_DIM)
    return cpat, spat


def _rope_inputs(seq, rotate):
    cpat, spat = _rope_tables(seq, rotate)
    tabq = jnp.concatenate([jnp.tile(cpat, (1, NUM_HEADS)), jnp.tile(spat, (1, NUM_HEADS))], axis=1)
    tabk = jnp.concatenate([cpat, spat, jnp.zeros((seq, LANES - 2 * ROPE_DIM), F32)], axis=1)
    return tabq, tabk


def _swap_perm():
    j = np.arange(ROPE_DIM)
    half = ROPE_DIM // 4
    return np.where((j % (2 * half)) < half, j + half, j - half)


def _placement(shared):
    nsrc = ROPE_DIM if shared else NUM_HEADS * ROPE_DIM
    e = np.zeros((LANES if shared else 2 * nsrc, NUM_HEADS * HEAD_PAD), np.float32)
    for hd in range(NUM_HEADS):
        for j in range(ROPE_DIM):
            col = hd * HEAD_PAD + NOPE_DIM + j
            src = j if shared else hd * ROPE_DIM + j
            e[src, col] = 1.0
            e[nsrc + src, col] = 1.0
    return jnp.asarray(e, dtype=BF16)


def _prep_layer(w_in, b_gate, g_q, w_uq, g_kv, w_ukv, w_oa, w_ob, w_oc, w_out, w_pool, s_pool):
    d = w_in.shape[0]
    q_lora, kv_lora = g_q.shape[0], g_kv.shape[0]
    perm = _swap_perm()
    off_kr = q_lora + kv_lora
    off_pool = off_kr + ROPE_DIM
    pool_w = w_pool.shape[0] * w_pool.shape[1]
    off_fft = off_pool + pool_w
    off_gate = w_in.shape[1] - NUM_BRANCH * d
    kr = w_in[:, off_kr:off_pool]
    wa = jnp.concatenate([w_in[:, :off_kr], kr, kr[:, perm],
                          jnp.zeros((d, LANES - 2 * ROPE_DIM), F32)], axis=1)
    qscale = (NOPE_DIM + ROPE_DIM) ** -0.5 * math.log2(math.e)
    wq = w_uq * qscale
    wqn = jnp.pad(wq[:, :, :NOPE_DIM], ((0, 0), (0, 0), (0, HEAD_PAD - NOPE_DIM)))
    wqr = wq[:, :, NOPE_DIM:]
    wqr = jnp.concatenate([wqr.reshape(q_lora, -1), wqr[:, :, perm].reshape(q_lora, -1)], axis=1)
    wkn = jnp.pad(w_ukv[:, :, :NOPE_DIM], ((0, 0), (0, 0), (0, HEAD_PAD - NOPE_DIM)))
    wv = jnp.pad(w_ukv[:, :, NOPE_DIM:], ((0, 0), (0, 0), (0, VT_ROWS - VAL_DIM)))
    vone = np.zeros((NUM_HEADS, VT_ROWS, 1), np.float32)
    vone[:, VAL_DIM] = 1.0
    return dict(
        wa=wa.astype(BF16), wpool=w_in[:, off_pool:off_fft].astype(BF16),
        wfft=w_in[:, off_fft:off_gate].astype(BF16), wgate=w_in[:, off_gate:].astype(BF16),
        bgate=b_gate.reshape(1, -1),
        gq=g_q.reshape(1, -1), wqn=wqn.reshape(q_lora, -1).astype(BF16), wqr=wqr.astype(BF16),
        eq=_placement(False), gkv=g_kv.reshape(1, -1),
        wkn=wkn.reshape(kv_lora, -1).astype(BF16), ek=_placement(True),
        wvt=wv.reshape(kv_lora, -1).T.astype(BF16), vone=jnp.asarray(vone.reshape(-1, 1)),
        woa=w_oa.astype(BF16), wob=w_ob.astype(BF16), woc=w_oc.astype(BF16), wout=w_out.astype(BF16),
        wp=w_pool.astype(BF16), sp=s_pool.reshape(1, -1),
    )


def _token_mixer(xs, modv1, modv2, lw, g1, g2, ropes, dft, batch, kv_ctx, tm):
    tabq, tabk = ropes
    q, k, vt, pool_in, fft_in, gate = proj_in(xs, modv1, g1, tabq, tabk, lw, batch, False, tm)
    if kv_ctx is None:
        attn = attention(q, k, vt, None, None, batch)
    else:
        attn = attention(q, kv_ctx[0], kv_ctx[1], k, vt, batch)
    pooled = pool_mixer(pool_in, lw["wp"], lw["sp"], batch)
    four = fourier_mixer(fft_in, batch, dft)
    x_new, h2 = merge(attn, pooled, four, gate, xs, modv2, lw, g2, batch, tm)
    return x_new, h2, (k, vt)


def kernel(x, c, ctx, c_ctx, w_mod, b_mod, g_norm1, w_in, b_gate, g_q, w_uq, g_kv, w_ukv, w_pool, s_pool,
           w_oa, w_ob, w_oc, w_out, g_norm2, ffn_w1, ffn_w3, ffn_w2, moe_router, moe_bias,
           moe_w1, moe_w3, moe_w2, g_final):
    batch, seq, d = x.shape
    n_ctx = ctx.shape[1]
    depth = w_mod.shape[0]
    n_exp = moe_router.shape[-1] if moe_router.shape[0] else 0
    xl = x.reshape(batch * seq, d)
    xc = ctx.reshape(batch * n_ctx, d)

    cvec = jnp.concatenate([c, c_ctx[None], jnp.zeros((8 - batch - 1, d), F32)], axis=0)
    mods = modulation(cvec, w_mod, b_mod)

    rope_l = _rope_inputs(seq, True)
    rope_c = _rope_inputs(n_ctx, False)
    dft_l = _dft_tables(seq)
    dft_c = _dft_tables(n_ctx)

    for layer in range(depth):
        last = layer == depth - 1
        lw = _prep_layer(w_in[layer], b_gate[layer], g_q[layer], w_uq[layer], g_kv[layer], w_ukv[layer],
                         w_oa[layer], w_ob[layer], w_oc[layer], w_out[layer], w_pool[layer], s_pool[layer])
        g1 = g_norm1[layer].reshape(1, d)
        g2 = g_norm2[layer].reshape(1, d)
        ml = mods[layer, :batch].reshape(batch, 6, d)
        mc = jnp.broadcast_to(mods[layer, batch].reshape(1, 6, d), (batch, 6, d))

        if last:
            k_c, vt_c = proj_in(xc, mc[:, 0:2], g1, rope_c[0], rope_c[1], lw, batch, True)
            kv_c = (k_c, vt_c)
        else:
            xc_new, h2_c, kv_c = _token_mixer(xc, mc[:, 0:2], mc[:, 2:5], lw, g1, g2, rope_c, dft_c,
                                              batch, None, 256)
        xl, h2_l, _ = _token_mixer(xl, ml[:, 0:2], ml[:, 2:5], lw, g1, g2, rope_l, dft_l, batch, kv_c, 256)
        if not last:
            xc = xc_new

        streams = [(xl, h2_l, ml)]
        if not last:
            streams.append((xc, h2_c, mc))
        fuse_final = last and layer % 2 == 1
        new = []
        for xs, h2, mm in streams:
            gt2 = mm[:, 5:6]
            li = layer // 2
            if layer % 2 == 0:
                xs = ffn(h2, xs, gt2, ffn_w1[li].astype(BF16), ffn_w3[li].astype(BF16),
                         ffn_w2[li].astype(BF16), batch)
            else:
                comb_t, pos_t, cum_incl = router(xs, mm[:, 3:5], g2, moe_router[li].T,
                                                 moe_bias[li].reshape(n_exp, 1), batch)
                y = moe(h2, comb_t, pos_t, cum_incl, moe_w1[li].astype(BF16), moe_w3[li].astype(BF16),
                        moe_w2[li].astype(BF16))
                xs = residual(xs, y, gt2, g_final.reshape(1, d), batch, fuse_final)
            new.append(xs)
        xl = new[0]
        if not last:
            xc = new[1]

    if not fuse_final:
        xl = final_norm(xl, g_final.reshape(1, d))
    return xl.reshape(batch, seq, d)
```

```python
import functools
import math

import numpy as np
import jax
import jax.numpy as jnp
from jax import lax
from jax.experimental import pallas as pl
from jax.experimental.pallas import tpu as pltpu

GRID_WIDTH = 64
NUM_HEADS = 8
NOPE_DIM = 64
ROPE_DIM = 32
VAL_DIM = 64
ROPE_THETA = 10000.0
POOL_WINS = (2, 4, 8, 16)
GROUP_CH = 128
NUM_BRANCH = 3
TOPK = 2
NORM_EPS = 1e-6

LANES = 128
HEAD_PAD = 128
VT_ROWS = 80
POOL_HALO = 16
DFT_N2 = 128
VMEM_LIMIT = 56 * 1024 * 1024
MOE_ROWS = 256
MOE_SUB = 256
MOE_CHUNK = 2048
MOE_WINDOW = 6
MOE_VMEM_LIMIT = 60 * 1024 * 1024
STALE_MAX_JUMP = 64.0

BF16 = jnp.bfloat16
F32 = jnp.float32


def _cparams(sem):
    return pltpu.CompilerParams(dimension_semantics=sem, vmem_limit_bytes=VMEM_LIMIT)


def _full(shape):
    n = len(shape)
    return pl.BlockSpec(shape, lambda *_: (0,) * n)


def _dot(a, b):
    return jnp.dot(a, b, preferred_element_type=F32)


def _dot_nt(a, b):
    return lax.dot_general(a, b, (((1,), (1,)), ((), ())), preferred_element_type=F32)


def _rms(x, g):
    return x * lax.rsqrt(jnp.mean(x * x, axis=-1, keepdims=True) + NORM_EPS) * g


def _mod_kernel(c_ref, w_ref, b_ref, o_ref):
    c = c_ref[...]
    s = c * jax.nn.sigmoid(c)
    o_ref[0] = jnp.dot(s, w_ref[0], preferred_element_type=F32,
                       precision=lax.Precision.HIGHEST) + b_ref[0]


def modulation(cvec, w_mod, b_mod):
    depth, d, n = w_mod.shape
    tn = 1536
    return pl.pallas_call(
        _mod_kernel,
        grid=(depth, n // tn),
        in_specs=[pl.BlockSpec((8, d), lambda l, j: (0, 0)),
                  pl.BlockSpec((1, d, tn), lambda l, j: (l, 0, j)),
                  pl.BlockSpec((1, 1, tn), lambda l, j: (l, 0, j))],
        out_specs=pl.BlockSpec((1, 8, tn), lambda l, j: (l, 0, j)),
        out_shape=jax.ShapeDtypeStruct((depth, 8, n), F32),
        compiler_params=_cparams(("parallel", "parallel")),
        name="modulation",
    )(cvec, w_mod, b_mod.reshape(depth, 1, n))


def _proj_in_kernel(x_ref, mod_ref, g1_ref, tabq_ref, tabk_ref,
                    wa_ref, wpool_ref, wfft_ref, wgate_ref, bgate_ref,
                    gq_ref, wqn_ref, wqr_ref, eq_ref, gkv_ref, wkn_ref, ek_ref, wvt_ref, vone_ref,
                    *out_refs, q_lora, kv_lora, kv_only):
    if kv_only:
        k_ref, vt_ref = out_refs
    else:
        q_ref, k_ref, vt_ref, pool_ref, fft_ref, gate_ref = out_refs
    x = x_ref[...]
    mod = mod_ref[0]
    h = (_rms(x, g1_ref[...]) * (1.0 + mod[1:2]) + mod[0:1]).astype(BF16)
    pa = _dot(h, wa_ref[...])
    kvn = _rms(pa[:, q_lora:q_lora + kv_lora], gkv_ref[...]).astype(BF16)
    kr = (pa[:, q_lora + kv_lora:] * tabk_ref[...]).astype(BF16)
    k_all = (_dot(kvn, wkn_ref[...]) + _dot(kr, ek_ref[...])).astype(BF16)
    for hd in range(NUM_HEADS):
        k_ref[hd] = k_all[:, hd * HEAD_PAD:(hd + 1) * HEAD_PAD]
    vt_ref[0] = (_dot_nt(wvt_ref[...], kvn) + vone_ref[...]).astype(BF16)
    if kv_only:
        return
    qn = _rms(pa[:, :q_lora], gq_ref[...]).astype(BF16)
    qr = (_dot(qn, wqr_ref[...]) * tabq_ref[...]).astype(BF16)
    q_all = (_dot(qn, wqn_ref[...]) + _dot(qr, eq_ref[...])).astype(BF16)
    for hd in range(NUM_HEADS):
        q_ref[hd] = q_all[:, hd * HEAD_PAD:(hd + 1) * HEAD_PAD]
    pool_ref[...] = _dot(h, wpool_ref[...])
    fft_ref[...] = _dot(h, wfft_ref[...]).astype(BF16)
    gate_ref[...] = jax.nn.sigmoid(_dot(h, wgate_ref[...]) + bgate_ref[...]).astype(BF16)


def proj_in(x, modv, g1, tabq, tabk, lw, batch, kv_only, tm=256):
    t, d = x.shape
    seq = t // batch
    npb = seq // tm
    q_lora = lw["gq"].shape[1]
    kv_lora = lw["gkv"].shape[1]
    hw = NUM_HEADS * HEAD_PAD
    weights = [lw["wa"], lw["wpool"], lw["wfft"], lw["wgate"], lw["bgate"],
               lw["gq"], lw["wqn"], lw["wqr"], lw["eq"], lw["gkv"], lw["wkn"], lw["ek"],
               lw["wvt"], lw["vone"]]
    in_specs = [pl.BlockSpec((tm, d), lambda i: (i, 0)),
                pl.BlockSpec((1, 2, d), lambda i: (i // npb, 0, 0)),
                _full(g1.shape),
                pl.BlockSpec((tm, tabq.shape[1]), lambda i: (i % npb, 0)),
                pl.BlockSpec((tm, tabk.shape[1]), lambda i: (i % npb, 0))]
    in_specs += [_full(w.shape) for w in weights]
    head_spec = pl.BlockSpec((NUM_HEADS, tm, HEAD_PAD), lambda i: (0, i, 0))
    head_shape = jax.ShapeDtypeStruct((NUM_HEADS, t, HEAD_PAD), BF16)
    vt_spec = pl.BlockSpec((1, NUM_HEADS * VT_ROWS, tm), lambda i: (i // npb, 0, i % npb))
    vt_shape = jax.ShapeDtypeStruct((batch, NUM_HEADS * VT_ROWS, seq), BF16)
    if kv_only:
        out_specs = [head_spec, vt_spec]
        out_shape = [head_shape, vt_shape]
    else:
        pw, fw, gw = lw["wpool"].shape[1], lw["wfft"].shape[1], lw["wgate"].shape[1]
        out_specs = [head_spec, head_spec, vt_spec,
                     pl.BlockSpec((tm, pw), lambda i: (i, 0)),
                     pl.BlockSpec((tm, fw), lambda i: (i, 0)),
                     pl.BlockSpec((tm, gw), lambda i: (i, 0))]
        out_shape = [head_shape, head_shape, vt_shape,
                     jax.ShapeDtypeStruct((t, pw), F32),
                     jax.ShapeDtypeStruct((t, fw), BF16),
                     jax.ShapeDtypeStruct((t, gw), BF16)]
    return pl.pallas_call(
        functools.partial(_proj_in_kernel, q_lora=q_lora, kv_lora=kv_lora, kv_only=kv_only),
        grid=(t // tm,),
        in_specs=in_specs, out_specs=out_specs, out_shape=out_shape,
        compiler_params=_cparams(("parallel",)),
        name="proj_in_kv" if kv_only else "proj_in",
    )(x, modv, g1, tabq, tabk, *weights)


def _attn_kernel(q_ref, kc_ref, vtc_ref, *rest, tk, tk_exact, n_lat_tiles):
    if n_lat_tiles:
        kl_ref, vtl_ref, o_ref = rest
    else:
        (o_ref,) = rest
    nh = q_ref.shape[0]

    def rows(hd):
        return slice(hd * VT_ROWS, (hd + 1) * VT_ROWS)

    def scores(ks):
        return tuple(_dot_nt(ks[hd], q_ref[hd]) for hd in range(nh))

    def update(ss, vts, state):
        out = []
        for hd in range(nh):
            smax = jnp.max(ss[hd], axis=0, keepdims=True)
            if state is None:
                m_new = smax
                acc = _dot(vts[hd], jnp.exp2(ss[hd] - m_new).astype(BF16))
            else:
                m_old, acc_old = state[hd]
                m_new = jnp.maximum(m_old, smax)
                pv = _dot(vts[hd], jnp.exp2(ss[hd] - m_new).astype(BF16))
                acc = acc_old * jnp.exp2(m_old - m_new) + pv
            out.append((m_new, acc))
        return tuple(out)

    def lat_k(off, n):
        return [kl_ref[hd, pl.ds(off, n), :] for hd in range(nh)]

    def lat_vt(off, n):
        return [vtl_ref[0, rows(hd), pl.ds(off, n)] for hd in range(nh)]

    def update_stale(ss, vts, state, jump):
        out = []
        for hd in range(nh):
            m_old, acc_old = state[hd]
            pv = _dot(vts[hd], jnp.exp2(ss[hd] - m_old).astype(BF16))
            smax = jnp.max(ss[hd], axis=0, keepdims=True)
            m_new = jnp.maximum(m_old, smax)
            out.append((m_new, (acc_old + pv) * jnp.exp2(m_old - m_new)))
            jump = jnp.maximum(jump, smax - m_old)
        return tuple(out), jump

    state = update(scores([kc_ref[hd] for hd in range(nh)]),
                   [vtc_ref[0, rows(hd), :] for hd in range(nh)], None)
    if n_lat_tiles:
        def fast_body(i, carry):
            off = pl.multiple_of(i * tk, tk)
            return update_stale(scores(lat_k(off, tk)), lat_vt(off, tk), *carry)
        fast, jump = lax.fori_loop(0, n_lat_tiles, fast_body,
                                   (state, jnp.zeros((1, q_ref.shape[1]), F32)))

        def exact_loop():
            def body(i, st):
                off = pl.multiple_of(i * tk_exact, tk_exact)
                return update(scores(lat_k(off, tk_exact)), lat_vt(off, tk_exact), st)
            return lax.fori_loop(0, n_lat_tiles * (tk // tk_exact), body, state)

        state = lax.cond(jnp.max(jump) > STALE_MAX_JUMP, exact_loop, lambda: fast)
    for pair in range(nh // 2):
        outs = []
        for hd in (2 * pair, 2 * pair + 1):
            acc = state[hd][1]
            outs.append(acc[:VAL_DIM] / acc[VAL_DIM:VAL_DIM + 1])
        o_ref[:, pair * 2 * VAL_DIM:(pair + 1) * 2 * VAL_DIM] = (
            jnp.concatenate(outs, axis=0).T.astype(BF16))


def attention(q, kc, vtc, kl, vtl, batch, tq=256, tk=2048):
    h, t, _ = q.shape
    lq = t // batch
    nq = lq // tq
    lc = kc.shape[1] // batch
    once = pl.Buffered(1)
    in_specs = [pl.BlockSpec((h, tq, HEAD_PAD), lambda b, i: (0, b * nq + i, 0)),
                pl.BlockSpec((h, lc, HEAD_PAD), lambda b, i: (0, b, 0), pipeline_mode=once),
                pl.BlockSpec((1, h * VT_ROWS, lc), lambda b, i: (b, 0, 0), pipeline_mode=once)]
    args = [q, kc, vtc]
    n_lat_tiles = 0
    if kl is not None:
        ll = kl.shape[1] // batch
        tk = min(tk, ll)
        n_lat_tiles = ll // tk
        in_specs += [pl.BlockSpec((h, ll, HEAD_PAD), lambda b, i: (0, b, 0), pipeline_mode=once),
                     pl.BlockSpec((1, h * VT_ROWS, ll), lambda b, i: (b, 0, 0), pipeline_mode=once)]
        args += [kl, vtl]
    return pl.pallas_call(
        functools.partial(_attn_kernel, tk=tk, tk_exact=min(tk, 512), n_lat_tiles=n_lat_tiles),
        grid=(batch, nq),
        in_specs=in_specs,
        out_specs=pl.BlockSpec((tq, h * VAL_DIM), lambda b, i: (b * nq + i, 0)),
        out_shape=jax.ShapeDtypeStruct((t, h * VAL_DIM), BF16),
        compiler_params=_cparams(("parallel", "arbitrary")),
        name="attention",
    )(*args)


def _pool_kernel(x_ref, prev_ref, next_ref, band_ref, wp_ref, sp_ref, o_ref, *, seq, tm):
    i = pl.program_id(0)
    npb = seq // tm
    pos0 = (i % npb) * tm
    has_prev = (pos0 > 0).astype(F32)
    has_next = (pos0 + tm < seq).astype(F32)
    x = x_ref[...]
    ext = jnp.concatenate([prev_ref[...] * has_prev, x, next_ref[...] * has_next], axis=0).astype(BF16)
    t = pos0 + lax.broadcasted_iota(jnp.int32, (tm, GROUP_CH), 0)
    outs = []
    for g, win in enumerate(POOL_WINS):
        left = win // 2
        right = win - 1 - left
        cnt = (jnp.minimum(t + right + 1, seq) - jnp.maximum(t - left, 0)).astype(F32)
        sl = slice(g * GROUP_CH, (g + 1) * GROUP_CH)
        wsum = _dot(band_ref[g], ext[:, sl])
        mixed = (wsum / cnt - x[:, sl]).astype(BF16)
        outs.append(_dot(mixed, wp_ref[g]))
    o_ref[...] = (jnp.concatenate(outs, axis=1) * sp_ref[...]).astype(BF16)


def _pool_bands(tm):
    r = np.arange(tm)[:, None]
    c = np.arange(tm + 2 * POOL_HALO)[None, :]
    d = c - POOL_HALO - r
    bands = [((d >= -(w // 2)) & (d <= w - 1 - w // 2)) for w in POOL_WINS]
    return jnp.asarray(np.stack(bands).astype(np.float32), dtype=BF16)


def pool_mixer(p, wp, sp, batch, tm=256):
    t, w = p.shape
    seq = t // batch
    tm = min(tm, seq)
    hb = tm // POOL_HALO
    nh = t // POOL_HALO
    return pl.pallas_call(
        functools.partial(_pool_kernel, seq=seq, tm=tm),
        grid=(t // tm,),
        in_specs=[pl.BlockSpec((tm, w), lambda i: (i, 0)),
                  pl.BlockSpec((POOL_HALO, w), lambda i: (jnp.maximum(i * hb - 1, 0), 0)),
                  pl.BlockSpec((POOL_HALO, w), lambda i: (jnp.minimum((i + 1) * hb, nh - 1), 0)),
                  _full((len(POOL_WINS), tm, tm + 2 * POOL_HALO)),
                  _full(wp.shape), _full(sp.shape)],
        out_specs=pl.BlockSpec((tm, w), lambda i: (i, 0)),
        out_shape=jax.ShapeDtypeStruct((t, w), BF16),
        compiler_params=_cparams(("parallel",)),
        name="pool_mixer",
    )(p, p, p, _pool_bands(tm), wp, sp)


def _dft_outer_kernel(f_ref, x_ref, o_ref):
    o_ref[0] = _dot(f_ref[...], x_ref[0]).astype(BF16)


def _dft_inner_kernel(a_ref, g_ref, cs_ref, o_ref, *, scale):
    n2 = g_ref.shape[1] // 2
    rhs = a_ref[0].reshape(2 * n2, a_ref.shape[-1])
    p = _dot(g_ref[0], rhs)
    outs = []
    for g in range(p.shape[1] // GROUP_CH):
        sl = slice(g * GROUP_CH, (g + 1) * GROUP_CH)
        lhs = jnp.concatenate([p[:n2, sl], p[n2:, sl]], axis=1).astype(BF16)
        outs.append(_dot(lhs, cs_ref[...]))
    o_ref[0] = (jnp.concatenate(outs, axis=1) * scale).astype(BF16)


def _dft_tables(seq):
    n2 = seq if seq <= 2 * DFT_N2 else DFT_N2
    n1 = seq // n2
    k2 = jnp.arange(n2, dtype=jnp.int32)[:, None]
    t2 = jnp.arange(n2, dtype=jnp.int32)[None, :]
    k1 = jnp.arange(n1, dtype=jnp.int32)[:, None, None]
    ang_a = ((t2 * k2) % n2).astype(F32) * (2.0 * math.pi / n2)
    ang_b = ((t2 * k1) % seq).astype(F32) * (2.0 * math.pi / seq)
    ca, sa, cb, sb = jnp.cos(ang_a), jnp.sin(ang_a), jnp.cos(ang_b), jnp.sin(ang_b)
    gr = ca * cb - sa * sb
    gi = -(sa * cb + ca * sb)
    gmat = jnp.concatenate([jnp.concatenate([gr, -gi], axis=2),
                            jnp.concatenate([gi, gr], axis=2)], axis=1).astype(BF16)
    a1 = jnp.arange(n1, dtype=jnp.int32)
    ang1 = ((a1[:, None] * a1[None, :]) % n1).astype(F32) * (2.0 * math.pi / n1)
    f1 = jnp.concatenate([jnp.cos(ang1), -jnp.sin(ang1)], axis=0).astype(BF16)
    c = jnp.arange(GROUP_CH, dtype=jnp.int32)
    angc = ((c[:, None] * c[None, :]) % GROUP_CH).astype(F32) * (2.0 * math.pi / GROUP_CH)
    cs = jnp.concatenate([jnp.cos(angc), jnp.sin(angc)], axis=0).astype(BF16)
    return gmat, f1, cs


def fourier_mixer(f, batch, tables):
    gmat, f1, cs = tables
    t, w = f.shape
    seq = t // batch
    n1, n2x2, _ = gmat.shape
    n2 = n2x2 // 2
    scale = 1.0 / math.sqrt(seq * GROUP_CH)
    if n1 > 1:
        cols = n2 * w
        tc = min(cols, 4096)
        a = pl.pallas_call(
            _dft_outer_kernel,
            grid=(batch, cols // tc),
            in_specs=[_full(f1.shape), pl.BlockSpec((1, n1, tc), lambda b, j: (b, 0, j))],
            out_specs=pl.BlockSpec((1, 2 * n1, tc), lambda b, j: (b, 0, j)),
            out_shape=jax.ShapeDtypeStruct((batch, 2 * n1, cols), BF16),
            compiler_params=_cparams(("parallel", "parallel")),
            name="dft_outer",
        )(f1, f.reshape(batch, n1, cols))
        a = a.reshape(batch, 2, n1, n2, w)
    else:
        fr = f.reshape(batch, 1, 1, n2, w)
        a = jnp.concatenate([fr, jnp.zeros_like(fr)], axis=1)
    out = pl.pallas_call(
        functools.partial(_dft_inner_kernel, scale=scale),
        grid=(batch, n1),
        in_specs=[pl.BlockSpec((1, 2, 1, n2, w), lambda b, k: (b, 0, k, 0, 0)),
                  pl.BlockSpec((1, 2 * n2, 2 * n2), lambda b, k: (k, 0, 0)),
                  _full(cs.shape)],
        out_specs=pl.BlockSpec((1, n2, w), lambda b, k: (b, 0, k)),
        out_shape=jax.ShapeDtypeStruct((batch, n2, n1 * w), BF16),
        compiler_params=_cparams(("parallel", "parallel")),
        name="dft_inner",
    )(a, gmat, cs)
    return out.reshape(t, w)


def _merge_kernel(a_ref, p_ref, f_ref, g_ref, x_ref, mod_ref, woa_ref, wob_ref, woc_ref, wout_ref,
                  g2_ref, xo_ref, h_ref):
    d = x_ref.shape[1]
    gate = g_ref[...].astype(F32)
    y = (gate[:, :d] * _dot(a_ref[...], woa_ref[...])
         + gate[:, d:2 * d] * _dot(p_ref[...], wob_ref[...])
         + gate[:, 2 * d:] * _dot(f_ref[...], woc_ref[...]))
    mod = mod_ref[0]
    xn = x_ref[...] + mod[0:1] * _dot(y.astype(BF16), wout_ref[...])
    xo_ref[...] = xn
    h_ref[...] = (_rms(xn, g2_ref[...]) * (1.0 + mod[2:3]) + mod[1:2]).astype(BF16)


def merge(attn, pooled, four, gate, x, modv, lw, g2, batch, tm=256):
    t, d = x.shape
    npb = (t // batch) // tm
    row = lambda w: pl.BlockSpec((tm, w), lambda i: (i, 0))
    weights = [lw["woa"], lw["wob"], lw["woc"], lw["wout"], g2]
    return pl.pallas_call(
        _merge_kernel,
        grid=(t // tm,),
        in_specs=[row(attn.shape[1]), row(pooled.shape[1]), row(four.shape[1]), row(gate.shape[1]),
                  row(d), pl.BlockSpec((1, 3, d), lambda i: (i // npb, 0, 0))]
                 + [_full(w.shape) for w in weights],
        out_specs=[row(d), row(d)],
        out_shape=[jax.ShapeDtypeStruct((t, d), F32), jax.ShapeDtypeStruct((t, d), BF16)],
        compiler_params=_cparams(("parallel",)),
        name="merge",
    )(attn, pooled, four, gate, x, modv, *weights)


def _ffn_kernel(h_ref, x_ref, gt_ref, w1_ref, w3_ref, w2_ref, o_ref):
    h = h_ref[...]
    a = _dot(h, w1_ref[...])
    b = _dot(h, w3_ref[...])
    g = (a * jax.nn.sigmoid(a) * b).astype(BF16)
    o_ref[...] = x_ref[...] + gt_ref[0] * _dot(g, w2_ref[...])


def ffn(h, x, gt, w1, w3, w2, batch, tm=256):
    t, d = x.shape
    npb = (t // batch) // tm
    return pl.pallas_call(
        _ffn_kernel,
        grid=(t // tm,),
        in_specs=[pl.BlockSpec((tm, d), lambda i: (i, 0)),
                  pl.BlockSpec((tm, d), lambda i: (i, 0)),
                  pl.BlockSpec((1, 1, d), lambda i: (i // npb, 0, 0)),
                  _full(w1.shape), _full(w3.shape), _full(w2.shape)],
        out_specs=pl.BlockSpec((tm, d), lambda i: (i, 0)),
        out_shape=jax.ShapeDtypeStruct((t, d), F32),
        compiler_params=_cparams(("parallel",)),
        name="ffn",
    )(h, x, gt, w1, w3, w2)


def _router_kernel(x_ref, mod_ref, g2_ref, wr_ref, br_ref, tri_ref, comb_ref, pos_ref, cum_ref, cnt_sc):
    @pl.when(pl.program_id(0) == 0)
    def _():
        cnt_sc[...] = jnp.zeros_like(cnt_sc)

    mod = mod_ref[0]
    h = _rms(x_ref[...], g2_ref[...]) * (1.0 + mod[1:2]) + mod[0:1]
    h_hi = h.astype(BF16)
    h_lo = (h - h_hi.astype(F32)).astype(BF16)
    w = wr_ref[...]
    w_hi = w.astype(BF16)
    w_lo = (w - w_hi.astype(F32)).astype(BF16)
    ne = w.shape[0]
    part = _dot_nt(jnp.concatenate([w_hi, w_lo], axis=0), h_hi)
    logits = part[:ne] + part[ne:] + _dot_nt(w_hi, h_lo) + br_ref[...]
    ne = logits.shape[0]
    eidx = lax.broadcasted_iota(jnp.int32, logits.shape, 0)
    m1 = jnp.max(logits, axis=0, keepdims=True)
    i1 = jnp.min(jnp.where(logits == m1, eidx, ne), axis=0, keepdims=True)
    sel1 = eidx == i1
    rest = jnp.where(sel1, -jnp.inf, logits)
    m2 = jnp.max(rest, axis=0, keepdims=True)
    i2 = jnp.min(jnp.where(rest == m2, eidx, ne), axis=0, keepdims=True)
    sel2 = eidx == i2
    e2 = jnp.exp(m2 - m1)
    w1 = 1.0 / (1.0 + e2)
    comb_ref[...] = jnp.where(sel1, w1, 0.0) + jnp.where(sel2, e2 * w1, 0.0)
    self = jnp.where(sel1 | sel2, 1.0, 0.0)
    before = cnt_sc[...][:, 0:1]
    rank = before + _dot(self.astype(BF16), tri_ref[...])
    pos_ref[...] = jnp.where(self > 0.0, rank, -1.0).astype(jnp.int32)
    after = before + jnp.sum(self, axis=1, keepdims=True)
    cnt_sc[...] = jnp.broadcast_to(after, cnt_sc.shape)
    cum_ref[...] = jnp.broadcast_to(after, cum_ref.shape).astype(jnp.int32)


def router(x, modv, g2, wr_t, br, batch, tm=256):
    t, d = x.shape
    ne = wr_t.shape[0]
    npb = (t // batch) // tm
    nt = t // tm
    tri = jnp.asarray(np.triu(np.ones((tm, tm), np.float32), 1), dtype=BF16)
    return pl.pallas_call(
        _router_kernel,
        grid=(nt,),
        in_specs=[pl.BlockSpec((tm, d), lambda i: (i, 0)),
                  pl.BlockSpec((1, 2, d), lambda i: (i // npb, 0, 0)),
                  _full(g2.shape), _full(wr_t.shape), _full(br.shape), _full(tri.shape)],
        out_specs=[pl.BlockSpec((ne, tm), lambda i: (0, i)),
                   pl.BlockSpec((ne, tm), lambda i: (0, i)),
                   pl.BlockSpec((ne, LANES), lambda i: (0, i))],
        out_shape=[jax.ShapeDtypeStruct((ne, t), F32), jax.ShapeDtypeStruct((ne, t), jnp.int32),
                   jax.ShapeDtypeStruct((ne, nt * LANES), jnp.int32)],
        scratch_shapes=[pltpu.VMEM((ne, LANES), F32)],
        compiler_params=_cparams(("arbitrary",)),
        name="router",
    )(x, modv, g2, wr_t, br, tri)


def _moe_kernel(cum_ref, h_ref, posr_ref, pc_ref, *rest, nsub, cum0, first):
    if first:
        w1_ref, w3_ref, w2_ref, o_ref, hc_sc, ys_sc = rest
    else:
        yin_ref, w1_ref, w3_ref, w2_ref, o_ref, hc_sc, ys_sc = rest
    c = pl.program_id(0)

    def cum(i):
        return cum_ref[cum0 + c * nsub + i]

    base = cum(0)
    nb = (cum(nsub) - base + MOE_ROWS - 1) // MOE_ROWS
    riota = lax.broadcasted_iota(jnp.int32, (MOE_ROWS, MOE_SUB), 0)

    nwin = min(MOE_WINDOW, nsub)
    wiota = lax.broadcasted_iota(jnp.int32, (MOE_ROWS, nwin * MOE_SUB), 0)

    def expert_block(j, carry):
        off = base + j * MOE_ROWS
        s_lo = sum((cum(s + 1) <= off).astype(jnp.int32) for s in range(nsub))
        s_hi = sum((cum(s) < off + MOE_ROWS).astype(jnp.int32) for s in range(nsub))

        @pl.when(s_hi - s_lo <= nwin)
        def _():
            start = pl.multiple_of(jnp.minimum(s_lo, nsub - nwin) * MOE_SUB, MOE_SUB)
            win = pl.ds(start, nwin * MOE_SUB)
            sel = jnp.where(posr_ref[0, :, win] - off == wiota, 1.0, 0.0).astype(BF16)
            hc_sc[...] = _dot(sel, h_ref[win, :])

        @pl.when(s_hi - s_lo > nwin)
        def _():
            hc_sc[...] = jnp.zeros_like(hc_sc)
            for s in range(nsub):
                sub = slice(s * MOE_SUB, (s + 1) * MOE_SUB)

                @pl.when((cum(s + 1) > off) & (cum(s) < off + MOE_ROWS))
                def _():
                    sel = jnp.where(posr_ref[0, :, sub] - off == riota, 1.0, 0.0).astype(BF16)
                    hc_sc[...] += _dot(sel, h_ref[sub, :])

        hc = hc_sc[...].astype(BF16)
        a = _dot(hc, w1_ref[0])
        b = _dot(hc, w3_ref[0])
        g = (a * jax.nn.sigmoid(a) * b).astype(BF16)
        ys_sc[j] = _dot(g, w2_ref[0]).astype(BF16)
        return carry

    lax.fori_loop(0, nb, expert_block, 0)

    ys_sc[nb] = jnp.zeros(ys_sc.shape[1:], BF16)
    ys_sc[nb + 1] = jnp.zeros(ys_sc.shape[1:], BF16)
    liota = lax.broadcasted_iota(jnp.int32, (MOE_SUB, 2 * MOE_ROWS), 1).astype(F32)
    for s in range(nsub):
        sub = slice(s * MOE_SUB, (s + 1) * MOE_SUB)
        pos = pc_ref[0, sub, 0:1]
        j_lo = jnp.minimum((cum(s) - base) // MOE_ROWS, nb)
        off = (base + j_lo * MOE_ROWS).astype(F32)
        sel = jnp.where(pos - off == liota, 1.0, 0.0).astype(BF16)
        rows = ys_sc[pl.ds(j_lo, 2)].reshape(2 * MOE_ROWS, ys_sc.shape[2])
        y = pc_ref[0, sub, 1:2] * _dot(sel, rows)
        if not first:
            y = yin_ref[sub, :].astype(F32) + y
        o_ref[sub, :] = y.astype(BF16)


def moe(h, comb_t, pos_t, cum_incl, w1, w3, w2):
    t, d = h.shape
    ne = comb_t.shape[0]
    tc = min(MOE_CHUNK, t)
    assert t % tc == 0 and tc % MOE_SUB == 0
    nsub = tc // MOE_SUB
    nt = t // MOE_SUB
    cum = jnp.concatenate([jnp.zeros((ne, 1), jnp.int32), cum_incl[:, ::LANES]], axis=1)
    cum = cum.reshape(-1)
    posr = pos_t.reshape(ne, 1, t)
    pc = jnp.stack([pos_t.astype(F32), comb_t], axis=-1)
    once = pl.Buffered(1)
    y = None
    for e in range(ne):
        first = y is None
        chunk = pl.BlockSpec((tc, d), lambda c, cum: (c, 0))
        in_specs = [chunk,
                    pl.BlockSpec((1, 1, tc), lambda c, cum, e=e: (e, 0, c)),
                    pl.BlockSpec((1, tc, 2), lambda c, cum, e=e: (e, c, 0))]
        in_specs += [] if first else [chunk]
        in_specs += [pl.BlockSpec((1,) + w.shape[1:], lambda c, cum, e=e: (e, 0, 0), pipeline_mode=once)
                     for w in (w1, w3, w2)]
        grid_spec = pltpu.PrefetchScalarGridSpec(
            num_scalar_prefetch=1, grid=(t // tc,), in_specs=in_specs, out_specs=chunk,
            scratch_shapes=[pltpu.VMEM((MOE_ROWS, d), F32),
                            pltpu.VMEM((tc // MOE_ROWS + 2, MOE_ROWS, d), BF16)])
        y = pl.pallas_call(
            functools.partial(_moe_kernel, nsub=nsub, cum0=e * (nt + 1), first=first),
            grid_spec=grid_spec,
            out_shape=jax.ShapeDtypeStruct((t, d), BF16),
            input_output_aliases={} if first else {4: 0},
            compiler_params=pltpu.CompilerParams(dimension_semantics=("arbitrary",),
                                                 vmem_limit_bytes=MOE_VMEM_LIMIT),
            name="moe_expert",
        )(*([cum, h, posr, pc] + ([] if first else [y]) + [w1, w3, w2]))
    return y


def _residual_kernel(x_ref, y_ref, gt_ref, g_ref, o_ref, *, norm):
    x = x_ref[...] + gt_ref[0] * y_ref[...].astype(F32)
    o_ref[...] = _rms(x, g_ref[...]) if norm else x


def residual(x, y, gt, g, batch, norm, tm=512):
    t, d = x.shape
    tm = min(tm, t // batch)
    npb = (t // batch) // tm
    return pl.pallas_call(
        functools.partial(_residual_kernel, norm=norm),
        grid=(t // tm,),
        in_specs=[pl.BlockSpec((tm, d), lambda i: (i, 0)), pl.BlockSpec((tm, d), lambda i: (i, 0)),
                  pl.BlockSpec((1, 1, d), lambda i: (i // npb, 0, 0)), _full(g.shape)],
        out_specs=pl.BlockSpec((tm, d), lambda i: (i, 0)),
        out_shape=jax.ShapeDtypeStruct((t, d), F32),
        compiler_params=_cparams(("parallel",)),
        name="residual_norm" if norm else "residual",
    )(x, y, gt, g)


def _final_norm_kernel(x_ref, g_ref, o_ref):
    o_ref[...] = _rms(x_ref[...], g_ref[...])


def final_norm(x, g, tm=512):
    t, d = x.shape
    return pl.pallas_call(
        _final_norm_kernel,
        grid=(t // tm,),
        in_specs=[pl.BlockSpec((tm, d), lambda i: (i, 0)), _full(g.shape)],
        out_specs=pl.BlockSpec((tm, d), lambda i: (i, 0)),
        out_shape=jax.ShapeDtypeStruct((t, d), F32),
        compiler_params=_cparams(("parallel",)),
        name="final_norm",
    )(x, g)


def _rope_tables(seq, rotate):
    half = ROPE_DIM // 2
    nf = ROPE_DIM // 4
    if rotate:
        t = jnp.arange(seq, dtype=jnp.int32)
        rows = (t // GRID_WIDTH).astype(F32)
        cols = (t % GRID_WIDTH).astype(F32)
        inv = ROPE_THETA ** (-jnp.arange(0, half, 2, dtype=F32) / half)
        ang = jnp.stack([rows[:, None] * inv, cols[:, None] * inv], axis=1)
        cos, sin = jnp.cos(ang), jnp.sin(ang)
    else:
        cos = jnp.ones((seq, 2, nf), F32)
        sin = jnp.zeros((seq, 2, nf), F32)
    cpat = jnp.concatenate([cos, cos], axis=2).reshape(seq, ROPE_DIM)
    spat = jnp.concatenate([-sin, sin], axis=2).reshape(seq, ROPE_DIM)
    return cpat, spat


def _rope_inputs(seq, rotate):
    cpat, spat = _rope_tables(seq, rotate)
    tabq = jnp.concatenate([jnp.tile(cpat, (1, NUM_HEADS)), jnp.tile(spat, (1, NUM_HEADS))], axis=1)
    tabk = jnp.concatenate([cpat, spat, jnp.zeros((seq, LANES - 2 * ROPE_DIM), F32)], axis=1)
    return tabq, tabk


def _swap_perm():
    j = np.arange(ROPE_DIM)
    half = ROPE_DIM // 4
    return np.where((j % (2 * half)) < half, j + half, j - half)


def _placement(shared):
    nsrc = ROPE_DIM if shared else NUM_HEADS * ROPE_DIM
    e = np.zeros((LANES if shared else 2 * nsrc, NUM_HEADS * HEAD_PAD), np.float32)
    for hd in range(NUM_HEADS):
        for j in range(ROPE_DIM):
            col = hd * HEAD_PAD + NOPE_DIM + j
            src = j if shared else hd * ROPE_DIM + j
            e[src, col] = 1.0
            e[nsrc + src, col] = 1.0
    return jnp.asarray(e, dtype=BF16)


def _prep_layer(w_in, b_gate, g_q, w_uq, g_kv, w_ukv, w_oa, w_ob, w_oc, w_out, w_pool, s_pool):
    d = w_in.shape[0]
    q_lora, kv_lora = g_q.shape[0], g_kv.shape[0]
    perm = _swap_perm()
    off_kr = q_lora + kv_lora
    off_pool = off_kr + ROPE_DIM
    pool_w = w_pool.shape[0] * w_pool.shape[1]
    off_fft = off_pool + pool_w
    off_gate = w_in.shape[1] - NUM_BRANCH * d
    kr = w_in[:, off_kr:off_pool]
    wa = jnp.concatenate([w_in[:, :off_kr], kr, kr[:, perm],
                          jnp.zeros((d, LANES - 2 * ROPE_DIM), F32)], axis=1)
    qscale = (NOPE_DIM + ROPE_DIM) ** -0.5 * math.log2(math.e)
    wq = w_uq * qscale
    wqn = jnp.pad(wq[:, :, :NOPE_DIM], ((0, 0), (0, 0), (0, HEAD_PAD - NOPE_DIM)))
    wqr = wq[:, :, NOPE_DIM:]
    wqr = jnp.concatenate([wqr.reshape(q_lora, -1), wqr[:, :, perm].reshape(q_lora, -1)], axis=1)
    wkn = jnp.pad(w_ukv[:, :, :NOPE_DIM], ((0, 0), (0, 0), (0, HEAD_PAD - NOPE_DIM)))
    wv = jnp.pad(w_ukv[:, :, NOPE_DIM:], ((0, 0), (0, 0), (0, VT_ROWS - VAL_DIM)))
    vone = np.zeros((NUM_HEADS, VT_ROWS, 1), np.float32)
    vone[:, VAL_DIM] = 1.0
    return dict(
        wa=wa.astype(BF16), wpool=w_in[:, off_pool:off_fft].astype(BF16),
        wfft=w_in[:, off_fft:off_gate].astype(BF16), wgate=w_in[:, off_gate:].astype(BF16),
        bgate=b_gate.reshape(1, -1),
        gq=g_q.reshape(1, -1), wqn=wqn.reshape(q_lora, -1).astype(BF16), wqr=wqr.astype(BF16),
        eq=_placement(False), gkv=g_kv.reshape(1, -1),
        wkn=wkn.reshape(kv_lora, -1).astype(BF16), ek=_placement(True),
        wvt=wv.reshape(kv_lora, -1).T.astype(BF16), vone=jnp.asarray(vone.reshape(-1, 1)),
        woa=w_oa.astype(BF16), wob=w_ob.astype(BF16), woc=w_oc.astype(BF16), wout=w_out.astype(BF16),
        wp=w_pool.astype(BF16), sp=s_pool.reshape(1, -1),
    )


def _token_mixer(xs, modv1, modv2, lw, g1, g2, ropes, dft, batch, kv_ctx, tm):
    tabq, tabk = ropes
    q, k, vt, pool_in, fft_in, gate = proj_in(xs, modv1, g1, tabq, tabk, lw, batch, False, tm)
    if kv_ctx is None:
        attn = attention(q, k, vt, None, None, batch)
    else:
        attn = attention(q, kv_ctx[0], kv_ctx[1], k, vt, batch)
    pooled = pool_mixer(pool_in, lw["wp"], lw["sp"], batch)
    four = fourier_mixer(fft_in, batch, dft)
    x_new, h2 = merge(attn, pooled, four, gate, xs, modv2, lw, g2, batch, tm)
    return x_new, h2, (k, vt)


def kernel(x, c, ctx, c_ctx, w_mod, b_mod, g_norm1, w_in, b_gate, g_q, w_uq, g_kv, w_ukv, w_pool, s_pool,
           w_oa, w_ob, w_oc, w_out, g_norm2, ffn_w1, ffn_w3, ffn_w2, moe_router, moe_bias,
           moe_w1, moe_w3, moe_w2, g_final):
    batch, seq, d = x.shape
    n_ctx = ctx.shape[1]
    depth = w_mod.shape[0]
    n_exp = moe_router.shape[-1] if moe_router.shape[0] else 0
    xl = x.reshape(batch * seq, d)
    xc = ctx.reshape(batch * n_ctx, d)

    cvec = jnp.concatenate([c, c_ctx[None], jnp.zeros((8 - batch - 1, d), F32)], axis=0)
    mods = modulation(cvec, w_mod, b_mod)

    rope_l = _rope_inputs(seq, True)
    rope_c = _rope_inputs(n_ctx, False)
    dft_l = _dft_tables(seq)
    dft_c = _dft_tables(n_ctx)

    for layer in range(depth):
        last = layer == depth - 1
        lw = _prep_layer(w_in[layer], b_gate[layer], g_q[layer], w_uq[layer], g_kv[layer], w_ukv[layer],
                         w_oa[layer], w_ob[layer], w_oc[layer], w_out[layer], w_pool[layer], s_pool[layer])
        g1 = g_norm1[layer].reshape(1, d)
        g2 = g_norm2[layer].reshape(1, d)
        ml = mods[layer, :batch].reshape(batch, 6, d)
        mc = jnp.broadcast_to(mods[layer, batch].reshape(1, 6, d), (batch, 6, d))

        if last:
            k_c, vt_c = proj_in(xc, mc[:, 0:2], g1, rope_c[0], rope_c[1], lw, batch, True)
            kv_c = (k_c, vt_c)
        else:
            xc_new, h2_c, kv_c = _token_mixer(xc, mc[:, 0:2], mc[:, 2:5], lw, g1, g2, rope_c, dft_c,
                                              batch, None, 256)
        xl, h2_l, _ = _token_mixer(xl, ml[:, 0:2], ml[:, 2:5], lw, g1, g2, rope_l, dft_l, batch, kv_c, 512)
        if not last:
            xc = xc_new

        streams = [(xl, h2_l, ml)]
        if not last:
            streams.append((xc, h2_c, mc))
        fuse_final = last and layer % 2 == 1
        new = []
        for xs, h2, mm in streams:
            gt2 = mm[:, 5:6]
            li = layer // 2
            if layer % 2 == 0:
                xs = ffn(h2, xs, gt2, ffn_w1[li].astype(BF16), ffn_w3[li].astype(BF16),
                         ffn_w2[li].astype(BF16), batch)
            else:
                comb_t, pos_t, cum_incl = router(xs, mm[:, 3:5], g2, moe_router[li].T,
                                                 moe_bias[li].reshape(n_exp, 1), batch)
                y = moe(h2, comb_t, pos_t, cum_incl, moe_w1[li].astype(BF16), moe_w3[li].astype(BF16),
                        moe_w2[li].astype(BF16))
                xs = residual(xs, y, gt2, g_final.reshape(1, d), batch, fuse_final)
            new.append(xs)
        xl = new[0]
        if not last:
            xc = new[1]

    if not fuse_final:
        xl = final_norm(xl, g_final.reshape(1, d))
    return xl.reshape(batch, seq, d)
```

```python
import functools
import math

import numpy as np
import jax
import jax.numpy as jnp
from jax import lax
from jax.experimental import pallas as pl
from jax.experimental.pallas import tpu as pltpu

GRID_WIDTH = 64
NUM_HEADS = 8
NOPE_DIM = 64
ROPE_DIM = 32
VAL_DIM = 64
ROPE_THETA = 10000.0
POOL_WINS = (2, 4, 8, 16)
GROUP_CH = 128
NUM_BRANCH = 3
TOPK = 2
NORM_EPS = 1e-6

LANES = 128
HEAD_PAD = 128
VT_ROWS = 80
POOL_HALO = 16
DFT_N2 = 128
DFT_GROUP = 8
VMEM_LIMIT = 56 * 1024 * 1024
MOE_ROWS = 256
MOE_SUB = 256
MOE_CHUNK = 2048
MOE_WINDOW = 6
MOE_VMEM_LIMIT = 60 * 1024 * 1024
STALE_MAX_JUMP = 64.0

BF16 = jnp.bfloat16
F32 = jnp.float32


def _cparams(sem):
    return pltpu.CompilerParams(dimension_semantics=sem, vmem_limit_bytes=VMEM_LIMIT)


def _full(shape):
    n = len(shape)
    return pl.BlockSpec(shape, lambda *_: (0,) * n)


def _dot(a, b):
    return jnp.dot(a, b, preferred_element_type=F32)


def _dot_nt(a, b):
    return lax.dot_general(a, b, (((1,), (1,)), ((), ())), preferred_element_type=F32)


def _rms(x, g):
    return x * lax.rsqrt(jnp.mean(x * x, axis=-1, keepdims=True) + NORM_EPS) * g


def _mod_kernel(c_ref, w_ref, b_ref, o_ref):
    c = c_ref[...]
    s = c * jax.nn.sigmoid(c)
    o_ref[0] = jnp.dot(s, w_ref[0], preferred_element_type=F32,
                       precision=lax.Precision.HIGHEST) + b_ref[0]


def modulation(cvec, w_mod, b_mod):
    depth, d, n = w_mod.shape
    tn = 1536
    return pl.pallas_call(
        _mod_kernel,
        grid=(depth, n // tn),
        in_specs=[pl.BlockSpec((8, d), lambda l, j: (0, 0)),
                  pl.BlockSpec((1, d, tn), lambda l, j: (l, 0, j)),
                  pl.BlockSpec((1, 1, tn), lambda l, j: (l, 0, j))],
        out_specs=pl.BlockSpec((1, 8, tn), lambda l, j: (l, 0, j)),
        out_shape=jax.ShapeDtypeStruct((depth, 8, n), F32),
        compiler_params=_cparams(("parallel", "parallel")),
        name="modulation",
    )(cvec, w_mod, b_mod.reshape(depth, 1, n))


def _proj_in_kernel(x_ref, mod_ref, g1_ref, tabq_ref, tabk_ref,
                    wa_ref, wpool_ref, wfft_ref, wgate_ref, bgate_ref,
                    gq_ref, wqn_ref, wqr_ref, eq_ref, gkv_ref, wkn_ref, ek_ref, wvt_ref, vone_ref,
                    *out_refs, q_lora, kv_lora, kv_only):
    if kv_only:
        k_ref, vt_ref = out_refs
    else:
        q_ref, k_ref, vt_ref, pool_ref, fft_ref, gate_ref = out_refs
    x = x_ref[...]
    mod = mod_ref[0]
    h = (_rms(x, g1_ref[...]) * (1.0 + mod[1:2]) + mod[0:1]).astype(BF16)
    pa = _dot(h, wa_ref[...])
    kvn = _rms(pa[:, q_lora:q_lora + kv_lora], gkv_ref[...]).astype(BF16)
    kr = (pa[:, q_lora + kv_lora:] * tabk_ref[...]).astype(BF16)
    k_all = (_dot(kvn, wkn_ref[...]) + _dot(kr, ek_ref[...])).astype(BF16)
    for hd in range(NUM_HEADS):
        k_ref[hd] = k_all[:, hd * HEAD_PAD:(hd + 1) * HEAD_PAD]
    vt_ref[0] = (_dot_nt(wvt_ref[...], kvn) + vone_ref[...]).astype(BF16)
    if kv_only:
        return
    qn = _rms(pa[:, :q_lora], gq_ref[...]).astype(BF16)
    qr = (_dot(qn, wqr_ref[...]) * tabq_ref[...]).astype(BF16)
    q_all = (_dot(qn, wqn_ref[...]) + _dot(qr, eq_ref[...])).astype(BF16)
    for hd in range(NUM_HEADS):
        q_ref[hd] = q_all[:, hd * HEAD_PAD:(hd + 1) * HEAD_PAD]
    pool_ref[...] = _dot(h, wpool_ref[...])
    fft_ref[...] = _dot(h, wfft_ref[...])
    gate_ref[...] = jax.nn.sigmoid(_dot(h, wgate_ref[...]) + bgate_ref[...]).astype(BF16)


def proj_in(x, modv, g1, tabq, tabk, lw, batch, kv_only, tm=256):
    t, d = x.shape
    seq = t // batch
    npb = seq // tm
    q_lora = lw["gq"].shape[1]
    kv_lora = lw["gkv"].shape[1]
    hw = NUM_HEADS * HEAD_PAD
    weights = [lw["wa"], lw["wpool"], lw["wfft"], lw["wgate"], lw["bgate"],
               lw["gq"], lw["wqn"], lw["wqr"], lw["eq"], lw["gkv"], lw["wkn"], lw["ek"],
               lw["wvt"], lw["vone"]]
    in_specs = [pl.BlockSpec((tm, d), lambda i: (i, 0)),
                pl.BlockSpec((1, 2, d), lambda i: (i // npb, 0, 0)),
                _full(g1.shape),
                pl.BlockSpec((tm, tabq.shape[1]), lambda i: (i % npb, 0)),
                pl.BlockSpec((tm, tabk.shape[1]), lambda i: (i % npb, 0))]
    in_specs += [_full(w.shape) for w in weights]
    head_spec = pl.BlockSpec((NUM_HEADS, tm, HEAD_PAD), lambda i: (0, i, 0))
    head_shape = jax.ShapeDtypeStruct((NUM_HEADS, t, HEAD_PAD), BF16)
    vt_spec = pl.BlockSpec((1, NUM_HEADS * VT_ROWS, tm), lambda i: (i // npb, 0, i % npb))
    vt_shape = jax.ShapeDtypeStruct((batch, NUM_HEADS * VT_ROWS, seq), BF16)
    if kv_only:
        out_specs = [head_spec, vt_spec]
        out_shape = [head_shape, vt_shape]
    else:
        pw, fw, gw = lw["wpool"].shape[1], lw["wfft"].shape[1], lw["wgate"].shape[1]
        out_specs = [head_spec, head_spec, vt_spec,
                     pl.BlockSpec((tm, pw), lambda i: (i, 0)),
                     pl.BlockSpec((tm, fw), lambda i: (i, 0)),
                     pl.BlockSpec((tm, gw), lambda i: (i, 0))]
        out_shape = [head_shape, head_shape, vt_shape,
                     jax.ShapeDtypeStruct((t, pw), F32),
                     jax.ShapeDtypeStruct((t, fw), F32),
                     jax.ShapeDtypeStruct((t, gw), BF16)]
    return pl.pallas_call(
        functools.partial(_proj_in_kernel, q_lora=q_lora, kv_lora=kv_lora, kv_only=kv_only),
        grid=(t // tm,),
        in_specs=in_specs, out_specs=out_specs, out_shape=out_shape,
        compiler_params=_cparams(("parallel",)),
        name="proj_in_kv" if kv_only else "proj_in",
    )(x, modv, g1, tabq, tabk, *weights)


def _attn_kernel(q_ref, kc_ref, vtc_ref, *rest, tk, tk_exact, n_lat_tiles):
    if n_lat_tiles:
        kl_ref, vtl_ref, o_ref = rest
    else:
        (o_ref,) = rest
    nh = q_ref.shape[0]

    def rows(hd):
        return slice(hd * VT_ROWS, (hd + 1) * VT_ROWS)

    def scores(ks):
        return tuple(_dot_nt(ks[hd], q_ref[hd]) for hd in range(nh))

    def update(ss, vts, state):
        out = []
        for hd in range(nh):
            smax = jnp.max(ss[hd], axis=0, keepdims=True)
            if state is None:
                m_new = smax
                acc = _dot(vts[hd], jnp.exp2(ss[hd] - m_new).astype(BF16))
            else:
                m_old, acc_old = state[hd]
                m_new = jnp.maximum(m_old, smax)
                pv = _dot(vts[hd], jnp.exp2(ss[hd] - m_new).astype(BF16))
                acc = acc_old * jnp.exp2(m_old - m_new) + pv
            out.append((m_new, acc))
        return tuple(out)

    def lat_k(off, n):
        return [kl_ref[hd, pl.ds(off, n), :] for hd in range(nh)]

    def lat_vt(off, n):
        return [vtl_ref[0, rows(hd), pl.ds(off, n)] for hd in range(nh)]

    def update_stale(ss, vts, state, jump):
        out = []
        for hd in range(nh):
            m_old, acc_old = state[hd]
            pv = _dot(vts[hd], jnp.exp2(ss[hd] - m_old).astype(BF16))
            smax = jnp.max(ss[hd], axis=0, keepdims=True)
            m_new = jnp.maximum(m_old, smax)
            out.append((m_new, (acc_old + pv) * jnp.exp2(m_old - m_new)))
            jump = jnp.maximum(jump, smax - m_old)
        return tuple(out), jump

    state = update(scores([kc_ref[hd] for hd in range(nh)]),
                   [vtc_ref[0, rows(hd), :] for hd in range(nh)], None)
    if n_lat_tiles:
        def fast_body(i, carry):
            off = pl.multiple_of(i * tk, tk)
            return update_stale(scores(lat_k(off, tk)), lat_vt(off, tk), *carry)
        fast, jump = lax.fori_loop(0, n_lat_tiles, fast_body,
                                   (state, jnp.zeros((1, q_ref.shape[1]), F32)))

        def exact_loop():
            def body(i, st):
                off = pl.multiple_of(i * tk_exact, tk_exact)
                return update(scores(lat_k(off, tk_exact)), lat_vt(off, tk_exact), st)
            return lax.fori_loop(0, n_lat_tiles * (tk // tk_exact), body, state)

        state = lax.cond(jnp.max(jump) > STALE_MAX_JUMP, exact_loop, lambda: fast)
    for pair in range(nh // 2):
        outs = []
        for hd in (2 * pair, 2 * pair + 1):
            acc = state[hd][1]
            outs.append(acc[:VAL_DIM] / acc[VAL_DIM:VAL_DIM + 1])
        o_ref[:, pair * 2 * VAL_DIM:(pair + 1) * 2 * VAL_DIM] = (
            jnp.concatenate(outs, axis=0).T.astype(BF16))


def attention(q, kc, vtc, kl, vtl, batch, tq=256, tk=2048):
    h, t, _ = q.shape
    lq = t // batch
    nq = lq // tq
    lc = kc.shape[1] // batch
    once = pl.Buffered(1)
    in_specs = [pl.BlockSpec((h, tq, HEAD_PAD), lambda b, i: (0, b * nq + i, 0)),
                pl.BlockSpec((h, lc, HEAD_PAD), lambda b, i: (0, b, 0), pipeline_mode=once),
                pl.BlockSpec((1, h * VT_ROWS, lc), lambda b, i: (b, 0, 0), pipeline_mode=once)]
    args = [q, kc, vtc]
    n_lat_tiles = 0
    if kl is not None:
        ll = kl.shape[1] // batch
        tk = min(tk, ll)
        n_lat_tiles = ll // tk
        in_specs += [pl.BlockSpec((h, ll, HEAD_PAD), lambda b, i: (0, b, 0), pipeline_mode=once),
                     pl.BlockSpec((1, h * VT_ROWS, ll), lambda b, i: (b, 0, 0), pipeline_mode=once)]
        args += [kl, vtl]
    return pl.pallas_call(
        functools.partial(_attn_kernel, tk=tk, tk_exact=min(tk, 512), n_lat_tiles=n_lat_tiles),
        grid=(batch, nq),
        in_specs=in_specs,
        out_specs=pl.BlockSpec((tq, h * VAL_DIM), lambda b, i: (b * nq + i, 0)),
        out_shape=jax.ShapeDtypeStruct((t, h * VAL_DIM), BF16),
        compiler_params=_cparams(("parallel", "arbitrary")),
        name="attention",
    )(*args)


def _pool_kernel(x_ref, prev_ref, next_ref, band_ref, wp_ref, sp_ref, o_ref, *, seq, tm):
    i = pl.program_id(0)
    npb = seq // tm
    pos0 = (i % npb) * tm
    has_prev = (pos0 > 0).astype(F32)
    has_next = (pos0 + tm < seq).astype(F32)
    x = x_ref[...]
    ext = jnp.concatenate([prev_ref[...] * has_prev, x, next_ref[...] * has_next], axis=0).astype(BF16)
    t = pos0 + lax.broadcasted_iota(jnp.int32, (tm, GROUP_CH), 0)
    outs = []
    for g, win in enumerate(POOL_WINS):
        left = win // 2
        right = win - 1 - left
        cnt = (jnp.minimum(t + right + 1, seq) - jnp.maximum(t - left, 0)).astype(F32)
        sl = slice(g * GROUP_CH, (g + 1) * GROUP_CH)
        wsum = _dot(band_ref[g], ext[:, sl])
        mixed = (wsum / cnt - x[:, sl]).astype(BF16)
        outs.append(_dot(mixed, wp_ref[g]))
    o_ref[...] = (jnp.concatenate(outs, axis=1) * sp_ref[...]).astype(BF16)


def _pool_bands(tm):
    r = np.arange(tm)[:, None]
    c = np.arange(tm + 2 * POOL_HALO)[None, :]
    d = c - POOL_HALO - r
    bands = [((d >= -(w // 2)) & (d <= w - 1 - w // 2)) for w in POOL_WINS]
    return jnp.asarray(np.stack(bands).astype(np.float32), dtype=BF16)


def pool_mixer(p, wp, sp, batch, tm=256):
    t, w = p.shape
    seq = t // batch
    tm = min(tm, seq)
    hb = tm // POOL_HALO
    nh = t // POOL_HALO
    return pl.pallas_call(
        functools.partial(_pool_kernel, seq=seq, tm=tm),
        grid=(t // tm,),
        in_specs=[pl.BlockSpec((tm, w), lambda i: (i, 0)),
                  pl.BlockSpec((POOL_HALO, w), lambda i: (jnp.maximum(i * hb - 1, 0), 0)),
                  pl.BlockSpec((POOL_HALO, w), lambda i: (jnp.minimum((i + 1) * hb, nh - 1), 0)),
                  _full((len(POOL_WINS), tm, tm + 2 * POOL_HALO)),
                  _full(wp.shape), _full(sp.shape)],
        out_specs=pl.BlockSpec((tm, w), lambda i: (i, 0)),
        out_shape=jax.ShapeDtypeStruct((t, w), BF16),
        compiler_params=_cparams(("parallel",)),
        name="pool_mixer",
    )(p, p, p, _pool_bands(tm), wp, sp)


def _dft_outer_kernel(f_ref, x_ref, o_ref):
    for j in range(x_ref.shape[2]):
        o_ref[0, :, j, :] = _dot(f_ref[...], x_ref[0, :, j, :].astype(BF16))


def _dft_inner_kernel(a_ref, g_ref, cs_ref, o_ref, *, scale):
    n2 = g_ref.shape[1] // 2
    for k in range(g_ref.shape[0]):
        rhs = a_ref[0, :, k].reshape(2 * n2, a_ref.shape[-1]).astype(BF16)
        p = _dot(g_ref[k], rhs)
        outs = []
        for g in range(p.shape[1] // GROUP_CH):
            sl = slice(g * GROUP_CH, (g + 1) * GROUP_CH)
            lhs = jnp.concatenate([p[:n2, sl], p[n2:, sl]], axis=1).astype(BF16)
            outs.append(_dot(lhs, cs_ref[...]))
        o_ref[0, :, k, :] = jnp.concatenate(outs, axis=1) * scale


def _dft_tables(seq):
    n2 = seq if seq <= 2 * DFT_N2 else DFT_N2
    n1 = seq // n2
    k2 = jnp.arange(n2, dtype=jnp.int32)[:, None]
    t2 = jnp.arange(n2, dtype=jnp.int32)[None, :]
    k1 = jnp.arange(n1, dtype=jnp.int32)[:, None, None]
    ang_a = ((t2 * k2) % n2).astype(F32) * (2.0 * math.pi / n2)
    ang_b = ((t2 * k1) % seq).astype(F32) * (2.0 * math.pi / seq)
    ca, sa, cb, sb = jnp.cos(ang_a), jnp.sin(ang_a), jnp.cos(ang_b), jnp.sin(ang_b)
    gr = ca * cb - sa * sb
    gi = -(sa * cb + ca * sb)
    gmat = jnp.concatenate([jnp.concatenate([gr, -gi], axis=2),
                            jnp.concatenate([gi, gr], axis=2)], axis=1).astype(BF16)
    a1 = jnp.arange(n1, dtype=jnp.int32)
    ang1 = ((a1[:, None] * a1[None, :]) % n1).astype(F32) * (2.0 * math.pi / n1)
    f1 = jnp.concatenate([jnp.cos(ang1), -jnp.sin(ang1)], axis=0).astype(BF16)
    c = jnp.arange(GROUP_CH, dtype=jnp.int32)
    angc = ((c[:, None] * c[None, :]) % GROUP_CH).astype(F32) * (2.0 * math.pi / GROUP_CH)
    cs = jnp.concatenate([jnp.cos(angc), jnp.sin(angc)], axis=0).astype(BF16)
    return gmat, f1, cs


def fourier_mixer(f, batch, tables):
    gmat, f1, cs = tables
    t, w = f.shape
    seq = t // batch
    n1, n2x2, _ = gmat.shape
    n2 = n2x2 // 2
    scale = 1.0 / math.sqrt(seq * GROUP_CH)
    if n1 > 1:
        a = pl.pallas_call(
            _dft_outer_kernel,
            grid=(batch, n2 // DFT_GROUP),
            in_specs=[_full(f1.shape), pl.BlockSpec((1, n1, DFT_GROUP, w), lambda b, j: (b, 0, j, 0))],
            out_specs=pl.BlockSpec((1, 2 * n1, DFT_GROUP, w), lambda b, j: (b, 0, j, 0)),
            out_shape=jax.ShapeDtypeStruct((batch, 2 * n1, n2, w), F32),
            compiler_params=_cparams(("parallel", "parallel")),
            name="dft_outer",
        )(f1, f.reshape(batch, n1, n2, w))
        a = a.reshape(batch, 2, n1, n2, w)
    else:
        fr = f.reshape(batch, 1, 1, n2, w)
        a = jnp.concatenate([fr, jnp.zeros_like(fr)], axis=1)
    kg = min(DFT_GROUP, n1)
    out = pl.pallas_call(
        functools.partial(_dft_inner_kernel, scale=scale),
        grid=(batch, n1 // kg),
        in_specs=[pl.BlockSpec((1, 2, kg, n2, w), lambda b, k: (b, 0, k, 0, 0)),
                  pl.BlockSpec((kg, 2 * n2, 2 * n2), lambda b, k: (k, 0, 0)),
                  _full(cs.shape)],
        out_specs=pl.BlockSpec((1, n2, kg, w), lambda b, k: (b, 0, k, 0)),
        out_shape=jax.ShapeDtypeStruct((batch, n2, n1, w), F32),
        compiler_params=_cparams(("parallel", "parallel")),
        name="dft_inner",
    )(a, gmat, cs)
    return out.reshape(t, w)


def _merge_kernel(a_ref, p_ref, f_ref, g_ref, x_ref, mod_ref, woa_ref, wob_ref, woc_ref, wout_ref,
                  g2_ref, xo_ref, h_ref):
    d = x_ref.shape[1]
    gate = g_ref[...].astype(F32)
    y = (gate[:, :d] * _dot(a_ref[...], woa_ref[...])
         + gate[:, d:2 * d] * _dot(p_ref[...], wob_ref[...])
         + gate[:, 2 * d:] * _dot(f_ref[...].astype(BF16), woc_ref[...]))
    mod = mod_ref[0]
    xn = x_ref[...] + mod[0:1] * _dot(y.astype(BF16), wout_ref[...])
    xo_ref[...] = xn
    h_ref[...] = (_rms(xn, g2_ref[...]) * (1.0 + mod[2:3]) + mod[1:2]).astype(BF16)


def merge(attn, pooled, four, gate, x, modv, lw, g2, batch, tm=256):
    t, d = x.shape
    npb = (t // batch) // tm
    row = lambda w: pl.BlockSpec((tm, w), lambda i: (i, 0))
    weights = [lw["woa"], lw["wob"], lw["woc"], lw["wout"], g2]
    return pl.pallas_call(
        _merge_kernel,
        grid=(t // tm,),
        in_specs=[row(attn.shape[1]), row(pooled.shape[1]), row(four.shape[1]), row(gate.shape[1]),
                  row(d), pl.BlockSpec((1, 3, d), lambda i: (i // npb, 0, 0))]
                 + [_full(w.shape) for w in weights],
        out_specs=[row(d), row(d)],
        out_shape=[jax.ShapeDtypeStruct((t, d), F32), jax.ShapeDtypeStruct((t, d), BF16)],
        compiler_params=_cparams(("parallel",)),
        name="merge",
    )(attn, pooled, four, gate, x, modv, *weights)


def _ffn_kernel(h_ref, x_ref, gt_ref, w1_ref, w3_ref, w2_ref, o_ref):
    h = h_ref[...]
    a = _dot(h, w1_ref[...])
    b = _dot(h, w3_ref[...])
    g = (a * jax.nn.sigmoid(a) * b).astype(BF16)
    o_ref[...] = x_ref[...] + gt_ref[0] * _dot(g, w2_ref[...])


def ffn(h, x, gt, w1, w3, w2, batch, tm=256):
    t, d = x.shape
    npb = (t // batch) // tm
    return pl.pallas_call(
        _ffn_kernel,
        grid=(t // tm,),
        in_specs=[pl.BlockSpec((tm, d), lambda i: (i, 0)),
                  pl.BlockSpec((tm, d), lambda i: (i, 0)),
                  pl.BlockSpec((1, 1, d), lambda i: (i // npb, 0, 0)),
                  _full(w1.shape), _full(w3.shape), _full(w2.shape)],
        out_specs=pl.BlockSpec((tm, d), lambda i: (i, 0)),
        out_shape=jax.ShapeDtypeStruct((t, d), F32),
        compiler_params=_cparams(("parallel",)),
        name="ffn",
    )(h, x, gt, w1, w3, w2)


def _router_kernel(x_ref, mod_ref, g2_ref, wr_ref, br_ref, tri_ref, comb_ref, pos_ref, cum_ref, cnt_sc):
    @pl.when(pl.program_id(0) == 0)
    def _():
        cnt_sc[...] = jnp.zeros_like(cnt_sc)

    mod = mod_ref[0]
    h = _rms(x_ref[...], g2_ref[...]) * (1.0 + mod[1:2]) + mod[0:1]
    h_hi = h.astype(BF16)
    h_lo = (h - h_hi.astype(F32)).astype(BF16)
    w = wr_ref[...]
    w_hi = w.astype(BF16)
    w_lo = (w - w_hi.astype(F32)).astype(BF16)
    ne = w.shape[0]
    part = _dot_nt(jnp.concatenate([w_hi, w_lo], axis=0), h_hi)
    logits = part[:ne] + part[ne:] + _dot_nt(w_hi, h_lo) + br_ref[...]
    ne = logits.shape[0]
    eidx = lax.broadcasted_iota(jnp.int32, logits.shape, 0)
    m1 = jnp.max(logits, axis=0, keepdims=True)
    i1 = jnp.min(jnp.where(logits == m1, eidx, ne), axis=0, keepdims=True)
    sel1 = eidx == i1
    rest = jnp.where(sel1, -jnp.inf, logits)
    m2 = jnp.max(rest, axis=0, keepdims=True)
    i2 = jnp.min(jnp.where(rest == m2, eidx, ne), axis=0, keepdims=True)
    sel2 = eidx == i2
    e2 = jnp.exp(m2 - m1)
    w1 = 1.0 / (1.0 + e2)
    comb_ref[...] = jnp.where(sel1, w1, 0.0) + jnp.where(sel2, e2 * w1, 0.0)
    self = jnp.where(sel1 | sel2, 1.0, 0.0)
    before = cnt_sc[...][:, 0:1]
    rank = before + _dot(self.astype(BF16), tri_ref[...])
    pos_ref[...] = jnp.where(self > 0.0, rank, -1.0).astype(jnp.int32)
    after = before + jnp.sum(self, axis=1, keepdims=True)
    cnt_sc[...] = jnp.broadcast_to(after, cnt_sc.shape)
    cum_ref[...] = jnp.broadcast_to(after, cum_ref.shape).astype(jnp.int32)


def router(x, modv, g2, wr_t, br, batch, tm=256):
    t, d = x.shape
    ne = wr_t.shape[0]
    npb = (t // batch) // tm
    nt = t // tm
    tri = jnp.asarray(np.triu(np.ones((tm, tm), np.float32), 1), dtype=BF16)
    return pl.pallas_call(
        _router_kernel,
        grid=(nt,),
        in_specs=[pl.BlockSpec((tm, d), lambda i: (i, 0)),
                  pl.BlockSpec((1, 2, d), lambda i: (i // npb, 0, 0)),
                  _full(g2.shape), _full(wr_t.shape), _full(br.shape), _full(tri.shape)],
        out_specs=[pl.BlockSpec((ne, tm), lambda i: (0, i)),
                   pl.BlockSpec((ne, tm), lambda i: (0, i)),
                   pl.BlockSpec((ne, LANES), lambda i: (0, i))],
        out_shape=[jax.ShapeDtypeStruct((ne, t), F32), jax.ShapeDtypeStruct((ne, t), jnp.int32),
                   jax.ShapeDtypeStruct((ne, nt * LANES), jnp.int32)],
        scratch_shapes=[pltpu.VMEM((ne, LANES), F32)],
        compiler_params=_cparams(("arbitrary",)),
        name="router",
    )(x, modv, g2, wr_t, br, tri)


def _moe_kernel(cum_ref, h_ref, posr_ref, pc_ref, *rest, nsub, cum0, first):
    if first:
        w1_ref, w3_ref, w2_ref, o_ref, hc_sc, ys_sc = rest
    else:
        yin_ref, w1_ref, w3_ref, w2_ref, o_ref, hc_sc, ys_sc = rest
    c = pl.program_id(0)

    def cum(i):
        return cum_ref[cum0 + c * nsub + i]

    base = cum(0)
    nb = (cum(nsub) - base + MOE_ROWS - 1) // MOE_ROWS
    riota = lax.broadcasted_iota(jnp.int32, (MOE_ROWS, MOE_SUB), 0)

    nwin = min(MOE_WINDOW, nsub)
    wiota = lax.broadcasted_iota(jnp.int32, (MOE_ROWS, nwin * MOE_SUB), 0)

    def expert_block(j, carry):
        off = base + j * MOE_ROWS
        s_lo = sum((cum(s + 1) <= off).astype(jnp.int32) for s in range(nsub))
        s_hi = sum((cum(s) < off + MOE_ROWS).astype(jnp.int32) for s in range(nsub))

        @pl.when(s_hi - s_lo <= nwin)
        def _():
            start = pl.multiple_of(jnp.minimum(s_lo, nsub - nwin) * MOE_SUB, MOE_SUB)
            win = pl.ds(start, nwin * MOE_SUB)
            sel = jnp.where(posr_ref[0, :, win] - off == wiota, 1.0, 0.0).astype(BF16)
            hc_sc[...] = _dot(sel, h_ref[win, :])

        @pl.when(s_hi - s_lo > nwin)
        def _():
            hc_sc[...] = jnp.zeros_like(hc_sc)
            for s in range(nsub):
                sub = slice(s * MOE_SUB, (s + 1) * MOE_SUB)

                @pl.when((cum(s + 1) > off) & (cum(s) < off + MOE_ROWS))
                def _():
                    sel = jnp.where(posr_ref[0, :, sub] - off == riota, 1.0, 0.0).astype(BF16)
                    hc_sc[...] += _dot(sel, h_ref[sub, :])

        hc = hc_sc[...].astype(BF16)
        a = _dot(hc, w1_ref[0])
        b = _dot(hc, w3_ref[0])
        g = (a * jax.nn.sigmoid(a) * b).astype(BF16)
        ys_sc[j] = _dot(g, w2_ref[0]).astype(BF16)
        return carry

    lax.fori_loop(0, nb, expert_block, 0)

    ys_sc[nb] = jnp.zeros(ys_sc.shape[1:], BF16)
    ys_sc[nb + 1] = jnp.zeros(ys_sc.shape[1:], BF16)
    liota = lax.broadcasted_iota(jnp.int32, (MOE_SUB, 2 * MOE_ROWS), 1).astype(F32)
    for s in range(nsub):
        sub = slice(s * MOE_SUB, (s + 1) * MOE_SUB)
        pos = pc_ref[0, sub, 0:1]
        j_lo = jnp.minimum((cum(s) - base) // MOE_ROWS, nb)
        off = (base + j_lo * MOE_ROWS).astype(F32)
        sel = jnp.where(pos - off == liota, 1.0, 0.0).astype(BF16)
        rows = ys_sc[pl.ds(j_lo, 2)].reshape(2 * MOE_ROWS, ys_sc.shape[2])
        y = pc_ref[0, sub, 1:2] * _dot(sel, rows)
        if not first:
            y = yin_ref[sub, :].astype(F32) + y
        o_ref[sub, :] = y.astype(BF16)


def moe(h, comb_t, pos_t, cum_incl, w1, w3, w2):
    t, d = h.shape
    ne = comb_t.shape[0]
    tc = min(MOE_CHUNK, t)
    assert t % tc == 0 and tc % MOE_SUB == 0
    nsub = tc // MOE_SUB
    nt = t // MOE_SUB
    cum = jnp.concatenate([jnp.zeros((ne, 1), jnp.int32), cum_incl[:, ::LANES]], axis=1)
    cum = cum.reshape(-1)
    posr = pos_t.reshape(ne, 1, t)
    pc = jnp.stack([pos_t.astype(F32), comb_t], axis=-1)
    once = pl.Buffered(1)
    y = None
    for e in range(ne):
        first = y is None
        chunk = pl.BlockSpec((tc, d), lambda c, cum: (c, 0))
        in_specs = [chunk,
                    pl.BlockSpec((1, 1, tc), lambda c, cum, e=e: (e, 0, c)),
                    pl.BlockSpec((1, tc, 2), lambda c, cum, e=e: (e, c, 0))]
        in_specs += [] if first else [chunk]
        in_specs += [pl.BlockSpec((1,) + w.shape[1:], lambda c, cum, e=e: (e, 0, 0), pipeline_mode=once)
                     for w in (w1, w3, w2)]
        grid_spec = pltpu.PrefetchScalarGridSpec(
            num_scalar_prefetch=1, grid=(t // tc,), in_specs=in_specs, out_specs=chunk,
            scratch_shapes=[pltpu.VMEM((MOE_ROWS, d), F32),
                            pltpu.VMEM((tc // MOE_ROWS + 2, MOE_ROWS, d), BF16)])
        y = pl.pallas_call(
            functools.partial(_moe_kernel, nsub=nsub, cum0=e * (nt + 1), first=first),
            grid_spec=grid_spec,
            out_shape=jax.ShapeDtypeStruct((t, d), BF16),
            input_output_aliases={} if first else {4: 0},
            compiler_params=pltpu.CompilerParams(dimension_semantics=("arbitrary",),
                                                 vmem_limit_bytes=MOE_VMEM_LIMIT),
            name="moe_expert",
        )(*([cum, h, posr, pc] + ([] if first else [y]) + [w1, w3, w2]))
    return y


def _residual_kernel(x_ref, y_ref, gt_ref, g_ref, o_ref, *, norm):
    x = x_ref[...] + gt_ref[0] * y_ref[...].astype(F32)
    o_ref[...] = _rms(x, g_ref[...]) if norm else x


def residual(x, y, gt, g, batch, norm, tm=512):
    t, d = x.shape
    tm = min(tm, t // batch)
    npb = (t // batch) // tm
    return pl.pallas_call(
        functools.partial(_residual_kernel, norm=norm),
        grid=(t // tm,),
        in_specs=[pl.BlockSpec((tm, d), lambda i: (i, 0)), pl.BlockSpec((tm, d), lambda i: (i, 0)),
                  pl.BlockSpec((1, 1, d), lambda i: (i // npb, 0, 0)), _full(g.shape)],
        out_specs=pl.BlockSpec((tm, d), lambda i: (i, 0)),
        out_shape=jax.ShapeDtypeStruct((t, d), F32),
        compiler_params=_cparams(("parallel",)),
        name="residual_norm" if norm else "residual",
    )(x, y, gt, g)


def _final_norm_kernel(x_ref, g_ref, o_ref):
    o_ref[...] = _rms(x_ref[...], g_ref[...])


def final_norm(x, g, tm=512):
    t, d = x.shape
    return pl.pallas_call(
        _final_norm_kernel,
        grid=(t // tm,),
        in_specs=[pl.BlockSpec((tm, d), lambda i: (i, 0)), _full(g.shape)],
        out_specs=pl.BlockSpec((tm, d), lambda i: (i, 0)),
        out_shape=jax.ShapeDtypeStruct((t, d), F32),
        compiler_params=_cparams(("parallel",)),
        name="final_norm",
    )(x, g)


def _rope_tables(seq, rotate):
    half = ROPE_DIM // 2
    nf = ROPE_DIM // 4
    if rotate:
        t = jnp.arange(seq, dtype=jnp.int32)
        rows = (t // GRID_WIDTH).astype(F32)
        cols = (t % GRID_WIDTH).astype(F32)
        inv = ROPE_THETA ** (-jnp.arange(0, half, 2, dtype=F32) / half)
        ang = jnp.stack([rows[:, None] * inv, cols[:, None] * inv], axis=1)
        cos, sin = jnp.cos(ang), jnp.sin(ang)
    else:
        cos = jnp.ones((seq, 2, nf), F32)
        sin = jnp.zeros((seq, 2, nf), F32)
    cpat = jnp.concatenate([cos, cos], axis=2).reshape(seq, ROPE_DIM)
    spat = jnp.concatenate([-sin, sin], axis=2).reshape(seq, ROPE_DIM)
    return cpat, spat


def _rope_inputs(seq, rotate):
    cpat, spat = _rope_tables(seq, rotate)
    tabq = jnp.concatenate([jnp.tile(cpat, (1, NUM_HEADS)), jnp.tile(spat, (1, NUM_HEADS))], axis=1)
    tabk = jnp.concatenate([cpat, spat, jnp.zeros((seq, LANES - 2 * ROPE_DIM), F32)], axis=1)
    return tabq, tabk


def _swap_perm():
    j = np.arange(ROPE_DIM)
    half = ROPE_DIM // 4
    return np.where((j % (2 * half)) < half, j + half, j - half)


def _placement(shared):
    nsrc = ROPE_DIM if shared else NUM_HEADS * ROPE_DIM
    e = np.zeros((LANES if shared else 2 * nsrc, NUM_HEADS * HEAD_PAD), np.float32)
    for hd in range(NUM_HEADS):
        for j in range(ROPE_DIM):
            col = hd * HEAD_PAD + NOPE_DIM + j
            src = j if shared else hd * ROPE_DIM + j
            e[src, col] = 1.0
            e[nsrc + src, col] = 1.0
    return jnp.asarray(e, dtype=BF16)


def _prep_layer(w_in, b_gate, g_q, w_uq, g_kv, w_ukv, w_oa, w_ob, w_oc, w_out, w_pool, s_pool):
    d = w_in.shape[0]
    q_lora, kv_lora = g_q.shape[0], g_kv.shape[0]
    perm = _swap_perm()
    off_kr = q_lora + kv_lora
    off_pool = off_kr + ROPE_DIM
    pool_w = w_pool.shape[0] * w_pool.shape[1]
    off_fft = off_pool + pool_w
    off_gate = w_in.shape[1] - NUM_BRANCH * d
    kr = w_in[:, off_kr:off_pool]
    wa = jnp.concatenate([w_in[:, :off_kr], kr, kr[:, perm],
                          jnp.zeros((d, LANES - 2 * ROPE_DIM), F32)], axis=1)
    qscale = (NOPE_DIM + ROPE_DIM) ** -0.5 * math.log2(math.e)
    wq = w_uq * qscale
    wqn = jnp.pad(wq[:, :, :NOPE_DIM], ((0, 0), (0, 0), (0, HEAD_PAD - NOPE_DIM)))
    wqr = wq[:, :, NOPE_DIM:]
    wqr = jnp.concatenate([wqr.reshape(q_lora, -1), wqr[:, :, perm].reshape(q_lora, -1)], axis=1)
    wkn = jnp.pad(w_ukv[:, :, :NOPE_DIM], ((0, 0), (0, 0), (0, HEAD_PAD - NOPE_DIM)))
    wv = jnp.pad(w_ukv[:, :, NOPE_DIM:], ((0, 0), (0, 0), (0, VT_ROWS - VAL_DIM)))
    vone = np.zeros((NUM_HEADS, VT_ROWS, 1), np.float32)
    vone[:, VAL_DIM] = 1.0
    return dict(
        wa=wa.astype(BF16), wpool=w_in[:, off_pool:off_fft].astype(BF16),
        wfft=w_in[:, off_fft:off_gate].astype(BF16), wgate=w_in[:, off_gate:].astype(BF16),
        bgate=b_gate.reshape(1, -1),
        gq=g_q.reshape(1, -1), wqn=wqn.reshape(q_lora, -1).astype(BF16), wqr=wqr.astype(BF16),
        eq=_placement(False), gkv=g_kv.reshape(1, -1),
        wkn=wkn.reshape(kv_lora, -1).astype(BF16), ek=_placement(True),
        wvt=wv.reshape(kv_lora, -1).T.astype(BF16), vone=jnp.asarray(vone.reshape(-1, 1)),
        woa=w_oa.astype(BF16), wob=w_ob.astype(BF16), woc=w_oc.astype(BF16), wout=w_out.astype(BF16),
        wp=w_pool.astype(BF16), sp=s_pool.reshape(1, -1),
    )


def _token_mixer(xs, modv1, modv2, lw, g1, g2, ropes, dft, batch, kv_ctx, tm):
    tabq, tabk = ropes
    q, k, vt, pool_in, fft_in, gate = proj_in(xs, modv1, g1, tabq, tabk, lw, batch, False, tm)
    if kv_ctx is None:
        attn = attention(q, k, vt, None, None, batch)
    else:
        attn = attention(q, kv_ctx[0], kv_ctx[1], k, vt, batch)
    pooled = pool_mixer(pool_in, lw["wp"], lw["sp"], batch)
    four = fourier_mixer(fft_in, batch, dft)
    x_new, h2 = merge(attn, pooled, four, gate, xs, modv2, lw, g2, batch, tm)
    return x_new, h2, (k, vt)


def kernel(x, c, ctx, c_ctx, w_mod, b_mod, g_norm1, w_in, b_gate, g_q, w_uq, g_kv, w_ukv, w_pool, s_pool,
           w_oa, w_ob, w_oc, w_out, g_norm2, ffn_w1, ffn_w3, ffn_w2, moe_router, moe_bias,
           moe_w1, moe_w3, moe_w2, g_final):
    batch, seq, d = x.shape
    n_ctx = ctx.shape[1]
    depth = w_mod.shape[0]
    n_exp = moe_router.shape[-1] if moe_router.shape[0] else 0
    xl = x.reshape(batch * seq, d)
    xc = ctx.reshape(batch * n_ctx, d)

    cvec = jnp.concatenate([c, c_ctx[None], jnp.zeros((8 - batch - 1, d), F32)], axis=0)
    mods = modulation(cvec, w_mod, b_mod)

    rope_l = _rope_inputs(seq, True)
    rope_c = _rope_inputs(n_ctx, False)
    dft_l = _dft_tables(seq)
    dft_c = _dft_tables(n_ctx)

    for layer in range(depth):
        last = layer == depth - 1
        lw = _prep_layer(w_in[layer], b_gate[layer], g_q[layer], w_uq[layer], g_kv[layer], w_ukv[layer],
                         w_oa[layer], w_ob[layer], w_oc[layer], w_out[layer], w_pool[layer], s_pool[layer])
        g1 = g_norm1[layer].reshape(1, d)
        g2 = g_norm2[layer].reshape(1, d)
        ml = mods[layer, :batch].reshape(batch, 6, d)
        mc = jnp.broadcast_to(mods[layer, batch].reshape(1, 6, d), (batch, 6, d))

        if last:
            k_c, vt_c = proj_in(xc, mc[:, 0:2], g1, rope_c[0], rope_c[1], lw, batch, True)
            kv_c = (k_c, vt_c)
        else:
            xc_new, h2_c, kv_c = _token_mixer(xc, mc[:, 0:2], mc[:, 2:5], lw, g1, g2, rope_c, dft_c,
                                              batch, None, 256)
        xl, h2_l, _ = _token_mixer(xl, ml[:, 0:2], ml[:, 2:5], lw, g1, g2, rope_l, dft_l, batch, kv_c, 512)
        if not last:
            xc = xc_new

        streams = [(xl, h2_l, ml)]
        if not last:
            streams.append((xc, h2_c, mc))
        fuse_final = last and layer % 2 == 1
        new = []
        for xs, h2, mm in streams:
            gt2 = mm[:, 5:6]
            li = layer // 2
            if layer % 2 == 0:
                xs = ffn(h2, xs, gt2, ffn_w1[li].astype(BF16), ffn_w3[li].astype(BF16),
                         ffn_w2[li].astype(BF16), batch)
            else:
                comb_t, pos_t, cum_incl = router(xs, mm[:, 3:5], g2, moe_router[li].T,
                                                 moe_bias[li].reshape(n_exp, 1), batch)
                y = moe(h2, comb_t, pos_t, cum_incl, moe_w1[li].astype(BF16), moe_w3[li].astype(BF16),
                        moe_w2[li].astype(BF16))
                xs = residual(xs, y, gt2, g_final.reshape(1, d), batch, fuse_final)
            new.append(xs)
        xl = new[0]
        if not last:
            xc = new[1]

    if not fuse_final:
        xl = final_norm(xl, g_final.reshape(1, d))
    return xl.reshape(batch, seq, d)
```

```python
import functools
import math

import numpy as np
import jax
import jax.numpy as jnp
from jax import lax
from jax.experimental import pallas as pl
from jax.experimental.pallas import tpu as pltpu

GRID_WIDTH = 64
NUM_HEADS = 8
NOPE_DIM = 64
ROPE_DIM = 32
VAL_DIM = 64
ROPE_THETA = 10000.0
POOL_WINS = (2, 4, 8, 16)
GROUP_CH = 128
NUM_BRANCH = 3
TOPK = 2
NORM_EPS = 1e-6

LANES = 128
HEAD_PAD = 128
VT_ROWS = 80
POOL_HALO = 16
DFT_N2 = 128
DFT_GROUP = 8
VMEM_LIMIT = 56 * 1024 * 1024
MOE_ROWS = 256
MOE_SUB = 256
MOE_CHUNK = 2048
MOE_WINDOW = 6
MOE_VMEM_LIMIT = 60 * 1024 * 1024
STALE_MAX_JUMP = 64.0

BF16 = jnp.bfloat16
F32 = jnp.float32


def _cparams(sem):
    return pltpu.CompilerParams(dimension_semantics=sem, vmem_limit_bytes=VMEM_LIMIT)


def _full(shape):
    n = len(shape)
    return pl.BlockSpec(shape, lambda *_: (0,) * n)


def _dot(a, b):
    return jnp.dot(a, b, preferred_element_type=F32)


def _dot_nt(a, b):
    return lax.dot_general(a, b, (((1,), (1,)), ((), ())), preferred_element_type=F32)


def _rms(x, g):
    return x * lax.rsqrt(jnp.mean(x * x, axis=-1, keepdims=True) + NORM_EPS) * g


def _mod_kernel(c_ref, w_ref, b_ref, o_ref):
    c = c_ref[...]
    s = c * jax.nn.sigmoid(c)
    w = w_ref[0]
    s_hi = s.astype(BF16)
    s_lo = (s - s_hi.astype(F32)).astype(BF16)
    w_hi = w.astype(BF16)
    w_lo = (w - w_hi.astype(F32)).astype(BF16)
    part = _dot(jnp.concatenate([s_hi, s_lo], axis=0), w_hi)
    rows = s.shape[0]
    o_ref[0] = part[:rows] + part[rows:] + _dot(s_hi, w_lo) + b_ref[0]


def modulation(cvec, w_mod, b_mod):
    depth, d, n = w_mod.shape
    tn = 1536
    return pl.pallas_call(
        _mod_kernel,
        grid=(depth, n // tn),
        in_specs=[pl.BlockSpec((8, d), lambda l, j: (0, 0)),
                  pl.BlockSpec((1, d, tn), lambda l, j: (l, 0, j)),
                  pl.BlockSpec((1, 1, tn), lambda l, j: (l, 0, j))],
        out_specs=pl.BlockSpec((1, 8, tn), lambda l, j: (l, 0, j)),
        out_shape=jax.ShapeDtypeStruct((depth, 8, n), F32),
        compiler_params=_cparams(("parallel", "parallel")),
        name="modulation",
    )(cvec, w_mod, b_mod.reshape(depth, 1, n))


def _proj_in_kernel(x_ref, mod_ref, g1_ref, tabq_ref, tabk_ref,
                    wa_ref, wpool_ref, wfft_ref, wgate_ref, bgate_ref,
                    gq_ref, wqn_ref, wqr_ref, eq_ref, gkv_ref, wkn_ref, ek_ref, wvt_ref, vone_ref,
                    *out_refs, q_lora, kv_lora, kv_only):
    if kv_only:
        k_ref, vt_ref = out_refs
    else:
        q_ref, k_ref, vt_ref, pool_ref, fft_ref, gate_ref = out_refs
    x = x_ref[...]
    mod = mod_ref[0]
    h = (_rms(x, g1_ref[...]) * (1.0 + mod[1:2]) + mod[0:1]).astype(BF16)
    pa = _dot(h, wa_ref[...])
    kvn = _rms(pa[:, q_lora:q_lora + kv_lora], gkv_ref[...]).astype(BF16)
    kr = (pa[:, q_lora + kv_lora:] * tabk_ref[...]).astype(BF16)
    k_all = (_dot(kvn, wkn_ref[...]) + _dot(kr, ek_ref[...])).astype(BF16)
    for hd in range(NUM_HEADS):
        k_ref[hd] = k_all[:, hd * HEAD_PAD:(hd + 1) * HEAD_PAD]
    vt_ref[0] = (_dot_nt(wvt_ref[...], kvn) + vone_ref[...]).astype(BF16)
    if kv_only:
        return
    qn = _rms(pa[:, :q_lora], gq_ref[...]).astype(BF16)
    qr = (_dot(qn, wqr_ref[...]) * tabq_ref[...]).astype(BF16)
    q_all = (_dot(qn, wqn_ref[...]) + _dot(qr, eq_ref[...])).astype(BF16)
    for hd in range(NUM_HEADS):
        q_ref[hd] = q_all[:, hd * HEAD_PAD:(hd + 1) * HEAD_PAD]
    pool_ref[...] = _dot(h, wpool_ref[...])
    fft_ref[...] = _dot(h, wfft_ref[...])
    gate_ref[...] = jax.nn.sigmoid(_dot(h, wgate_ref[...]) + bgate_ref[...]).astype(BF16)


def proj_in(x, modv, g1, tabq, tabk, lw, batch, kv_only, tm=256):
    t, d = x.shape
    seq = t // batch
    npb = seq // tm
    q_lora = lw["gq"].shape[1]
    kv_lora = lw["gkv"].shape[1]
    hw = NUM_HEADS * HEAD_PAD
    weights = [lw["wa"], lw["wpool"], lw["wfft"], lw["wgate"], lw["bgate"],
               lw["gq"], lw["wqn"], lw["wqr"], lw["eq"], lw["gkv"], lw["wkn"], lw["ek"],
               lw["wvt"], lw["vone"]]
    in_specs = [pl.BlockSpec((tm, d), lambda i: (i, 0)),
                pl.BlockSpec((1, 2, d), lambda i: (i // npb, 0, 0)),
                _full(g1.shape),
                pl.BlockSpec((tm, tabq.shape[1]), lambda i: (i % npb, 0)),
                pl.BlockSpec((tm, tabk.shape[1]), lambda i: (i % npb, 0))]
    in_specs += [_full(w.shape) for w in weights]
    head_spec = pl.BlockSpec((NUM_HEADS, tm, HEAD_PAD), lambda i: (0, i, 0))
    head_shape = jax.ShapeDtypeStruct((NUM_HEADS, t, HEAD_PAD), BF16)
    vt_spec = pl.BlockSpec((1, NUM_HEADS * VT_ROWS, tm), lambda i: (i // npb, 0, i % npb))
    vt_shape = jax.ShapeDtypeStruct((batch, NUM_HEADS * VT_ROWS, seq), BF16)
    if kv_only:
        out_specs = [head_spec, vt_spec]
        out_shape = [head_shape, vt_shape]
    else:
        pw, fw, gw = lw["wpool"].shape[1], lw["wfft"].shape[1], lw["wgate"].shape[1]
        out_specs = [head_spec, head_spec, vt_spec,
                     pl.BlockSpec((tm, pw), lambda i: (i, 0)),
                     pl.BlockSpec((tm, fw), lambda i: (i, 0)),
                     pl.BlockSpec((tm, gw), lambda i: (i, 0))]
        out_shape = [head_shape, head_shape, vt_shape,
                     jax.ShapeDtypeStruct((t, pw), F32),
                     jax.ShapeDtypeStruct((t, fw), F32),
                     jax.ShapeDtypeStruct((t, gw), BF16)]
    return pl.pallas_call(
        functools.partial(_proj_in_kernel, q_lora=q_lora, kv_lora=kv_lora, kv_only=kv_only),
        grid=(t // tm,),
        in_specs=in_specs, out_specs=out_specs, out_shape=out_shape,
        compiler_params=_cparams(("parallel",)),
        name="proj_in_kv" if kv_only else "proj_in",
    )(x, modv, g1, tabq, tabk, *weights)


def _attn_kernel(q_ref, kc_ref, vtc_ref, *rest, tk, tk_exact, n_lat_tiles):
    if n_lat_tiles:
        kl_ref, vtl_ref, o_ref = rest
    else:
        (o_ref,) = rest
    nh = q_ref.shape[0]

    def rows(hd):
        return slice(hd * VT_ROWS, (hd + 1) * VT_ROWS)

    def scores(ks):
        return tuple(_dot_nt(ks[hd], q_ref[hd]) for hd in range(nh))

    def update(ss, vts, state):
        out = []
        for hd in range(nh):
            smax = jnp.max(ss[hd], axis=0, keepdims=True)
            if state is None:
                m_new = smax
                acc = _dot(vts[hd], jnp.exp2(ss[hd] - m_new).astype(BF16))
            else:
                m_old, acc_old = state[hd]
                m_new = jnp.maximum(m_old, smax)
                pv = _dot(vts[hd], jnp.exp2(ss[hd] - m_new).astype(BF16))
                acc = acc_old * jnp.exp2(m_old - m_new) + pv
            out.append((m_new, acc))
        return tuple(out)

    def lat_k(off, n):
        return [kl_ref[hd, pl.ds(off, n), :] for hd in range(nh)]

    def lat_vt(off, n):
        return [vtl_ref[0, rows(hd), pl.ds(off, n)] for hd in range(nh)]

    def update_stale(ss, vts, state, jump):
        out = []
        for hd in range(nh):
            m_old, acc_old = state[hd]
            pv = _dot(vts[hd], jnp.exp2(ss[hd] - m_old).astype(BF16))
            smax = jnp.max(ss[hd], axis=0, keepdims=True)
            m_new = jnp.maximum(m_old, smax)
            out.append((m_new, (acc_old + pv) * jnp.exp2(m_old - m_new)))
            jump = jnp.maximum(jump, smax - m_old)
        return tuple(out), jump

    state = update(scores([kc_ref[hd] for hd in range(nh)]),
                   [vtc_ref[0, rows(hd), :] for hd in range(nh)], None)
    if n_lat_tiles:
        def fast_body(i, carry):
            off = pl.multiple_of(i * tk, tk)
            return update_stale(scores(lat_k(off, tk)), lat_vt(off, tk), *carry)
        fast, jump = lax.fori_loop(0, n_lat_tiles, fast_body,
                                   (state, jnp.zeros((1, q_ref.shape[1]), F32)))

        def exact_loop():
            def body(i, st):
                off = pl.multiple_of(i * tk_exact, tk_exact)
                return update(scores(lat_k(off, tk_exact)), lat_vt(off, tk_exact), st)
            return lax.fori_loop(0, n_lat_tiles * (tk // tk_exact), body, state)

        state = lax.cond(jnp.max(jump) > STALE_MAX_JUMP, exact_loop, lambda: fast)
    for pair in range(nh // 2):
        outs = []
        for hd in (2 * pair, 2 * pair + 1):
            acc = state[hd][1]
            outs.append(acc[:VAL_DIM] / acc[VAL_DIM:VAL_DIM + 1])
        o_ref[:, pair * 2 * VAL_DIM:(pair + 1) * 2 * VAL_DIM] = (
            jnp.concatenate(outs, axis=0).T.astype(BF16))


def attention(q, kc, vtc, kl, vtl, batch, tq=256, tk=2048):
    h, t, _ = q.shape
    lq = t // batch
    nq = lq // tq
    lc = kc.shape[1] // batch
    once = pl.Buffered(1)
    in_specs = [pl.BlockSpec((h, tq, HEAD_PAD), lambda b, i: (0, b * nq + i, 0)),
                pl.BlockSpec((h, lc, HEAD_PAD), lambda b, i: (0, b, 0), pipeline_mode=once),
                pl.BlockSpec((1, h * VT_ROWS, lc), lambda b, i: (b, 0, 0), pipeline_mode=once)]
    args = [q, kc, vtc]
    n_lat_tiles = 0
    if kl is not None:
        ll = kl.shape[1] // batch
        tk = min(tk, ll)
        n_lat_tiles = ll // tk
        in_specs += [pl.BlockSpec((h, ll, HEAD_PAD), lambda b, i: (0, b, 0), pipeline_mode=once),
                     pl.BlockSpec((1, h * VT_ROWS, ll), lambda b, i: (b, 0, 0), pipeline_mode=once)]
        args += [kl, vtl]
    return pl.pallas_call(
        functools.partial(_attn_kernel, tk=tk, tk_exact=min(tk, 512), n_lat_tiles=n_lat_tiles),
        grid=(batch, nq),
        in_specs=in_specs,
        out_specs=pl.BlockSpec((tq, h * VAL_DIM), lambda b, i: (b * nq + i, 0)),
        out_shape=jax.ShapeDtypeStruct((t, h * VAL_DIM), BF16),
        compiler_params=_cparams(("parallel", "arbitrary")),
        name="attention",
    )(*args)


def _pool_kernel(x_ref, prev_ref, next_ref, band_ref, wp_ref, sp_ref, o_ref, *, seq, tm):
    i = pl.program_id(0)
    npb = seq // tm
    pos0 = (i % npb) * tm
    has_prev = (pos0 > 0).astype(F32)
    has_next = (pos0 + tm < seq).astype(F32)
    x = x_ref[...]
    ext = jnp.concatenate([prev_ref[...] * has_prev, x, next_ref[...] * has_next], axis=0).astype(BF16)
    t = pos0 + lax.broadcasted_iota(jnp.int32, (tm, GROUP_CH), 0)
    outs = []
    for g, win in enumerate(POOL_WINS):
        left = win // 2
        right = win - 1 - left
        cnt = (jnp.minimum(t + right + 1, seq) - jnp.maximum(t - left, 0)).astype(F32)
        sl = slice(g * GROUP_CH, (g + 1) * GROUP_CH)
        wsum = _dot(band_ref[g], ext[:, sl])
        mixed = (wsum / cnt - x[:, sl]).astype(BF16)
        outs.append(_dot(mixed, wp_ref[g]))
    o_ref[...] = (jnp.concatenate(outs, axis=1) * sp_ref[...]).astype(BF16)


def _pool_bands(tm):
    r = np.arange(tm)[:, None]
    c = np.arange(tm + 2 * POOL_HALO)[None, :]
    d = c - POOL_HALO - r
    bands = [((d >= -(w // 2)) & (d <= w - 1 - w // 2)) for w in POOL_WINS]
    return jnp.asarray(np.stack(bands).astype(np.float32), dtype=BF16)


def pool_mixer(p, wp, sp, batch, tm=256):
    t, w = p.shape
    seq = t // batch
    tm = min(tm, seq)
    hb = tm // POOL_HALO
    nh = t // POOL_HALO
    return pl.pallas_call(
        functools.partial(_pool_kernel, seq=seq, tm=tm),
        grid=(t // tm,),
        in_specs=[pl.BlockSpec((tm, w), lambda i: (i, 0)),
                  pl.BlockSpec((POOL_HALO, w), lambda i: (jnp.maximum(i * hb - 1, 0), 0)),
                  pl.BlockSpec((POOL_HALO, w), lambda i: (jnp.minimum((i + 1) * hb, nh - 1), 0)),
                  _full((len(POOL_WINS), tm, tm + 2 * POOL_HALO)),
                  _full(wp.shape), _full(sp.shape)],
        out_specs=pl.BlockSpec((tm, w), lambda i: (i, 0)),
        out_shape=jax.ShapeDtypeStruct((t, w), BF16),
        compiler_params=_cparams(("parallel",)),
        name="pool_mixer",
    )(p, p, p, _pool_bands(tm), wp, sp)


def _dft_outer_kernel(f_ref, x_ref, o_ref):
    for j in range(x_ref.shape[2]):
        o_ref[0, :, j, :] = _dot(f_ref[...], x_ref[0, :, j, :].astype(BF16))


def _dft_inner_kernel(a_ref, g_ref, cs_ref, o_ref, *, scale):
    n2 = g_ref.shape[1] // 2
    for k in range(g_ref.shape[0]):
        rhs = a_ref[0, :, k].reshape(2 * n2, a_ref.shape[-1]).astype(BF16)
        p = _dot(g_ref[k], rhs)
        outs = []
        for g in range(p.shape[1] // GROUP_CH):
            sl = slice(g * GROUP_CH, (g + 1) * GROUP_CH)
            lhs = jnp.concatenate([p[:n2, sl], p[n2:, sl]], axis=1).astype(BF16)
            outs.append(_dot(lhs, cs_ref[...]))
        o_ref[0, :, k, :] = jnp.concatenate(outs, axis=1) * scale


def _dft_tables(seq):
    n2 = seq if seq <= 2 * DFT_N2 else DFT_N2
    n1 = seq // n2
    k2 = jnp.arange(n2, dtype=jnp.int32)[:, None]
    t2 = jnp.arange(n2, dtype=jnp.int32)[None, :]
    k1 = jnp.arange(n1, dtype=jnp.int32)[:, None, None]
    ang_a = ((t2 * k2) % n2).astype(F32) * (2.0 * math.pi / n2)
    ang_b = ((t2 * k1) % seq).astype(F32) * (2.0 * math.pi / seq)
    ca, sa, cb, sb = jnp.cos(ang_a), jnp.sin(ang_a), jnp.cos(ang_b), jnp.sin(ang_b)
    gr = ca * cb - sa * sb
    gi = -(sa * cb + ca * sb)
    gmat = jnp.concatenate([jnp.concatenate([gr, -gi], axis=2),
                            jnp.concatenate([gi, gr], axis=2)], axis=1).astype(BF16)
    a1 = jnp.arange(n1, dtype=jnp.int32)
    ang1 = ((a1[:, None] * a1[None, :]) % n1).astype(F32) * (2.0 * math.pi / n1)
    f1 = jnp.concatenate([jnp.cos(ang1), -jnp.sin(ang1)], axis=0).astype(BF16)
    c = jnp.arange(GROUP_CH, dtype=jnp.int32)
    angc = ((c[:, None] * c[None, :]) % GROUP_CH).astype(F32) * (2.0 * math.pi / GROUP_CH)
    cs = jnp.concatenate([jnp.cos(angc), jnp.sin(angc)], axis=0).astype(BF16)
    return gmat, f1, cs


def fourier_mixer(f, batch, tables):
    gmat, f1, cs = tables
    t, w = f.shape
    seq = t // batch
    n1, n2x2, _ = gmat.shape
    n2 = n2x2 // 2
    scale = 1.0 / math.sqrt(seq * GROUP_CH)
    if n1 > 1:
        a = pl.pallas_call(
            _dft_outer_kernel,
            grid=(batch, n2 // DFT_GROUP),
            in_specs=[_full(f1.shape), pl.BlockSpec((1, n1, DFT_GROUP, w), lambda b, j: (b, 0, j, 0))],
            out_specs=pl.BlockSpec((1, 2 * n1, DFT_GROUP, w), lambda b, j: (b, 0, j, 0)),
            out_shape=jax.ShapeDtypeStruct((batch, 2 * n1, n2, w), F32),
            compiler_params=_cparams(("parallel", "parallel")),
            name="dft_outer",
        )(f1, f.reshape(batch, n1, n2, w))
        a = a.reshape(batch, 2, n1, n2, w)
    else:
        fr = f.reshape(batch, 1, 1, n2, w)
        a = jnp.concatenate([fr, jnp.zeros_like(fr)], axis=1)
    kg = min(DFT_GROUP, n1)
    out = pl.pallas_call(
        functools.partial(_dft_inner_kernel, scale=scale),
        grid=(batch, n1 // kg),
        in_specs=[pl.BlockSpec((1, 2, kg, n2, w), lambda b, k: (b, 0, k, 0, 0)),
                  pl.BlockSpec((kg, 2 * n2, 2 * n2), lambda b, k: (k, 0, 0)),
                  _full(cs.shape)],
        out_specs=pl.BlockSpec((1, n2, kg, w), lambda b, k: (b, 0, k, 0)),
        out_shape=jax.ShapeDtypeStruct((batch, n2, n1, w), F32),
        compiler_params=_cparams(("parallel", "parallel")),
        name="dft_inner",
    )(a, gmat, cs)
    return out.reshape(t, w)


def _merge_kernel(a_ref, p_ref, f_ref, g_ref, x_ref, mod_ref, woa_ref, wob_ref, woc_ref, wout_ref,
                  g2_ref, xo_ref, h_ref):
    d = x_ref.shape[1]
    gate = g_ref[...].astype(F32)
    y = (gate[:, :d] * _dot(a_ref[...], woa_ref[...])
         + gate[:, d:2 * d] * _dot(p_ref[...], wob_ref[...])
         + gate[:, 2 * d:] * _dot(f_ref[...].astype(BF16), woc_ref[...]))
    mod = mod_ref[0]
    xn = x_ref[...] + mod[0:1] * _dot(y.astype(BF16), wout_ref[...])
    xo_ref[...] = xn
    h_ref[...] = (_rms(xn, g2_ref[...]) * (1.0 + mod[2:3]) + mod[1:2]).astype(BF16)


def merge(attn, pooled, four, gate, x, modv, lw, g2, batch, tm=256):
    t, d = x.shape
    npb = (t // batch) // tm
    row = lambda w: pl.BlockSpec((tm, w), lambda i: (i, 0))
    weights = [lw["woa"], lw["wob"], lw["woc"], lw["wout"], g2]
    return pl.pallas_call(
        _merge_kernel,
        grid=(t // tm,),
        in_specs=[row(attn.shape[1]), row(pooled.shape[1]), row(four.shape[1]), row(gate.shape[1]),
                  row(d), pl.BlockSpec((1, 3, d), lambda i: (i // npb, 0, 0))]
                 + [_full(w.shape) for w in weights],
        out_specs=[row(d), row(d)],
        out_shape=[jax.ShapeDtypeStruct((t, d), F32), jax.ShapeDtypeStruct((t, d), BF16)],
        compiler_params=_cparams(("parallel",)),
        name="merge",
    )(attn, pooled, four, gate, x, modv, *weights)


def _ffn_kernel(h_ref, x_ref, gt_ref, w1_ref, w3_ref, w2_ref, o_ref):
    h = h_ref[...]
    a = _dot(h, w1_ref[...])
    b = _dot(h, w3_ref[...])
    g = (a * jax.nn.sigmoid(a) * b).astype(BF16)
    o_ref[...] = x_ref[...] + gt_ref[0] * _dot(g, w2_ref[...])


def ffn(h, x, gt, w1, w3, w2, batch, tm=256):
    t, d = x.shape
    npb = (t // batch) // tm
    return pl.pallas_call(
        _ffn_kernel,
        grid=(t // tm,),
        in_specs=[pl.BlockSpec((tm, d), lambda i: (i, 0)),
                  pl.BlockSpec((tm, d), lambda i: (i, 0)),
                  pl.BlockSpec((1, 1, d), lambda i: (i // npb, 0, 0)),
                  _full(w1.shape), _full(w3.shape), _full(w2.shape)],
        out_specs=pl.BlockSpec((tm, d), lambda i: (i, 0)),
        out_shape=jax.ShapeDtypeStruct((t, d), F32),
        compiler_params=_cparams(("parallel",)),
        name="ffn",
    )(h, x, gt, w1, w3, w2)


def _router_kernel(x_ref, mod_ref, g2_ref, wr_ref, br_ref, tri_ref, comb_ref, pos_ref, cum_ref, cnt_sc):
    @pl.when(pl.program_id(0) == 0)
    def _():
        cnt_sc[...] = jnp.zeros_like(cnt_sc)

    mod = mod_ref[0]
    h = _rms(x_ref[...], g2_ref[...]) * (1.0 + mod[1:2]) + mod[0:1]
    h_hi = h.astype(BF16)
    h_lo = (h - h_hi.astype(F32)).astype(BF16)
    w = wr_ref[...]
    w_hi = w.astype(BF16)
    w_lo = (w - w_hi.astype(F32)).astype(BF16)
    ne = w.shape[0]
    part = _dot_nt(jnp.concatenate([w_hi, w_lo], axis=0), h_hi)
    logits = part[:ne] + part[ne:] + _dot_nt(w_hi, h_lo) + br_ref[...]
    ne = logits.shape[0]
    eidx = lax.broadcasted_iota(jnp.int32, logits.shape, 0)
    m1 = jnp.max(logits, axis=0, keepdims=True)
    i1 = jnp.min(jnp.where(logits == m1, eidx, ne), axis=0, keepdims=True)
    sel1 = eidx == i1
    rest = jnp.where(sel1, -jnp.inf, logits)
    m2 = jnp.max(rest, axis=0, keepdims=True)
    i2 = jnp.min(jnp.where(rest == m2, eidx, ne), axis=0, keepdims=True)
    sel2 = eidx == i2
    e2 = jnp.exp(m2 - m1)
    w1 = 1.0 / (1.0 + e2)
    comb_ref[...] = jnp.where(sel1, w1, 0.0) + jnp.where(sel2, e2 * w1, 0.0)
    self = jnp.where(sel1 | sel2, 1.0, 0.0)
    before = cnt_sc[...][:, 0:1]
    rank = before + _dot(self.astype(BF16), tri_ref[...])
    pos_ref[...] = jnp.where(self > 0.0, rank, -1.0).astype(jnp.int32)
    after = before + jnp.sum(self, axis=1, keepdims=True)
    cnt_sc[...] = jnp.broadcast_to(after, cnt_sc.shape)
    cum_ref[...] = jnp.broadcast_to(after, cum_ref.shape).astype(jnp.int32)


def router(x, modv, g2, wr_t, br, batch, tm=256):
    t, d = x.shape
    ne = wr_t.shape[0]
    npb = (t // batch) // tm
    nt = t // tm
    tri = jnp.asarray(np.triu(np.ones((tm, tm), np.float32), 1), dtype=BF16)
    return pl.pallas_call(
        _router_kernel,
        grid=(nt,),
        in_specs=[pl.BlockSpec((tm, d), lambda i: (i, 0)),
                  pl.BlockSpec((1, 2, d), lambda i: (i // npb, 0, 0)),
                  _full(g2.shape), _full(wr_t.shape), _full(br.shape), _full(tri.shape)],
        out_specs=[pl.BlockSpec((ne, tm), lambda i: (0, i)),
                   pl.BlockSpec((ne, tm), lambda i: (0, i)),
                   pl.BlockSpec((ne, LANES), lambda i: (0, i))],
        out_shape=[jax.ShapeDtypeStruct((ne, t), F32), jax.ShapeDtypeStruct((ne, t), jnp.int32),
                   jax.ShapeDtypeStruct((ne, nt * LANES), jnp.int32)],
        scratch_shapes=[pltpu.VMEM((ne, LANES), F32)],
        compiler_params=_cparams(("arbitrary",)),
        name="router",
    )(x, modv, g2, wr_t, br, tri)


def _moe_kernel(cum_ref, h_ref, posr_ref, pc_ref, *rest, nsub, cum0, first, col):
    if first:
        w1_ref, w3_ref, w2_ref, o_ref, hc_sc, ys_sc = rest
    else:
        yin_ref, w1_ref, w3_ref, w2_ref, o_ref, hc_sc, ys_sc = rest
    c = pl.program_id(0)

    def cum(i):
        return cum_ref[cum0 + c * nsub + i]

    base = cum(0)
    total = cum(nsub) - base
    nfull = total // MOE_ROWS
    rem = total - nfull * MOE_ROWS
    nb = nfull + (rem > 0).astype(jnp.int32)
    nwin = min(MOE_WINDOW, nsub)

    def expert_block(j, nrows):
        off = base + j * MOE_ROWS
        riota = lax.broadcasted_iota(jnp.int32, (nrows, MOE_SUB), 0)
        wiota = lax.broadcasted_iota(jnp.int32, (nrows, nwin * MOE_SUB), 0)
        s_lo = sum((cum(s + 1) <= off).astype(jnp.int32) for s in range(nsub))
        s_hi = sum((cum(s) < off + nrows).astype(jnp.int32) for s in range(nsub))

        @pl.when(s_hi - s_lo <= nwin)
        def _():
            start = pl.multiple_of(jnp.minimum(s_lo, nsub - nwin) * MOE_SUB, MOE_SUB)
            win = pl.ds(start, nwin * MOE_SUB)
            sel = jnp.where(posr_ref[0, :, win] - off == wiota, 1.0, 0.0).astype(BF16)
            hc_sc[:nrows] = _dot(sel, h_ref[win, :])

        @pl.when(s_hi - s_lo > nwin)
        def _():
            hc_sc[:nrows] = jnp.zeros((nrows, hc_sc.shape[1]), F32)
            for s in range(nsub):
                sub = slice(s * MOE_SUB, (s + 1) * MOE_SUB)

                @pl.when((cum(s + 1) > off) & (cum(s) < off + nrows))
                def _():
                    sel = jnp.where(posr_ref[0, :, sub] - off == riota, 1.0, 0.0).astype(BF16)
                    hc_sc[:nrows] += _dot(sel, h_ref[sub, :])

        hc = hc_sc[:nrows].astype(BF16)
        a = _dot(hc, w1_ref[0])
        b = _dot(hc, w3_ref[0])
        g = (a * jax.nn.sigmoid(a) * b).astype(BF16)
        ys_sc[j, :nrows] = _dot(g, w2_ref[0]).astype(BF16)
        if nrows < MOE_ROWS:
            ys_sc[j, nrows:] = jnp.zeros((MOE_ROWS - nrows, ys_sc.shape[2]), BF16)

    def full_block(j, carry):
        expert_block(j, MOE_ROWS)
        return carry

    lax.fori_loop(0, nfull, full_block, 0)

    @pl.when(rem > MOE_ROWS // 2)
    def _():
        expert_block(nfull, MOE_ROWS)

    @pl.when((rem > 0) & (rem <= MOE_ROWS // 2))
    def _():
        expert_block(nfull, MOE_ROWS // 2)

    ys_sc[nb] = jnp.zeros(ys_sc.shape[1:], BF16)
    ys_sc[nb + 1] = jnp.zeros(ys_sc.shape[1:], BF16)
    liota = lax.broadcasted_iota(jnp.int32, (MOE_SUB, 2 * MOE_ROWS), 1).astype(F32)
    for s in range(nsub):
        sub = slice(s * MOE_SUB, (s + 1) * MOE_SUB)
        ne = pc_ref.shape[1] // 2
        pos = pc_ref[sub, col:col + 1]
        j_lo = jnp.minimum((cum(s) - base) // MOE_ROWS, nb)
        off = (base + j_lo * MOE_ROWS).astype(F32)
        sel = jnp.where(pos - off == liota, 1.0, 0.0).astype(BF16)
        rows = ys_sc[pl.ds(j_lo, 2)].reshape(2 * MOE_ROWS, ys_sc.shape[2])
        y = pc_ref[sub, ne + col:ne + col + 1] * _dot(sel, rows)
        if not first:
            y = yin_ref[sub, :].astype(F32) + y
        o_ref[sub, :] = y.astype(BF16)


def moe(h, comb_t, pos_t, cum_incl, w1, w3, w2):
    t, d = h.shape
    ne = comb_t.shape[0]
    tc = min(MOE_CHUNK, t)
    assert t % tc == 0 and tc % MOE_SUB == 0
    nsub = tc // MOE_SUB
    nt = t // MOE_SUB
    cum = jnp.concatenate([jnp.zeros((ne, 1), jnp.int32), cum_incl[:, ::LANES]], axis=1)
    cum = cum.reshape(-1)
    posr = pos_t.reshape(ne, 1, t)
    pc = jnp.concatenate([pos_t.astype(F32).T, comb_t.T], axis=1)
    once = pl.Buffered(1)
    y = None
    for e in range(ne):
        first = y is None
        chunk = pl.BlockSpec((tc, d), lambda c, cum: (c, 0))
        in_specs = [chunk,
                    pl.BlockSpec((1, 1, tc), lambda c, cum, e=e: (e, 0, c)),
                    pl.BlockSpec((tc, 2 * ne), lambda c, cum: (c, 0))]
        in_specs += [] if first else [chunk]
        in_specs += [pl.BlockSpec((1,) + w.shape[1:], lambda c, cum, e=e: (e, 0, 0), pipeline_mode=once)
                     for w in (w1, w3, w2)]
        grid_spec = pltpu.PrefetchScalarGridSpec(
            num_scalar_prefetch=1, grid=(t // tc,), in_specs=in_specs, out_specs=chunk,
            scratch_shapes=[pltpu.VMEM((MOE_ROWS, d), F32),
                            pltpu.VMEM((tc // MOE_ROWS + 2, MOE_ROWS, d), BF16)])
        y = pl.pallas_call(
            functools.partial(_moe_kernel, nsub=nsub, cum0=e * (nt + 1), first=first, col=e),
            grid_spec=grid_spec,
            out_shape=jax.ShapeDtypeStruct((t, d), BF16),
            input_output_aliases={} if first else {4: 0},
            compiler_params=pltpu.CompilerParams(dimension_semantics=("arbitrary",),
                                                 vmem_limit_bytes=MOE_VMEM_LIMIT),
            name="moe_expert",
        )(*([cum, h, posr, pc] + ([] if first else [y]) + [w1, w3, w2]))
    return y


def _residual_kernel(x_ref, y_ref, gt_ref, g_ref, o_ref, *, norm):
    x = x_ref[...] + gt_ref[0] * y_ref[...].astype(F32)
    o_ref[...] = _rms(x, g_ref[...]) if norm else x


def residual(x, y, gt, g, batch, norm, tm=512):
    t, d = x.shape
    tm = min(tm, t // batch)
    npb = (t // batch) // tm
    return pl.pallas_call(
        functools.partial(_residual_kernel, norm=norm),
        grid=(t // tm,),
        in_specs=[pl.BlockSpec((tm, d), lambda i: (i, 0)), pl.BlockSpec((tm, d), lambda i: (i, 0)),
                  pl.BlockSpec((1, 1, d), lambda i: (i // npb, 0, 0)), _full(g.shape)],
        out_specs=pl.BlockSpec((tm, d), lambda i: (i, 0)),
        out_shape=jax.ShapeDtypeStruct((t, d), F32),
        compiler_params=_cparams(("parallel",)),
        name="residual_norm" if norm else "residual",
    )(x, y, gt, g)


def _final_norm_kernel(x_ref, g_ref, o_ref):
    o_ref[...] = _rms(x_ref[...], g_ref[...])


def final_norm(x, g, tm=512):
    t, d = x.shape
    return pl.pallas_call(
        _final_norm_kernel,
        grid=(t // tm,),
        in_specs=[pl.BlockSpec((tm, d), lambda i: (i, 0)), _full(g.shape)],
        out_specs=pl.BlockSpec((tm, d), lambda i: (i, 0)),
        out_shape=jax.ShapeDtypeStruct((t, d), F32),
        compiler_params=_cparams(("parallel",)),
        name="final_norm",
    )(x, g)


def _rope_tables(seq, rotate):
    half = ROPE_DIM // 2
    nf = ROPE_DIM // 4
    if rotate:
        t = jnp.arange(seq, dtype=jnp.int32)
        rows = (t // GRID_WIDTH).astype(F32)
        cols = (t % GRID_WIDTH).astype(F32)
        inv = ROPE_THETA ** (-jnp.arange(0, half, 2, dtype=F32) / half)
        ang = jnp.stack([rows[:, None] * inv, cols[:, None] * inv], axis=1)
        cos, sin = jnp.cos(ang), jnp.sin(ang)
    else:
        cos = jnp.ones((seq, 2, nf), F32)
        sin = jnp.zeros((seq, 2, nf), F32)
    cpat = jnp.concatenate([cos, cos], axis=2).reshape(seq, ROPE_DIM)
    spat = jnp.concatenate([-sin, sin], axis=2).reshape(seq, ROPE_DIM)
    return cpat, spat


def _rope_inputs(seq, rotate):
    cpat, spat = _rope_tables(seq, rotate)
    tabq = jnp.concatenate([jnp.tile(cpat, (1, NUM_HEADS)), jnp.tile(spat, (1, NUM_HEADS))], axis=1)
    tabk = jnp.concatenate([cpat, spat, jnp.zeros((seq, LANES - 2 * ROPE_DIM), F32)], axis=1)
    return tabq, tabk


def _swap_perm():
    j = np.arange(ROPE_DIM)
    half = ROPE_DIM // 4
    return np.where((j % (2 * half)) < half, j + half, j - half)


def _placement(shared):
    nsrc = ROPE_DIM if shared else NUM_HEADS * ROPE_DIM
    e = np.zeros((LANES if shared else 2 * nsrc, NUM_HEADS * HEAD_PAD), np.float32)
    for hd in range(NUM_HEADS):
        for j in range(ROPE_DIM):
            col = hd * HEAD_PAD + NOPE_DIM + j
            src = j if shared else hd * ROPE_DIM + j
            e[src, col] = 1.0
            e[nsrc + src, col] = 1.0
    return jnp.asarray(e, dtype=BF16)


def _prep_layer(w_in, b_gate, g_q, w_uq, g_kv, w_ukv, w_oa, w_ob, w_oc, w_out, w_pool, s_pool):
    d = w_in.shape[0]
    q_lora, kv_lora = g_q.shape[0], g_kv.shape[0]
    perm = _swap_perm()
    off_kr = q_lora + kv_lora
    off_pool = off_kr + ROPE_DIM
    pool_w = w_pool.shape[0] * w_pool.shape[1]
    off_fft = off_pool + pool_w
    off_gate = w_in.shape[1] - NUM_BRANCH * d
    kr = w_in[:, off_kr:off_pool]
    wa = jnp.concatenate([w_in[:, :off_kr], kr, kr[:, perm],
                          jnp.zeros((d, LANES - 2 * ROPE_DIM), F32)], axis=1)
    qscale = (NOPE_DIM + ROPE_DIM) ** -0.5 * math.log2(math.e)
    wq = w_uq * qscale
    wqn = jnp.pad(wq[:, :, :NOPE_DIM], ((0, 0), (0, 0), (0, HEAD_PAD - NOPE_DIM)))
    wqr = wq[:, :, NOPE_DIM:]
    wqr = jnp.concatenate([wqr.reshape(q_lora, -1), wqr[:, :, perm].reshape(q_lora, -1)], axis=1)
    wkn = jnp.pad(w_ukv[:, :, :NOPE_DIM], ((0, 0), (0, 0), (0, HEAD_PAD - NOPE_DIM)))
    wv = jnp.pad(w_ukv[:, :, NOPE_DIM:], ((0, 0), (0, 0), (0, VT_ROWS - VAL_DIM)))
    vone = np.zeros((NUM_HEADS, VT_ROWS, 1), np.float32)
    vone[:, VAL_DIM] = 1.0
    return dict(
        wa=wa.astype(BF16), wpool=w_in[:, off_pool:off_fft].astype(BF16),
        wfft=w_in[:, off_fft:off_gate].astype(BF16), wgate=w_in[:, off_gate:].astype(BF16),
        bgate=b_gate.reshape(1, -1),
        gq=g_q.reshape(1, -1), wqn=wqn.reshape(q_lora, -1).astype(BF16), wqr=wqr.astype(BF16),
        eq=_placement(False), gkv=g_kv.reshape(1, -1),
        wkn=wkn.reshape(kv_lora, -1).astype(BF16), ek=_placement(True),
        wvt=wv.reshape(kv_lora, -1).T.astype(BF16), vone=jnp.asarray(vone.reshape(-1, 1)),
        woa=w_oa.astype(BF16), wob=w_ob.astype(BF16), woc=w_oc.astype(BF16), wout=w_out.astype(BF16),
        wp=w_pool.astype(BF16), sp=s_pool.reshape(1, -1),
    )


def _token_mixer(xs, modv1, modv2, lw, g1, g2, ropes, dft, batch, kv_ctx, tm):
    tabq, tabk = ropes
    q, k, vt, pool_in, fft_in, gate = proj_in(xs, modv1, g1, tabq, tabk, lw, batch, False, tm)
    if kv_ctx is None:
        attn = attention(q, k, vt, None, None, batch)
    else:
        attn = attention(q, kv_ctx[0], kv_ctx[1], k, vt, batch)
    pooled = pool_mixer(pool_in, lw["wp"], lw["sp"], batch)
    four = fourier_mixer(fft_in, batch, dft)
    x_new, h2 = merge(attn, pooled, four, gate, xs, modv2, lw, g2, batch, tm)
    return x_new, h2, (k, vt)


def kernel(x, c, ctx, c_ctx, w_mod, b_mod, g_norm1, w_in, b_gate, g_q, w_uq, g_kv, w_ukv, w_pool, s_pool,
           w_oa, w_ob, w_oc, w_out, g_norm2, ffn_w1, ffn_w3, ffn_w2, moe_router, moe_bias,
           moe_w1, moe_w3, moe_w2, g_final):
    batch, seq, d = x.shape
    n_ctx = ctx.shape[1]
    depth = w_mod.shape[0]
    n_exp = moe_router.shape[-1] if moe_router.shape[0] else 0
    xl = x.reshape(batch * seq, d)
    xc = ctx.reshape(batch * n_ctx, d)

    cvec = jnp.concatenate([c, c_ctx[None], jnp.zeros((8 - batch - 1, d), F32)], axis=0)
    mods = modulation(cvec, w_mod, b_mod)

    rope_l = _rope_inputs(seq, True)
    rope_c = _rope_inputs(n_ctx, False)
    dft_l = _dft_tables(seq)
    dft_c = _dft_tables(n_ctx)

    for layer in range(depth):
        last = layer == depth - 1
        lw = _prep_layer(w_in[layer], b_gate[layer], g_q[layer], w_uq[layer], g_kv[layer], w_ukv[layer],
                         w_oa[layer], w_ob[layer], w_oc[layer], w_out[layer], w_pool[layer], s_pool[layer])
        g1 = g_norm1[layer].reshape(1, d)
        g2 = g_norm2[layer].reshape(1, d)
        ml = mods[layer, :batch].reshape(batch, 6, d)
        mc = jnp.broadcast_to(mods[layer, batch].reshape(1, 6, d), (batch, 6, d))

        if last:
            k_c, vt_c = proj_in(xc, mc[:, 0:2], g1, rope_c[0], rope_c[1], lw, batch, True)
            kv_c = (k_c, vt_c)
        else:
            xc_new, h2_c, kv_c = _token_mixer(xc, mc[:, 0:2], mc[:, 2:5], lw, g1, g2, rope_c, dft_c,
                                              batch, None, 256)
        xl, h2_l, _ = _token_mixer(xl, ml[:, 0:2], ml[:, 2:5], lw, g1, g2, rope_l, dft_l, batch, kv_c, 512)
        if not last:
            xc = xc_new

        streams = [(xl, h2_l, ml)]
        if not last:
            streams.append((xc, h2_c, mc))
        fuse_final = last and layer % 2 == 1
        new = []
        for xs, h2, mm in streams:
            gt2 = mm[:, 5:6]
            li = layer // 2
            if layer % 2 == 0:
                xs = ffn(h2, xs, gt2, ffn_w1[li].astype(BF16), ffn_w3[li].astype(BF16),
                         ffn_w2[li].astype(BF16), batch)
            else:
                comb_t, pos_t, cum_incl = router(xs, mm[:, 3:5], g2, moe_router[li].T,
                                                 moe_bias[li].reshape(n_exp, 1), batch)
                y = moe(h2, comb_t, pos_t, cum_incl, moe_w1[li].astype(BF16), moe_w3[li].astype(BF16),
                        moe_w2[li].astype(BF16))
                xs = residual(xs, y, gt2, g_final.reshape(1, d), batch, fuse_final)
            new.append(xs)
        xl = new[0]
        if not last:
            xc = new[1]

    if not fuse_final:
        xl = final_norm(xl, g_final.reshape(1, d))
    return xl.reshape(batch, seq, d)
```

```python
import functools
import math

import numpy as np
import jax
import jax.numpy as jnp
from jax import lax
from jax.experimental import pallas as pl
from jax.experimental.pallas import tpu as pltpu

GRID_WIDTH = 64
NUM_HEADS = 8
NOPE_DIM = 64
ROPE_DIM = 32
VAL_DIM = 64
ROPE_THETA = 10000.0
POOL_WINS = (2, 4, 8, 16)
GROUP_CH = 128
NUM_BRANCH = 3
TOPK = 2
NORM_EPS = 1e-6

LANES = 128
HEAD_PAD = 128
VT_ROWS = 80
POOL_HALO = 16
DFT_N2 = 128
DFT_GROUP = 8
VMEM_LIMIT = 56 * 1024 * 1024
MOE_ROWS = 256
MOE_SUB = 256
MOE_CHUNK = 2048
MOE_WINDOW = 6
MOE_VMEM_LIMIT = 60 * 1024 * 1024
STALE_MAX_JUMP = 64.0

BF16 = jnp.bfloat16
F32 = jnp.float32


def _cparams(sem):
    return pltpu.CompilerParams(dimension_semantics=sem, vmem_limit_bytes=VMEM_LIMIT)


def _full(shape):
    n = len(shape)
    return pl.BlockSpec(shape, lambda *_: (0,) * n)


def _dot(a, b):
    return jnp.dot(a, b, preferred_element_type=F32)


def _dot_nt(a, b):
    return lax.dot_general(a, b, (((1,), (1,)), ((), ())), preferred_element_type=F32)


def _rms(x, g):
    return x * lax.rsqrt(jnp.mean(x * x, axis=-1, keepdims=True) + NORM_EPS) * g


def _mod_kernel(c_ref, w_ref, b_ref, o_ref):
    c = c_ref[...]
    s = c * jax.nn.sigmoid(c)
    w = w_ref[0]
    s_hi = s.astype(BF16)
    s_lo = (s - s_hi.astype(F32)).astype(BF16)
    w_hi = w.astype(BF16)
    w_lo = (w - w_hi.astype(F32)).astype(BF16)
    part = _dot(jnp.concatenate([s_hi, s_lo], axis=0), w_hi)
    rows = s.shape[0]
    o_ref[0] = part[:rows] + part[rows:] + _dot(s_hi, w_lo) + b_ref[0]


def modulation(cvec, w_mod, b_mod):
    depth, d, n = w_mod.shape
    tn = 1536
    return pl.pallas_call(
        _mod_kernel,
        grid=(depth, n // tn),
        in_specs=[pl.BlockSpec((8, d), lambda l, j: (0, 0)),
                  pl.BlockSpec((1, d, tn), lambda l, j: (l, 0, j)),
                  pl.BlockSpec((1, 1, tn), lambda l, j: (l, 0, j))],
        out_specs=pl.BlockSpec((1, 8, tn), lambda l, j: (l, 0, j)),
        out_shape=jax.ShapeDtypeStruct((depth, 8, n), F32),
        compiler_params=_cparams(("parallel", "parallel")),
        name="modulation",
    )(cvec, w_mod, b_mod.reshape(depth, 1, n))


def _proj_in_kernel(x_ref, mod_ref, g1_ref, tabq_ref, tabk_ref,
                    wa_ref, wpool_ref, wfft_ref, wgate_ref, bgate_ref,
                    gq_ref, wqn_ref, wqr_ref, eq_ref, gkv_ref, wkn_ref, ek_ref, wvt_ref, vone_ref,
                    *out_refs, q_lora, kv_lora, kv_only):
    if kv_only:
        k_ref, vt_ref = out_refs
    else:
        q_ref, k_ref, vt_ref, pool_ref, fft_ref, gate_ref = out_refs
    x = x_ref[...]
    mod = mod_ref[0]
    h = (_rms(x, g1_ref[...]) * (1.0 + mod[1:2]) + mod[0:1]).astype(BF16)
    pa = _dot(h, wa_ref[...])
    kvn = _rms(pa[:, q_lora:q_lora + kv_lora], gkv_ref[...]).astype(BF16)
    kr = (pa[:, q_lora + kv_lora:] * tabk_ref[...]).astype(BF16)
    k_all = (_dot(kvn, wkn_ref[...]) + _dot(kr, ek_ref[...])).astype(BF16)
    for hd in range(NUM_HEADS):
        k_ref[hd] = k_all[:, hd * HEAD_PAD:(hd + 1) * HEAD_PAD]
    vt_ref[0] = (_dot_nt(wvt_ref[...], kvn) + vone_ref[...]).astype(BF16)
    if kv_only:
        return
    qn = _rms(pa[:, :q_lora], gq_ref[...]).astype(BF16)
    qr = (_dot(qn, wqr_ref[...]) * tabq_ref[...]).astype(BF16)
    q_all = (_dot(qn, wqn_ref[...]) + _dot(qr, eq_ref[...])).astype(BF16)
    for hd in range(NUM_HEADS):
        q_ref[hd] = q_all[:, hd * HEAD_PAD:(hd + 1) * HEAD_PAD]
    pool_ref[...] = _dot(h, wpool_ref[...])
    fft_ref[...] = _dot(h, wfft_ref[...])
    gate_ref[...] = jax.nn.sigmoid(_dot(h, wgate_ref[...]) + bgate_ref[...]).astype(BF16)


def proj_in(x, modv, g1, tabq, tabk, lw, batch, kv_only, tm=256):
    t, d = x.shape
    seq = t // batch
    npb = seq // tm
    q_lora = lw["gq"].shape[1]
    kv_lora = lw["gkv"].shape[1]
    hw = NUM_HEADS * HEAD_PAD
    weights = [lw["wa"], lw["wpool"], lw["wfft"], lw["wgate"], lw["bgate"],
               lw["gq"], lw["wqn"], lw["wqr"], lw["eq"], lw["gkv"], lw["wkn"], lw["ek"],
               lw["wvt"], lw["vone"]]
    in_specs = [pl.BlockSpec((tm, d), lambda i: (i, 0)),
                pl.BlockSpec((1, 2, d), lambda i: (i // npb, 0, 0)),
                _full(g1.shape),
                pl.BlockSpec((tm, tabq.shape[1]), lambda i: (i % npb, 0)),
                pl.BlockSpec((tm, tabk.shape[1]), lambda i: (i % npb, 0))]
    in_specs += [_full(w.shape) for w in weights]
    head_spec = pl.BlockSpec((NUM_HEADS, tm, HEAD_PAD), lambda i: (0, i, 0))
    head_shape = jax.ShapeDtypeStruct((NUM_HEADS, t, HEAD_PAD), BF16)
    vt_spec = pl.BlockSpec((1, NUM_HEADS * VT_ROWS, tm), lambda i: (i // npb, 0, i % npb))
    vt_shape = jax.ShapeDtypeStruct((batch, NUM_HEADS * VT_ROWS, seq), BF16)
    if kv_only:
        out_specs = [head_spec, vt_spec]
        out_shape = [head_shape, vt_shape]
    else:
        pw, fw, gw = lw["wpool"].shape[1], lw["wfft"].shape[1], lw["wgate"].shape[1]
        out_specs = [head_spec, head_spec, vt_spec,
                     pl.BlockSpec((tm, pw), lambda i: (i, 0)),
                     pl.BlockSpec((tm, fw), lambda i: (i, 0)),
                     pl.BlockSpec((tm, gw), lambda i: (i, 0))]
        out_shape = [head_shape, head_shape, vt_shape,
                     jax.ShapeDtypeStruct((t, pw), F32),
                     jax.ShapeDtypeStruct((t, fw), F32),
                     jax.ShapeDtypeStruct((t, gw), BF16)]
    return pl.pallas_call(
        functools.partial(_proj_in_kernel, q_lora=q_lora, kv_lora=kv_lora, kv_only=kv_only),
        grid=(t // tm,),
        in_specs=in_specs, out_specs=out_specs, out_shape=out_shape,
        compiler_params=_cparams(("parallel",)),
        name="proj_in_kv" if kv_only else "proj_in",
    )(x, modv, g1, tabq, tabk, *weights)


def _attn_kernel(q_ref, kc_ref, vtc_ref, *rest, tk, tk_exact, n_lat_tiles):
    if n_lat_tiles:
        kl_ref, vtl_ref, o_ref = rest
    else:
        (o_ref,) = rest
    nh = q_ref.shape[0]

    def rows(hd):
        return slice(hd * VT_ROWS, (hd + 1) * VT_ROWS)

    def scores(ks):
        return tuple(_dot_nt(ks[hd], q_ref[hd]) for hd in range(nh))

    def update(ss, vts, state):
        out = []
        for hd in range(nh):
            smax = jnp.max(ss[hd], axis=0, keepdims=True)
            if state is None:
                m_new = smax
                acc = _dot(vts[hd], jnp.exp2(ss[hd] - m_new).astype(BF16))
            else:
                m_old, acc_old = state[hd]
                m_new = jnp.maximum(m_old, smax)
                pv = _dot(vts[hd], jnp.exp2(ss[hd] - m_new).astype(BF16))
                acc = acc_old * jnp.exp2(m_old - m_new) + pv
            out.append((m_new, acc))
        return tuple(out)

    def lat_k(off, n):
        return [kl_ref[hd, pl.ds(off, n), :] for hd in range(nh)]

    def lat_vt(off, n):
        return [vtl_ref[0, rows(hd), pl.ds(off, n)] for hd in range(nh)]

    def update_stale(ss, vts, state, jump):
        out = []
        for hd in range(nh):
            m_old, acc_old = state[hd]
            pv = _dot(vts[hd], jnp.exp2(ss[hd] - m_old).astype(BF16))
            smax = jnp.max(ss[hd], axis=0, keepdims=True)
            m_new = jnp.maximum(m_old, smax)
            out.append((m_new, (acc_old + pv) * jnp.exp2(m_old - m_new)))
            jump = jnp.maximum(jump, smax - m_old)
        return tuple(out), jump

    state = update(scores([kc_ref[hd] for hd in range(nh)]),
                   [vtc_ref[0, rows(hd), :] for hd in range(nh)], None)
    if n_lat_tiles:
        def fast_body(i, carry):
            off = pl.multiple_of(i * tk, tk)
            return update_stale(scores(lat_k(off, tk)), lat_vt(off, tk), *carry)
        fast, jump = lax.fori_loop(0, n_lat_tiles, fast_body,
                                   (state, jnp.zeros((1, q_ref.shape[1]), F32)))

        def exact_loop():
            def body(i, st):
                off = pl.multiple_of(i * tk_exact, tk_exact)
                return update(scores(lat_k(off, tk_exact)), lat_vt(off, tk_exact), st)
            return lax.fori_loop(0, n_lat_tiles * (tk // tk_exact), body, state)

        state = lax.cond(jnp.max(jump) > STALE_MAX_JUMP, exact_loop, lambda: fast)
    for pair in range(nh // 2):
        outs = []
        for hd in (2 * pair, 2 * pair + 1):
            acc = state[hd][1]
            outs.append(acc[:VAL_DIM] / acc[VAL_DIM:VAL_DIM + 1])
        o_ref[:, pair * 2 * VAL_DIM:(pair + 1) * 2 * VAL_DIM] = (
            jnp.concatenate(outs, axis=0).T.astype(BF16))


def attention(q, kc, vtc, kl, vtl, batch, tq=256, tk=2048):
    h, t, _ = q.shape
    lq = t // batch
    nq = lq // tq
    lc = kc.shape[1] // batch
    once = pl.Buffered(1)
    in_specs = [pl.BlockSpec((h, tq, HEAD_PAD), lambda b, i: (0, b * nq + i, 0)),
                pl.BlockSpec((h, lc, HEAD_PAD), lambda b, i: (0, b, 0), pipeline_mode=once),
                pl.BlockSpec((1, h * VT_ROWS, lc), lambda b, i: (b, 0, 0), pipeline_mode=once)]
    args = [q, kc, vtc]
    n_lat_tiles = 0
    if kl is not None:
        ll = kl.shape[1] // batch
        tk = min(tk, ll)
        n_lat_tiles = ll // tk
        in_specs += [pl.BlockSpec((h, ll, HEAD_PAD), lambda b, i: (0, b, 0), pipeline_mode=once),
                     pl.BlockSpec((1, h * VT_ROWS, ll), lambda b, i: (b, 0, 0), pipeline_mode=once)]
        args += [kl, vtl]
    return pl.pallas_call(
        functools.partial(_attn_kernel, tk=tk, tk_exact=min(tk, 512), n_lat_tiles=n_lat_tiles),
        grid=(batch, nq),
        in_specs=in_specs,
        out_specs=pl.BlockSpec((tq, h * VAL_DIM), lambda b, i: (b * nq + i, 0)),
        out_shape=jax.ShapeDtypeStruct((t, h * VAL_DIM), BF16),
        compiler_params=_cparams(("parallel", "arbitrary")),
        name="attention",
    )(*args)


def _pool_kernel(x_ref, prev_ref, next_ref, band_ref, wp_ref, sp_ref, o_ref, *, seq, tm):
    i = pl.program_id(0)
    npb = seq // tm
    pos0 = (i % npb) * tm
    has_prev = (pos0 > 0).astype(F32)
    has_next = (pos0 + tm < seq).astype(F32)
    x = x_ref[...]
    ext = jnp.concatenate([prev_ref[...] * has_prev, x, next_ref[...] * has_next], axis=0).astype(BF16)
    t = pos0 + lax.broadcasted_iota(jnp.int32, (tm, GROUP_CH), 0)
    outs = []
    for g, win in enumerate(POOL_WINS):
        left = win // 2
        right = win - 1 - left
        cnt = (jnp.minimum(t + right + 1, seq) - jnp.maximum(t - left, 0)).astype(F32)
        sl = slice(g * GROUP_CH, (g + 1) * GROUP_CH)
        wsum = _dot(band_ref[g], ext[:, sl])
        mixed = (wsum / cnt - x[:, sl]).astype(BF16)
        outs.append(_dot(mixed, wp_ref[g]))
    o_ref[...] = (jnp.concatenate(outs, axis=1) * sp_ref[...]).astype(BF16)


def _pool_bands(tm):
    r = np.arange(tm)[:, None]
    c = np.arange(tm + 2 * POOL_HALO)[None, :]
    d = c - POOL_HALO - r
    bands = [((d >= -(w // 2)) & (d <= w - 1 - w // 2)) for w in POOL_WINS]
    return jnp.asarray(np.stack(bands).astype(np.float32), dtype=BF16)


def pool_mixer(p, wp, sp, batch, tm=256):
    t, w = p.shape
    seq = t // batch
    tm = min(tm, seq)
    hb = tm // POOL_HALO
    nh = t // POOL_HALO
    return pl.pallas_call(
        functools.partial(_pool_kernel, seq=seq, tm=tm),
        grid=(t // tm,),
        in_specs=[pl.BlockSpec((tm, w), lambda i: (i, 0)),
                  pl.BlockSpec((POOL_HALO, w), lambda i: (jnp.maximum(i * hb - 1, 0), 0)),
                  pl.BlockSpec((POOL_HALO, w), lambda i: (jnp.minimum((i + 1) * hb, nh - 1), 0)),
                  _full((len(POOL_WINS), tm, tm + 2 * POOL_HALO)),
                  _full(wp.shape), _full(sp.shape)],
        out_specs=pl.BlockSpec((tm, w), lambda i: (i, 0)),
        out_shape=jax.ShapeDtypeStruct((t, w), BF16),
        compiler_params=_cparams(("parallel",)),
        name="pool_mixer",
    )(p, p, p, _pool_bands(tm), wp, sp)


def _dft_outer_kernel(f_ref, x_ref, o_ref):
    for j in range(x_ref.shape[2]):
        o_ref[0, :, j, :] = _dot(f_ref[...], x_ref[0, :, j, :].astype(BF16))


def _dft_inner_kernel(a_ref, g_ref, cs_ref, o_ref, *, scale):
    n2 = g_ref.shape[1] // 2
    for k in range(g_ref.shape[0]):
        rhs = a_ref[0, :, k].reshape(2 * n2, a_ref.shape[-1]).astype(BF16)
        p = _dot(g_ref[k], rhs)
        outs = []
        for g in range(p.shape[1] // GROUP_CH):
            sl = slice(g * GROUP_CH, (g + 1) * GROUP_CH)
            lhs = jnp.concatenate([p[:n2, sl], p[n2:, sl]], axis=1).astype(BF16)
            outs.append(_dot(lhs, cs_ref[...]))
        o_ref[0, :, k, :] = jnp.concatenate(outs, axis=1) * scale


def _dft_tables(seq):
    n2 = seq if seq <= 2 * DFT_N2 else DFT_N2
    n1 = seq // n2
    k2 = jnp.arange(n2, dtype=jnp.int32)[:, None]
    t2 = jnp.arange(n2, dtype=jnp.int32)[None, :]
    k1 = jnp.arange(n1, dtype=jnp.int32)[:, None, None]
    ang_a = ((t2 * k2) % n2).astype(F32) * (2.0 * math.pi / n2)
    ang_b = ((t2 * k1) % seq).astype(F32) * (2.0 * math.pi / seq)
    ca, sa, cb, sb = jnp.cos(ang_a), jnp.sin(ang_a), jnp.cos(ang_b), jnp.sin(ang_b)
    gr = ca * cb - sa * sb
    gi = -(sa * cb + ca * sb)
    gmat = jnp.concatenate([jnp.concatenate([gr, -gi], axis=2),
                            jnp.concatenate([gi, gr], axis=2)], axis=1).astype(BF16)
    a1 = jnp.arange(n1, dtype=jnp.int32)
    ang1 = ((a1[:, None] * a1[None, :]) % n1).astype(F32) * (2.0 * math.pi / n1)
    f1 = jnp.concatenate([jnp.cos(ang1), -jnp.sin(ang1)], axis=0).astype(BF16)
    c = jnp.arange(GROUP_CH, dtype=jnp.int32)
    angc = ((c[:, None] * c[None, :]) % GROUP_CH).astype(F32) * (2.0 * math.pi / GROUP_CH)
    cs = jnp.concatenate([jnp.cos(angc), jnp.sin(angc)], axis=0).astype(BF16)
    return gmat, f1, cs


def fourier_mixer(f, batch, tables):
    gmat, f1, cs = tables
    t, w = f.shape
    seq = t // batch
    n1, n2x2, _ = gmat.shape
    n2 = n2x2 // 2
    scale = 1.0 / math.sqrt(seq * GROUP_CH)
    if n1 > 1:
        a = pl.pallas_call(
            _dft_outer_kernel,
            grid=(batch, n2 // DFT_GROUP),
            in_specs=[_full(f1.shape), pl.BlockSpec((1, n1, DFT_GROUP, w), lambda b, j: (b, 0, j, 0))],
            out_specs=pl.BlockSpec((1, 2 * n1, DFT_GROUP, w), lambda b, j: (b, 0, j, 0)),
            out_shape=jax.ShapeDtypeStruct((batch, 2 * n1, n2, w), F32),
            compiler_params=_cparams(("parallel", "parallel")),
            name="dft_outer",
        )(f1, f.reshape(batch, n1, n2, w))
        a = a.reshape(batch, 2, n1, n2, w)
    else:
        fr = f.reshape(batch, 1, 1, n2, w)
        a = jnp.concatenate([fr, jnp.zeros_like(fr)], axis=1)
    kg = min(DFT_GROUP, n1)
    out = pl.pallas_call(
        functools.partial(_dft_inner_kernel, scale=scale),
        grid=(batch, n1 // kg),
        in_specs=[pl.BlockSpec((1, 2, kg, n2, w), lambda b, k: (b, 0, k, 0, 0)),
                  pl.BlockSpec((kg, 2 * n2, 2 * n2), lambda b, k: (k, 0, 0)),
                  _full(cs.shape)],
        out_specs=pl.BlockSpec((1, n2, kg, w), lambda b, k: (b, 0, k, 0)),
        out_shape=jax.ShapeDtypeStruct((batch, n2, n1, w), F32),
        compiler_params=_cparams(("parallel", "parallel")),
        name="dft_inner",
    )(a, gmat, cs)
    return out.reshape(t, w)


def _merge_kernel(a_ref, p_ref, f_ref, g_ref, x_ref, mod_ref, woa_ref, wob_ref, woc_ref, wout_ref,
                  g2_ref, xo_ref, h_ref):
    d = x_ref.shape[1]
    gate = g_ref[...].astype(F32)
    y = (gate[:, :d] * _dot(a_ref[...], woa_ref[...])
         + gate[:, d:2 * d] * _dot(p_ref[...], wob_ref[...])
         + gate[:, 2 * d:] * _dot(f_ref[...].astype(BF16), woc_ref[...]))
    mod = mod_ref[0]
    xn = x_ref[...] + mod[0:1] * _dot(y.astype(BF16), wout_ref[...])
    xo_ref[...] = xn
    h_ref[...] = (_rms(xn, g2_ref[...]) * (1.0 + mod[2:3]) + mod[1:2]).astype(BF16)


def merge(attn, pooled, four, gate, x, modv, lw, g2, batch, tm=256):
    t, d = x.shape
    npb = (t // batch) // tm
    row = lambda w: pl.BlockSpec((tm, w), lambda i: (i, 0))
    weights = [lw["woa"], lw["wob"], lw["woc"], lw["wout"], g2]
    return pl.pallas_call(
        _merge_kernel,
        grid=(t // tm,),
        in_specs=[row(attn.shape[1]), row(pooled.shape[1]), row(four.shape[1]), row(gate.shape[1]),
                  row(d), pl.BlockSpec((1, 3, d), lambda i: (i // npb, 0, 0))]
                 + [_full(w.shape) for w in weights],
        out_specs=[row(d), row(d)],
        out_shape=[jax.ShapeDtypeStruct((t, d), F32), jax.ShapeDtypeStruct((t, d), BF16)],
        compiler_params=_cparams(("parallel",)),
        name="merge",
    )(attn, pooled, four, gate, x, modv, *weights)


def _ffn_kernel(h_ref, x_ref, gt_ref, w1_ref, w3_ref, w2_ref, o_ref):
    h = h_ref[...]
    a = _dot(h, w1_ref[...])
    b = _dot(h, w3_ref[...])
    g = (a * jax.nn.sigmoid(a) * b).astype(BF16)
    o_ref[...] = x_ref[...] + gt_ref[0] * _dot(g, w2_ref[...])


def ffn(h, x, gt, w1, w3, w2, batch, tm=256):
    t, d = x.shape
    npb = (t // batch) // tm
    return pl.pallas_call(
        _ffn_kernel,
        grid=(t // tm,),
        in_specs=[pl.BlockSpec((tm, d), lambda i: (i, 0)),
                  pl.BlockSpec((tm, d), lambda i: (i, 0)),
                  pl.BlockSpec((1, 1, d), lambda i: (i // npb, 0, 0)),
                  _full(w1.shape), _full(w3.shape), _full(w2.shape)],
        out_specs=pl.BlockSpec((tm, d), lambda i: (i, 0)),
        out_shape=jax.ShapeDtypeStruct((t, d), F32),
        compiler_params=_cparams(("parallel",)),
        name="ffn",
    )(h, x, gt, w1, w3, w2)


def _router_kernel(x_ref, mod_ref, g2_ref, wr_ref, br_ref, tri_ref, comb_ref, pos_ref, cum_ref, cnt_sc):
    @pl.when(pl.program_id(0) == 0)
    def _():
        cnt_sc[...] = jnp.zeros_like(cnt_sc)

    mod = mod_ref[0]
    h = _rms(x_ref[...], g2_ref[...]) * (1.0 + mod[1:2]) + mod[0:1]
    h_hi = h.astype(BF16)
    h_lo = (h - h_hi.astype(F32)).astype(BF16)
    w = wr_ref[...]
    w_hi = w.astype(BF16)
    w_lo = (w - w_hi.astype(F32)).astype(BF16)
    ne = w.shape[0]
    part = _dot_nt(jnp.concatenate([w_hi, w_lo], axis=0), h_hi)
    logits = part[:ne] + part[ne:] + _dot_nt(w_hi, h_lo) + br_ref[...]
    ne = logits.shape[0]
    eidx = lax.broadcasted_iota(jnp.int32, logits.shape, 0)
    m1 = jnp.max(logits, axis=0, keepdims=True)
    i1 = jnp.min(jnp.where(logits == m1, eidx, ne), axis=0, keepdims=True)
    sel1 = eidx == i1
    rest = jnp.where(sel1, -jnp.inf, logits)
    m2 = jnp.max(rest, axis=0, keepdims=True)
    i2 = jnp.min(jnp.where(rest == m2, eidx, ne), axis=0, keepdims=True)
    sel2 = eidx == i2
    e2 = jnp.exp(m2 - m1)
    w1 = 1.0 / (1.0 + e2)
    comb_ref[...] = jnp.where(sel1, w1, 0.0) + jnp.where(sel2, e2 * w1, 0.0)
    self = jnp.where(sel1 | sel2, 1.0, 0.0)
    before = cnt_sc[...][:, 0:1]
    rank = before + _dot(self.astype(BF16), tri_ref[...])
    pos_ref[...] = jnp.where(self > 0.0, rank, -1.0).astype(jnp.int32)
    after = before + jnp.sum(self, axis=1, keepdims=True)
    cnt_sc[...] = jnp.broadcast_to(after, cnt_sc.shape)
    cum_ref[...] = jnp.broadcast_to(after, cum_ref.shape).astype(jnp.int32)


def router(x, modv, g2, wr_t, br, batch, tm=256):
    t, d = x.shape
    ne = wr_t.shape[0]
    npb = (t // batch) // tm
    nt = t // tm
    tri = jnp.asarray(np.triu(np.ones((tm, tm), np.float32), 1), dtype=BF16)
    return pl.pallas_call(
        _router_kernel,
        grid=(nt,),
        in_specs=[pl.BlockSpec((tm, d), lambda i: (i, 0)),
                  pl.BlockSpec((1, 2, d), lambda i: (i // npb, 0, 0)),
                  _full(g2.shape), _full(wr_t.shape), _full(br.shape), _full(tri.shape)],
        out_specs=[pl.BlockSpec((ne, tm), lambda i: (0, i)),
                   pl.BlockSpec((ne, tm), lambda i: (0, i)),
                   pl.BlockSpec((ne, LANES), lambda i: (0, i))],
        out_shape=[jax.ShapeDtypeStruct((ne, t), F32), jax.ShapeDtypeStruct((ne, t), jnp.int32),
                   jax.ShapeDtypeStruct((ne, nt * LANES), jnp.int32)],
        scratch_shapes=[pltpu.VMEM((ne, LANES), F32)],
        compiler_params=_cparams(("arbitrary",)),
        name="router",
    )(x, modv, g2, wr_t, br, tri)


def _moe_kernel(cum_ref, h_ref, posr_ref, pc_ref, *rest, nsub, cum0, first, col):
    if first:
        w1_ref, w3_ref, w2_ref, o_ref, hc_sc, ys_sc = rest
    else:
        yin_ref, w1_ref, w3_ref, w2_ref, o_ref, hc_sc, ys_sc = rest
    c = pl.program_id(0)

    def cum(i):
        return cum_ref[cum0 + c * nsub + i]

    base = cum(0)
    total = cum(nsub) - base
    nfull = total // MOE_ROWS
    rem = total - nfull * MOE_ROWS
    nb = nfull + (rem > 0).astype(jnp.int32)
    nwin = min(MOE_WINDOW, nsub)

    def expert_block(j, nrows):
        off = base + j * MOE_ROWS
        s_lo = sum((cum(s + 1) <= off).astype(jnp.int32) for s in range(nsub))
        s_hi = sum((cum(s) < off + nrows).astype(jnp.int32) for s in range(nsub))

        def compact(win, width):
            iota = lax.broadcasted_iota(jnp.int32, (nrows, width), 0)
            sel = jnp.where(posr_ref[0, :, win] - off == iota, 1.0, 0.0).astype(BF16)
            hc_sc[:nrows] = _dot(sel, h_ref[win, :]).astype(BF16)

        @pl.when(s_hi - s_lo <= nwin)
        def _():
            start = pl.multiple_of(jnp.minimum(s_lo, nsub - nwin) * MOE_SUB, MOE_SUB)
            compact(pl.ds(start, nwin * MOE_SUB), nwin * MOE_SUB)

        if nwin < nsub:
            @pl.when(s_hi - s_lo > nwin)
            def _():
                compact(slice(None), nsub * MOE_SUB)

        hc = hc_sc[:nrows]
        a = _dot(hc, w1_ref[0])
        b = _dot(hc, w3_ref[0])
        g = (a * jax.nn.sigmoid(a) * b).astype(BF16)
        ys_sc[j, :nrows] = _dot(g, w2_ref[0]).astype(BF16)
        if nrows < MOE_ROWS:
            ys_sc[j, nrows:] = jnp.zeros((MOE_ROWS - nrows, ys_sc.shape[2]), BF16)

    def full_block(j, carry):
        expert_block(j, MOE_ROWS)
        return carry

    lax.fori_loop(0, nfull, full_block, 0)

    @pl.when(rem > MOE_ROWS // 2)
    def _():
        expert_block(nfull, MOE_ROWS)

    @pl.when((rem > 0) & (rem <= MOE_ROWS // 2))
    def _():
        expert_block(nfull, MOE_ROWS // 2)

    ys_sc[nb] = jnp.zeros(ys_sc.shape[1:], BF16)
    ys_sc[nb + 1] = jnp.zeros(ys_sc.shape[1:], BF16)
    liota = lax.broadcasted_iota(jnp.int32, (MOE_SUB, 2 * MOE_ROWS), 1).astype(F32)
    for s in range(nsub):
        sub = slice(s * MOE_SUB, (s + 1) * MOE_SUB)
        ne = pc_ref.shape[1] // 2
        pos = pc_ref[sub, col:col + 1]
        j_lo = jnp.minimum((cum(s) - base) // MOE_ROWS, nb)
        off = (base + j_lo * MOE_ROWS).astype(F32)
        sel = jnp.where(pos - off == liota, 1.0, 0.0).astype(BF16)
        rows = ys_sc[pl.ds(j_lo, 2)].reshape(2 * MOE_ROWS, ys_sc.shape[2])
        y = pc_ref[sub, ne + col:ne + col + 1] * _dot(sel, rows)
        if not first:
            y = yin_ref[sub, :].astype(F32) + y
        o_ref[sub, :] = y.astype(BF16)


def moe(h, comb_t, pos_t, cum_incl, w1, w3, w2):
    t, d = h.shape
    ne = comb_t.shape[0]
    tc = min(MOE_CHUNK, t)
    assert t % tc == 0 and tc % MOE_SUB == 0
    nsub = tc // MOE_SUB
    nt = t // MOE_SUB
    cum = jnp.concatenate([jnp.zeros((ne, 1), jnp.int32), cum_incl[:, ::LANES]], axis=1)
    cum = cum.reshape(-1)
    posr = pos_t.reshape(ne, 1, t)
    pc = jnp.concatenate([pos_t.astype(F32).T, comb_t.T], axis=1)
    once = pl.Buffered(1)
    y = None
    for e in range(ne):
        first = y is None
        chunk = pl.BlockSpec((tc, d), lambda c, cum: (c, 0))
        in_specs = [chunk,
                    pl.BlockSpec((1, 1, tc), lambda c, cum, e=e: (e, 0, c)),
                    pl.BlockSpec((tc, 2 * ne), lambda c, cum: (c, 0))]
        in_specs += [] if first else [chunk]
        in_specs += [pl.BlockSpec((1,) + w.shape[1:], lambda c, cum, e=e: (e, 0, 0), pipeline_mode=once)
                     for w in (w1, w3, w2)]
        grid_spec = pltpu.PrefetchScalarGridSpec(
            num_scalar_prefetch=1, grid=(t // tc,), in_specs=in_specs, out_specs=chunk,
            scratch_shapes=[pltpu.VMEM((MOE_ROWS, d), BF16),
                            pltpu.VMEM((tc // MOE_ROWS + 2, MOE_ROWS, d), BF16)])
        y = pl.pallas_call(
            functools.partial(_moe_kernel, nsub=nsub, cum0=e * (nt + 1), first=first, col=e),
            grid_spec=grid_spec,
            out_shape=jax.ShapeDtypeStruct((t, d), BF16),
            input_output_aliases={} if first else {4: 0},
            compiler_params=pltpu.CompilerParams(dimension_semantics=("arbitrary",),
                                                 vmem_limit_bytes=MOE_VMEM_LIMIT),
            name="moe_expert",
        )(*([cum, h, posr, pc] + ([] if first else [y]) + [w1, w3, w2]))
    return y


def _residual_kernel(x_ref, y_ref, gt_ref, g_ref, o_ref, *, norm):
    x = x_ref[...] + gt_ref[0] * y_ref[...].astype(F32)
    o_ref[...] = _rms(x, g_ref[...]) if norm else x


def residual(x, y, gt, g, batch, norm, tm=512):
    t, d = x.shape
    tm = min(tm, t // batch)
    npb = (t // batch) // tm
    return pl.pallas_call(
        functools.partial(_residual_kernel, norm=norm),
        grid=(t // tm,),
        in_specs=[pl.BlockSpec((tm, d), lambda i: (i, 0)), pl.BlockSpec((tm, d), lambda i: (i, 0)),
                  pl.BlockSpec((1, 1, d), lambda i: (i // npb, 0, 0)), _full(g.shape)],
        out_specs=pl.BlockSpec((tm, d), lambda i: (i, 0)),
        out_shape=jax.ShapeDtypeStruct((t, d), F32),
        compiler_params=_cparams(("parallel",)),
        name="residual_norm" if norm else "residual",
    )(x, y, gt, g)


def _final_norm_kernel(x_ref, g_ref, o_ref):
    o_ref[...] = _rms(x_ref[...], g_ref[...])


def final_norm(x, g, tm=512):
    t, d = x.shape
    return pl.pallas_call(
        _final_norm_kernel,
        grid=(t // tm,),
        in_specs=[pl.BlockSpec((tm, d), lambda i: (i, 0)), _full(g.shape)],
        out_specs=pl.BlockSpec((tm, d), lambda i: (i, 0)),
        out_shape=jax.ShapeDtypeStruct((t, d), F32),
        compiler_params=_cparams(("parallel",)),
        name="final_norm",
    )(x, g)


def _rope_tables(seq, rotate):
    half = ROPE_DIM // 2
    nf = ROPE_DIM // 4
    if rotate:
        t = jnp.arange(seq, dtype=jnp.int32)
        rows = (t // GRID_WIDTH).astype(F32)
        cols = (t % GRID_WIDTH).astype(F32)
        inv = ROPE_THETA ** (-jnp.arange(0, half, 2, dtype=F32) / half)
        ang = jnp.stack([rows[:, None] * inv, cols[:, None] * inv], axis=1)
        cos, sin = jnp.cos(ang), jnp.sin(ang)
    else:
        cos = jnp.ones((seq, 2, nf), F32)
        sin = jnp.zeros((seq, 2, nf), F32)
    cpat = jnp.concatenate([cos, cos], axis=2).reshape(seq, ROPE_DIM)
    spat = jnp.concatenate([-sin, sin], axis=2).reshape(seq, ROPE_DIM)
    return cpat, spat


def _rope_inputs(seq, rotate):
    cpat, spat = _rope_tables(seq, rotate)
    tabq = jnp.concatenate([jnp.tile(cpat, (1, NUM_HEADS)), jnp.tile(spat, (1, NUM_HEADS))], axis=1)
    tabk = jnp.concatenate([cpat, spat, jnp.zeros((seq, LANES - 2 * ROPE_DIM), F32)], axis=1)
    return tabq, tabk


def _swap_perm():
    j = np.arange(ROPE_DIM)
    half = ROPE_DIM // 4
    return np.where((j % (2 * half)) < half, j + half, j - half)


def _placement(shared):
    nsrc = ROPE_DIM if shared else NUM_HEADS * ROPE_DIM
    e = np.zeros((LANES if shared else 2 * nsrc, NUM_HEADS * HEAD_PAD), np.float32)
    for hd in range(NUM_HEADS):
        for j in range(ROPE_DIM):
            col = hd * HEAD_PAD + NOPE_DIM + j
            src = j if shared else hd * ROPE_DIM + j
            e[src, col] = 1.0
            e[nsrc + src, col] = 1.0
    return jnp.asarray(e, dtype=BF16)


def _prep_layer(w_in, b_gate, g_q, w_uq, g_kv, w_ukv, w_oa, w_ob, w_oc, w_out, w_pool, s_pool):
    d = w_in.shape[0]
    q_lora, kv_lora = g_q.shape[0], g_kv.shape[0]
    perm = _swap_perm()
    off_kr = q_lora + kv_lora
    off_pool = off_kr + ROPE_DIM
    pool_w = w_pool.shape[0] * w_pool.shape[1]
    off_fft = off_pool + pool_w
    off_gate = w_in.shape[1] - NUM_BRANCH * d
    kr = w_in[:, off_kr:off_pool]
    wa = jnp.concatenate([w_in[:, :off_kr], kr, kr[:, perm],
                          jnp.zeros((d, LANES - 2 * ROPE_DIM), F32)], axis=1)
    qscale = (NOPE_DIM + ROPE_DIM) ** -0.5 * math.log2(math.e)
    wq = w_uq * qscale
    wqn = jnp.pad(wq[:, :, :NOPE_DIM], ((0, 0), (0, 0), (0, HEAD_PAD - NOPE_DIM)))
    wqr = wq[:, :, NOPE_DIM:]
    wqr = jnp.concatenate([wqr.reshape(q_lora, -1), wqr[:, :, perm].reshape(q_lora, -1)], axis=1)
    wkn = jnp.pad(w_ukv[:, :, :NOPE_DIM], ((0, 0), (0, 0), (0, HEAD_PAD - NOPE_DIM)))
    wv = jnp.pad(w_ukv[:, :, NOPE_DIM:], ((0, 0), (0, 0), (0, VT_ROWS - VAL_DIM)))
    vone = np.zeros((NUM_HEADS, VT_ROWS, 1), np.float32)
    vone[:, VAL_DIM] = 1.0
    return dict(
        wa=wa.astype(BF16), wpool=w_in[:, off_pool:off_fft].astype(BF16),
        wfft=w_in[:, off_fft:off_gate].astype(BF16), wgate=w_in[:, off_gate:].astype(BF16),
        bgate=b_gate.reshape(1, -1),
        gq=g_q.reshape(1, -1), wqn=wqn.reshape(q_lora, -1).astype(BF16), wqr=wqr.astype(BF16),
        eq=_placement(False), gkv=g_kv.reshape(1, -1),
        wkn=wkn.reshape(kv_lora, -1).astype(BF16), ek=_placement(True),
        wvt=wv.reshape(kv_lora, -1).T.astype(BF16), vone=jnp.asarray(vone.reshape(-1, 1)),
        woa=w_oa.astype(BF16), wob=w_ob.astype(BF16), woc=w_oc.astype(BF16), wout=w_out.astype(BF16),
        wp=w_pool.astype(BF16), sp=s_pool.reshape(1, -1),
    )


def _token_mixer(xs, modv1, modv2, lw, g1, g2, ropes, dft, batch, kv_ctx, tm):
    tabq, tabk = ropes
    q, k, vt, pool_in, fft_in, gate = proj_in(xs, modv1, g1, tabq, tabk, lw, batch, False, tm)
    if kv_ctx is None:
        attn = attention(q, k, vt, None, None, batch)
    else:
        attn = attention(q, kv_ctx[0], kv_ctx[1], k, vt, batch)
    pooled = pool_mixer(pool_in, lw["wp"], lw["sp"], batch)
    four = fourier_mixer(fft_in, batch, dft)
    x_new, h2 = merge(attn, pooled, four, gate, xs, modv2, lw, g2, batch, tm)
    return x_new, h2, (k, vt)


def kernel(x, c, ctx, c_ctx, w_mod, b_mod, g_norm1, w_in, b_gate, g_q, w_uq, g_kv, w_ukv, w_pool, s_pool,
           w_oa, w_ob, w_oc, w_out, g_norm2, ffn_w1, ffn_w3, ffn_w2, moe_router, moe_bias,
           moe_w1, moe_w3, moe_w2, g_final):
    batch, seq, d = x.shape
    n_ctx = ctx.shape[1]
    depth = w_mod.shape[0]
    n_exp = moe_router.shape[-1] if moe_router.shape[0] else 0
    xl = x.reshape(batch * seq, d)
    xc = ctx.reshape(batch * n_ctx, d)

    cvec = jnp.concatenate([c, c_ctx[None], jnp.zeros((8 - batch - 1, d), F32)], axis=0)
    mods = modulation(cvec, w_mod, b_mod)

    rope_l = _rope_inputs(seq, True)
    rope_c = _rope_inputs(n_ctx, False)
    dft_l = _dft_tables(seq)
    dft_c = _dft_tables(n_ctx)

    for layer in range(depth):
        last = layer == depth - 1
        lw = _prep_layer(w_in[layer], b_gate[layer], g_q[layer], w_uq[layer], g_kv[layer], w_ukv[layer],
                         w_oa[layer], w_ob[layer], w_oc[layer], w_out[layer], w_pool[layer], s_pool[layer])
        g1 = g_norm1[layer].reshape(1, d)
        g2 = g_norm2[layer].reshape(1, d)
        ml = mods[layer, :batch].reshape(batch, 6, d)
        mc = jnp.broadcast_to(mods[layer, batch].reshape(1, 6, d), (batch, 6, d))

        if last:
            k_c, vt_c = proj_in(xc, mc[:, 0:2], g1, rope_c[0], rope_c[1], lw, batch, True)
            kv_c = (k_c, vt_c)
        else:
            xc_new, h2_c, kv_c = _token_mixer(xc, mc[:, 0:2], mc[:, 2:5], lw, g1, g2, rope_c, dft_c,
                                              batch, None, 256)
        xl, h2_l, _ = _token_mixer(xl, ml[:, 0:2], ml[:, 2:5], lw, g1, g2, rope_l, dft_l, batch, kv_c, 512)
        if not last:
            xc = xc_new

        streams = [(xl, h2_l, ml)]
        if not last:
            streams.append((xc, h2_c, mc))
        fuse_final = last and layer % 2 == 1
        new = []
        for xs, h2, mm in streams:
            gt2 = mm[:, 5:6]
            li = layer // 2
            if layer % 2 == 0:
                xs = ffn(h2, xs, gt2, ffn_w1[li].astype(BF16), ffn_w3[li].astype(BF16),
                         ffn_w2[li].astype(BF16), batch)
            else:
                comb_t, pos_t, cum_incl = router(xs, mm[:, 3:5], g2, moe_router[li].T,
                                                 moe_bias[li].reshape(n_exp, 1), batch)
                y = moe(h2, comb_t, pos_t, cum_incl, moe_w1[li].astype(BF16), moe_w3[li].astype(BF16),
                        moe_w2[li].astype(BF16))
                xs = residual(xs, y, gt2, g_final.reshape(1, d), batch, fuse_final)
            new.append(xs)
        xl = new[0]
        if not last:
            xc = new[1]

    if not fuse_final:
        xl = final_norm(xl, g_final.reshape(1, d))
    return xl.reshape(batch, seq, d)
```

```python
import functools
import math

import numpy as np
import jax
import jax.numpy as jnp
from jax import lax
from jax.experimental import pallas as pl
from jax.experimental.pallas import tpu as pltpu

GRID_WIDTH = 64
NUM_HEADS = 8
NOPE_DIM = 64
ROPE_DIM = 32
VAL_DIM = 64
ROPE_THETA = 10000.0
POOL_WINS = (2, 4, 8, 16)
GROUP_CH = 128
NUM_BRANCH = 3
TOPK = 2
NORM_EPS = 1e-6

LANES = 128
HEAD_PAD = 128
VT_ROWS = 80
POOL_HALO = 16
DFT_N2 = 128
DFT_GROUP = 8
VMEM_LIMIT = 56 * 1024 * 1024
MOE_ROWS = 256
MOE_SUB = 256
MOE_CHUNK = 2048
MOE_WINDOW = 6
MOE_VMEM_LIMIT = 60 * 1024 * 1024
STALE_MAX_JUMP = 64.0

BF16 = jnp.bfloat16
F32 = jnp.float32


def _cparams(sem):
    return pltpu.CompilerParams(dimension_semantics=sem, vmem_limit_bytes=VMEM_LIMIT)


def _full(shape):
    n = len(shape)
    return pl.BlockSpec(shape, lambda *_: (0,) * n)


def _dot(a, b):
    return jnp.dot(a, b, preferred_element_type=F32)


def _dot_nt(a, b):
    return lax.dot_general(a, b, (((1,), (1,)), ((), ())), preferred_element_type=F32)


def _rms(x, g):
    return x * lax.rsqrt(jnp.mean(x * x, axis=-1, keepdims=True) + NORM_EPS) * g


def _mod_kernel(c_ref, w_ref, b_ref, o_ref):
    c = c_ref[...]
    s = c * jax.nn.sigmoid(c)
    w = w_ref[0]
    s_hi = s.astype(BF16)
    s_lo = (s - s_hi.astype(F32)).astype(BF16)
    w_hi = w.astype(BF16)
    w_lo = (w - w_hi.astype(F32)).astype(BF16)
    part = _dot(jnp.concatenate([s_hi, s_lo], axis=0), w_hi)
    rows = s.shape[0]
    o_ref[0] = part[:rows] + part[rows:] + _dot(s_hi, w_lo) + b_ref[0]


def modulation(cvec, w_mod, b_mod):
    depth, d, n = w_mod.shape
    tn = 1536
    return pl.pallas_call(
        _mod_kernel,
        grid=(depth, n // tn),
        in_specs=[pl.BlockSpec((8, d), lambda l, j: (0, 0)),
                  pl.BlockSpec((1, d, tn), lambda l, j: (l, 0, j)),
                  pl.BlockSpec((1, 1, tn), lambda l, j: (l, 0, j))],
        out_specs=pl.BlockSpec((1, 8, tn), lambda l, j: (l, 0, j)),
        out_shape=jax.ShapeDtypeStruct((depth, 8, n), F32),
        compiler_params=_cparams(("parallel", "parallel")),
        name="modulation",
    )(cvec, w_mod, b_mod.reshape(depth, 1, n))


def _proj_in_kernel(x_ref, mod_ref, g1_ref, tabq_ref, tabk_ref,
                    wa_ref, wpool_ref, wfft_ref, wgate_ref, bgate_ref,
                    gq_ref, wqn_ref, wqr_ref, eq_ref, gkv_ref, wkn_ref, ek_ref, wvt_ref, vone_ref,
                    *out_refs, q_lora, kv_lora, kv_only):
    if kv_only:
        k_ref, vt_ref = out_refs
    else:
        q_ref, k_ref, vt_ref, pool_ref, fft_ref, gate_ref = out_refs
    x = x_ref[...]
    mod = mod_ref[0]
    h = (_rms(x, g1_ref[...]) * (1.0 + mod[1:2]) + mod[0:1]).astype(BF16)
    pa = _dot(h, wa_ref[...])
    kvn = _rms(pa[:, q_lora:q_lora + kv_lora], gkv_ref[...]).astype(BF16)
    kr = (pa[:, q_lora + kv_lora:] * tabk_ref[...]).astype(BF16)
    k_all = (_dot(kvn, wkn_ref[...]) + _dot(kr, ek_ref[...])).astype(BF16)
    for hd in range(NUM_HEADS):
        k_ref[hd] = k_all[:, hd * HEAD_PAD:(hd + 1) * HEAD_PAD]
    vt_ref[0] = (_dot_nt(wvt_ref[...], kvn) + vone_ref[...]).astype(BF16)
    if kv_only:
        return
    qn = _rms(pa[:, :q_lora], gq_ref[...]).astype(BF16)
    qr = (_dot(qn, wqr_ref[...]) * tabq_ref[...]).astype(BF16)
    q_all = (_dot(qn, wqn_ref[...]) + _dot(qr, eq_ref[...])).astype(BF16)
    for hd in range(NUM_HEADS):
        q_ref[hd] = q_all[:, hd * HEAD_PAD:(hd + 1) * HEAD_PAD]
    pool_ref[...] = _dot(h, wpool_ref[...])
    fft_ref[...] = _dot(h, wfft_ref[...])
    gate_ref[...] = jax.nn.sigmoid(_dot(h, wgate_ref[...]) + bgate_ref[...]).astype(BF16)


def proj_in(x, modv, g1, tabq, tabk, lw, batch, kv_only, tm=256):
    t, d = x.shape
    seq = t // batch
    npb = seq // tm
    q_lora = lw["gq"].shape[1]
    kv_lora = lw["gkv"].shape[1]
    hw = NUM_HEADS * HEAD_PAD
    weights = [lw["wa"], lw["wpool"], lw["wfft"], lw["wgate"], lw["bgate"],
               lw["gq"], lw["wqn"], lw["wqr"], lw["eq"], lw["gkv"], lw["wkn"], lw["ek"],
               lw["wvt"], lw["vone"]]
    in_specs = [pl.BlockSpec((tm, d), lambda i: (i, 0)),
                pl.BlockSpec((1, 2, d), lambda i: (i // npb, 0, 0)),
                _full(g1.shape),
                pl.BlockSpec((tm, tabq.shape[1]), lambda i: (i % npb, 0)),
                pl.BlockSpec((tm, tabk.shape[1]), lambda i: (i % npb, 0))]
    in_specs += [_full(w.shape) for w in weights]
    head_spec = pl.BlockSpec((NUM_HEADS, tm, HEAD_PAD), lambda i: (0, i, 0))
    head_shape = jax.ShapeDtypeStruct((NUM_HEADS, t, HEAD_PAD), BF16)
    vt_spec = pl.BlockSpec((1, NUM_HEADS * VT_ROWS, tm), lambda i: (i // npb, 0, i % npb))
    vt_shape = jax.ShapeDtypeStruct((batch, NUM_HEADS * VT_ROWS, seq), BF16)
    if kv_only:
        out_specs = [head_spec, vt_spec]
        out_shape = [head_shape, vt_shape]
    else:
        pw, fw, gw = lw["wpool"].shape[1], lw["wfft"].shape[1], lw["wgate"].shape[1]
        out_specs = [head_spec, head_spec, vt_spec,
                     pl.BlockSpec((tm, pw), lambda i: (i, 0)),
                     pl.BlockSpec((tm, fw), lambda i: (i, 0)),
                     pl.BlockSpec((tm, gw), lambda i: (i, 0))]
        out_shape = [head_shape, head_shape, vt_shape,
                     jax.ShapeDtypeStruct((t, pw), F32),
                     jax.ShapeDtypeStruct((t, fw), F32),
                     jax.ShapeDtypeStruct((t, gw), BF16)]
    return pl.pallas_call(
        functools.partial(_proj_in_kernel, q_lora=q_lora, kv_lora=kv_lora, kv_only=kv_only),
        grid=(t // tm,),
        in_specs=in_specs, out_specs=out_specs, out_shape=out_shape,
        compiler_params=_cparams(("parallel",)),
        name="proj_in_kv" if kv_only else "proj_in",
    )(x, modv, g1, tabq, tabk, *weights)


def _attn_kernel(q_ref, kc_ref, vtc_ref, *rest, tk, tk_exact, n_lat_tiles):
    if n_lat_tiles:
        kl_ref, vtl_ref, o_ref = rest
    else:
        (o_ref,) = rest
    nh = q_ref.shape[0]

    def rows(hd):
        return slice(hd * VT_ROWS, (hd + 1) * VT_ROWS)

    def scores(ks):
        return tuple(_dot_nt(ks[hd], q_ref[hd]) for hd in range(nh))

    def update(ss, vts, state):
        out = []
        for hd in range(nh):
            smax = jnp.max(ss[hd], axis=0, keepdims=True)
            if state is None:
                m_new = smax
                acc = _dot(vts[hd], jnp.exp2(ss[hd] - m_new).astype(BF16))
            else:
                m_old, acc_old = state[hd]
                m_new = jnp.maximum(m_old, smax)
                pv = _dot(vts[hd], jnp.exp2(ss[hd] - m_new).astype(BF16))
                acc = acc_old * jnp.exp2(m_old - m_new) + pv
            out.append((m_new, acc))
        return tuple(out)

    def lat_k(off, n):
        return [kl_ref[hd, pl.ds(off, n), :] for hd in range(nh)]

    def lat_vt(off, n):
        return [vtl_ref[0, rows(hd), pl.ds(off, n)] for hd in range(nh)]

    def update_stale(ss, vts, state, jump):
        out = []
        for hd in range(nh):
            m_old, acc_old = state[hd]
            pv = _dot(vts[hd], jnp.exp2(ss[hd] - m_old).astype(BF16))
            smax = jnp.max(ss[hd], axis=0, keepdims=True)
            m_new = jnp.maximum(m_old, smax)
            out.append((m_new, (acc_old + pv) * jnp.exp2(m_old - m_new)))
            jump = jnp.maximum(jump, smax - m_old)
        return tuple(out), jump

    state = update(scores([kc_ref[hd] for hd in range(nh)]),
                   [vtc_ref[0, rows(hd), :] for hd in range(nh)], None)
    if n_lat_tiles:
        def fast_body(i, carry):
            off = pl.multiple_of(i * tk, tk)
            return update_stale(scores(lat_k(off, tk)), lat_vt(off, tk), *carry)
        fast, jump = lax.fori_loop(0, n_lat_tiles, fast_body,
                                   (state, jnp.zeros((1, q_ref.shape[1]), F32)))

        def exact_loop():
            def body(i, st):
                off = pl.multiple_of(i * tk_exact, tk_exact)
                return update(scores(lat_k(off, tk_exact)), lat_vt(off, tk_exact), st)
            return lax.fori_loop(0, n_lat_tiles * (tk // tk_exact), body, state)

        state = lax.cond(jnp.max(jump) > STALE_MAX_JUMP, exact_loop, lambda: fast)
    for pair in range(nh // 2):
        outs = []
        for hd in (2 * pair, 2 * pair + 1):
            acc = state[hd][1]
            outs.append(acc[:VAL_DIM] / acc[VAL_DIM:VAL_DIM + 1])
        o_ref[:, pair * 2 * VAL_DIM:(pair + 1) * 2 * VAL_DIM] = (
            jnp.concatenate(outs, axis=0).T.astype(BF16))


def attention(q, kc, vtc, kl, vtl, batch, tq=256, tk=2048):
    h, t, _ = q.shape
    lq = t // batch
    nq = lq // tq
    lc = kc.shape[1] // batch
    once = pl.Buffered(1)
    in_specs = [pl.BlockSpec((h, tq, HEAD_PAD), lambda b, i: (0, b * nq + i, 0)),
                pl.BlockSpec((h, lc, HEAD_PAD), lambda b, i: (0, b, 0), pipeline_mode=once),
                pl.BlockSpec((1, h * VT_ROWS, lc), lambda b, i: (b, 0, 0), pipeline_mode=once)]
    args = [q, kc, vtc]
    n_lat_tiles = 0
    if kl is not None:
        ll = kl.shape[1] // batch
        tk = min(tk, ll)
        n_lat_tiles = ll // tk
        in_specs += [pl.BlockSpec((h, ll, HEAD_PAD), lambda b, i: (0, b, 0), pipeline_mode=once),
                     pl.BlockSpec((1, h * VT_ROWS, ll), lambda b, i: (b, 0, 0), pipeline_mode=once)]
        args += [kl, vtl]
    return pl.pallas_call(
        functools.partial(_attn_kernel, tk=tk, tk_exact=min(tk, 512), n_lat_tiles=n_lat_tiles),
        grid=(batch, nq),
        in_specs=in_specs,
        out_specs=pl.BlockSpec((tq, h * VAL_DIM), lambda b, i: (b * nq + i, 0)),
        out_shape=jax.ShapeDtypeStruct((t, h * VAL_DIM), BF16),
        compiler_params=_cparams(("parallel", "arbitrary")),
        name="attention",
    )(*args)


def _pool_kernel(x_ref, prev_ref, next_ref, band_ref, wp_ref, sp_ref, o_ref, *, seq, tm):
    i = pl.program_id(0)
    npb = seq // tm
    pos0 = (i % npb) * tm
    has_prev = (pos0 > 0).astype(F32)
    has_next = (pos0 + tm < seq).astype(F32)
    x = x_ref[...]
    ext = jnp.concatenate([prev_ref[...] * has_prev, x, next_ref[...] * has_next], axis=0).astype(BF16)
    t = pos0 + lax.broadcasted_iota(jnp.int32, (tm, GROUP_CH), 0)
    outs = []
    for g, win in enumerate(POOL_WINS):
        left = win // 2
        right = win - 1 - left
        cnt = (jnp.minimum(t + right + 1, seq) - jnp.maximum(t - left, 0)).astype(F32)
        sl = slice(g * GROUP_CH, (g + 1) * GROUP_CH)
        wsum = _dot(band_ref[g], ext[:, sl])
        mixed = (wsum / cnt - x[:, sl]).astype(BF16)
        outs.append(_dot(mixed, wp_ref[g]))
    o_ref[...] = (jnp.concatenate(outs, axis=1) * sp_ref[...]).astype(BF16)


def _pool_bands(tm):
    r = np.arange(tm)[:, None]
    c = np.arange(tm + 2 * POOL_HALO)[None, :]
    d = c - POOL_HALO - r
    bands = [((d >= -(w // 2)) & (d <= w - 1 - w // 2)) for w in POOL_WINS]
    return jnp.asarray(np.stack(bands).astype(np.float32), dtype=BF16)


def pool_mixer(p, wp, sp, batch, tm=256):
    t, w = p.shape
    seq = t // batch
    tm = min(tm, seq)
    hb = tm // POOL_HALO
    nh = t // POOL_HALO
    return pl.pallas_call(
        functools.partial(_pool_kernel, seq=seq, tm=tm),
        grid=(t // tm,),
        in_specs=[pl.BlockSpec((tm, w), lambda i: (i, 0)),
                  pl.BlockSpec((POOL_HALO, w), lambda i: (jnp.maximum(i * hb - 1, 0), 0)),
                  pl.BlockSpec((POOL_HALO, w), lambda i: (jnp.minimum((i + 1) * hb, nh - 1), 0)),
                  _full((len(POOL_WINS), tm, tm + 2 * POOL_HALO)),
                  _full(wp.shape), _full(sp.shape)],
        out_specs=pl.BlockSpec((tm, w), lambda i: (i, 0)),
        out_shape=jax.ShapeDtypeStruct((t, w), BF16),
        compiler_params=_cparams(("parallel",)),
        name="pool_mixer",
    )(p, p, p, _pool_bands(tm), wp, sp)


def _dft_outer_kernel(f_ref, x_ref, o_ref):
    _, n1, grp, w = x_ref.shape
    out = _dot(f_ref[...], x_ref[0].reshape(n1 * grp, w).astype(BF16))
    o_ref[0] = out.reshape(o_ref.shape[1], grp, w)


def _dft_inner_kernel(a_ref, g_ref, cs_ref, o_ref, *, scale):
    n2 = g_ref.shape[1] // 2
    for k in range(g_ref.shape[0]):
        rhs = a_ref[0, :, k].reshape(2 * n2, a_ref.shape[-1]).astype(BF16)
        p = _dot(g_ref[k], rhs)
        outs = []
        for g in range(p.shape[1] // GROUP_CH):
            sl = slice(g * GROUP_CH, (g + 1) * GROUP_CH)
            lhs = jnp.concatenate([p[:n2, sl], p[n2:, sl]], axis=1).astype(BF16)
            outs.append(_dot(lhs, cs_ref[...]))
        o_ref[0, :, k, :] = jnp.concatenate(outs, axis=1) * scale


def _dft_tables(seq):
    n2 = seq if seq <= 2 * DFT_N2 else DFT_N2
    n1 = seq // n2
    k2 = jnp.arange(n2, dtype=jnp.int32)[:, None]
    t2 = jnp.arange(n2, dtype=jnp.int32)[None, :]
    k1 = jnp.arange(n1, dtype=jnp.int32)[:, None, None]
    ang_a = ((t2 * k2) % n2).astype(F32) * (2.0 * math.pi / n2)
    ang_b = ((t2 * k1) % seq).astype(F32) * (2.0 * math.pi / seq)
    ca, sa, cb, sb = jnp.cos(ang_a), jnp.sin(ang_a), jnp.cos(ang_b), jnp.sin(ang_b)
    gr = ca * cb - sa * sb
    gi = -(sa * cb + ca * sb)
    gmat = jnp.concatenate([jnp.concatenate([gr, -gi], axis=2),
                            jnp.concatenate([gi, gr], axis=2)], axis=1).astype(BF16)
    a1 = jnp.arange(n1, dtype=jnp.int32)
    ang1 = ((a1[:, None] * a1[None, :]) % n1).astype(F32) * (2.0 * math.pi / n1)
    f1 = jnp.concatenate([jnp.cos(ang1), -jnp.sin(ang1)], axis=0)
    f1 = jnp.kron(f1, jnp.eye(DFT_GROUP, dtype=F32)).astype(BF16)
    c = jnp.arange(GROUP_CH, dtype=jnp.int32)
    angc = ((c[:, None] * c[None, :]) % GROUP_CH).astype(F32) * (2.0 * math.pi / GROUP_CH)
    cs = jnp.concatenate([jnp.cos(angc), jnp.sin(angc)], axis=0).astype(BF16)
    return gmat, f1, cs


def fourier_mixer(f, batch, tables):
    gmat, f1, cs = tables
    t, w = f.shape
    seq = t // batch
    n1, n2x2, _ = gmat.shape
    n2 = n2x2 // 2
    scale = 1.0 / math.sqrt(seq * GROUP_CH)
    if n1 > 1:
        a = pl.pallas_call(
            _dft_outer_kernel,
            grid=(batch, n2 // DFT_GROUP),
            in_specs=[_full(f1.shape), pl.BlockSpec((1, n1, DFT_GROUP, w), lambda b, j: (b, 0, j, 0))],
            out_specs=pl.BlockSpec((1, 2 * n1, DFT_GROUP, w), lambda b, j: (b, 0, j, 0)),
            out_shape=jax.ShapeDtypeStruct((batch, 2 * n1, n2, w), F32),
            compiler_params=_cparams(("parallel", "parallel")),
            name="dft_outer",
        )(f1, f.reshape(batch, n1, n2, w))
        a = a.reshape(batch, 2, n1, n2, w)
    else:
        fr = f.reshape(batch, 1, 1, n2, w)
        a = jnp.concatenate([fr, jnp.zeros_like(fr)], axis=1)
    kg = min(DFT_GROUP, n1)
    out = pl.pallas_call(
        functools.partial(_dft_inner_kernel, scale=scale),
        grid=(batch, n1 // kg),
        in_specs=[pl.BlockSpec((1, 2, kg, n2, w), lambda b, k: (b, 0, k, 0, 0)),
                  pl.BlockSpec((kg, 2 * n2, 2 * n2), lambda b, k: (k, 0, 0)),
                  _full(cs.shape)],
        out_specs=pl.BlockSpec((1, n2, kg, w), lambda b, k: (b, 0, k, 0)),
        out_shape=jax.ShapeDtypeStruct((batch, n2, n1, w), F32),
        compiler_params=_cparams(("parallel", "parallel")),
        name="dft_inner",
    )(a, gmat, cs)
    return out.reshape(t, w)


def _merge_kernel(a_ref, p_ref, f_ref, g_ref, x_ref, mod_ref, woa_ref, wob_ref, woc_ref, wout_ref,
                  g2_ref, xo_ref, h_ref):
    d = x_ref.shape[1]
    gate = g_ref[...].astype(F32)
    y = (gate[:, :d] * _dot(a_ref[...], woa_ref[...])
         + gate[:, d:2 * d] * _dot(p_ref[...], wob_ref[...])
         + gate[:, 2 * d:] * _dot(f_ref[...].astype(BF16), woc_ref[...]))
    mod = mod_ref[0]
    xn = x_ref[...] + mod[0:1] * _dot(y.astype(BF16), wout_ref[...])
    xo_ref[...] = xn
    h_ref[...] = (_rms(xn, g2_ref[...]) * (1.0 + mod[2:3]) + mod[1:2]).astype(BF16)


def merge(attn, pooled, four, gate, x, modv, lw, g2, batch, tm=256):
    t, d = x.shape
    npb = (t // batch) // tm
    row = lambda w: pl.BlockSpec((tm, w), lambda i: (i, 0))
    weights = [lw["woa"], lw["wob"], lw["woc"], lw["wout"], g2]
    return pl.pallas_call(
        _merge_kernel,
        grid=(t // tm,),
        in_specs=[row(attn.shape[1]), row(pooled.shape[1]), row(four.shape[1]), row(gate.shape[1]),
                  row(d), pl.BlockSpec((1, 3, d), lambda i: (i // npb, 0, 0))]
                 + [_full(w.shape) for w in weights],
        out_specs=[row(d), row(d)],
        out_shape=[jax.ShapeDtypeStruct((t, d), F32), jax.ShapeDtypeStruct((t, d), BF16)],
        compiler_params=_cparams(("parallel",)),
        name="merge",
    )(attn, pooled, four, gate, x, modv, *weights)


def _ffn_kernel(h_ref, x_ref, gt_ref, w1_ref, w3_ref, w2_ref, o_ref):
    h = h_ref[...]
    a = _dot(h, w1_ref[...])
    b = _dot(h, w3_ref[...])
    g = (a * jax.nn.sigmoid(a) * b).astype(BF16)
    o_ref[...] = x_ref[...] + gt_ref[0] * _dot(g, w2_ref[...])


def ffn(h, x, gt, w1, w3, w2, batch, tm=256):
    t, d = x.shape
    npb = (t // batch) // tm
    return pl.pallas_call(
        _ffn_kernel,
        grid=(t // tm,),
        in_specs=[pl.BlockSpec((tm, d), lambda i: (i, 0)),
                  pl.BlockSpec((tm, d), lambda i: (i, 0)),
                  pl.BlockSpec((1, 1, d), lambda i: (i // npb, 0, 0)),
                  _full(w1.shape), _full(w3.shape), _full(w2.shape)],
        out_specs=pl.BlockSpec((tm, d), lambda i: (i, 0)),
        out_shape=jax.ShapeDtypeStruct((t, d), F32),
        compiler_params=_cparams(("parallel",)),
        name="ffn",
    )(h, x, gt, w1, w3, w2)


def _router_kernel(x_ref, mod_ref, g2_ref, wr_ref, br_ref, tri_ref, comb_ref, pos_ref, cum_ref, cnt_sc):
    @pl.when(pl.program_id(0) == 0)
    def _():
        cnt_sc[...] = jnp.zeros_like(cnt_sc)

    mod = mod_ref[0]
    h = _rms(x_ref[...], g2_ref[...]) * (1.0 + mod[1:2]) + mod[0:1]
    h_hi = h.astype(BF16)
    h_lo = (h - h_hi.astype(F32)).astype(BF16)
    w = wr_ref[...]
    w_hi = w.astype(BF16)
    w_lo = (w - w_hi.astype(F32)).astype(BF16)
    ne = w.shape[0]
    part = _dot_nt(jnp.concatenate([w_hi, w_lo], axis=0), h_hi)
    logits = part[:ne] + part[ne:] + _dot_nt(w_hi, h_lo) + br_ref[...]
    ne = logits.shape[0]
    eidx = lax.broadcasted_iota(jnp.int32, logits.shape, 0)
    m1 = jnp.max(logits, axis=0, keepdims=True)
    i1 = jnp.min(jnp.where(logits == m1, eidx, ne), axis=0, keepdims=True)
    sel1 = eidx == i1
    rest = jnp.where(sel1, -jnp.inf, logits)
    m2 = jnp.max(rest, axis=0, keepdims=True)
    i2 = jnp.min(jnp.where(rest == m2, eidx, ne), axis=0, keepdims=True)
    sel2 = eidx == i2
    e2 = jnp.exp(m2 - m1)
    w1 = 1.0 / (1.0 + e2)
    comb_ref[...] = jnp.where(sel1, w1, 0.0) + jnp.where(sel2, e2 * w1, 0.0)
    self = jnp.where(sel1 | sel2, 1.0, 0.0)
    before = cnt_sc[...][:, 0:1]
    rank = before + _dot(self.astype(BF16), tri_ref[...])
    pos_ref[...] = jnp.where(self > 0.0, rank, -1.0).astype(jnp.int32)
    after = before + jnp.sum(self, axis=1, keepdims=True)
    cnt_sc[...] = jnp.broadcast_to(after, cnt_sc.shape)
    cum_ref[...] = jnp.broadcast_to(after, cum_ref.shape).astype(jnp.int32)


def router(x, modv, g2, wr_t, br, batch, tm=256):
    t, d = x.shape
    ne = wr_t.shape[0]
    npb = (t // batch) // tm
    nt = t // tm
    tri = jnp.asarray(np.triu(np.ones((tm, tm), np.float32), 1), dtype=BF16)
    return pl.pallas_call(
        _router_kernel,
        grid=(nt,),
        in_specs=[pl.BlockSpec((tm, d), lambda i: (i, 0)),
                  pl.BlockSpec((1, 2, d), lambda i: (i // npb, 0, 0)),
                  _full(g2.shape), _full(wr_t.shape), _full(br.shape), _full(tri.shape)],
        out_specs=[pl.BlockSpec((ne, tm), lambda i: (0, i)),
                   pl.BlockSpec((ne, tm), lambda i: (0, i)),
                   pl.BlockSpec((ne, LANES), lambda i: (0, i))],
        out_shape=[jax.ShapeDtypeStruct((ne, t), F32), jax.ShapeDtypeStruct((ne, t), jnp.int32),
                   jax.ShapeDtypeStruct((ne, nt * LANES), jnp.int32)],
        scratch_shapes=[pltpu.VMEM((ne, LANES), F32)],
        compiler_params=_cparams(("arbitrary",)),
        name="router",
    )(x, modv, g2, wr_t, br, tri)


def _moe_kernel(cum_ref, h_ref, posr_ref, pc_ref, *rest, nsub, cum0, first, col):
    if first:
        w1_ref, w3_ref, w2_ref, o_ref, hc_sc, ys_sc = rest
    else:
        yin_ref, w1_ref, w3_ref, w2_ref, o_ref, hc_sc, ys_sc = rest
    c = pl.program_id(0)

    def cum(i):
        return cum_ref[cum0 + c * nsub + i]

    base = cum(0)
    total = cum(nsub) - base
    nfull = total // MOE_ROWS
    rem = total - nfull * MOE_ROWS
    nb = nfull + (rem > 0).astype(jnp.int32)
    nwin = min(MOE_WINDOW, nsub)

    def expert_block(j, nrows):
        off = base + j * MOE_ROWS
        s_lo = sum((cum(s + 1) <= off).astype(jnp.int32) for s in range(nsub))
        s_hi = sum((cum(s) < off + nrows).astype(jnp.int32) for s in range(nsub))

        def compact(win, width):
            iota = lax.broadcasted_iota(jnp.int32, (nrows, width), 0)
            sel = jnp.where(posr_ref[0, :, win] - off == iota, 1.0, 0.0).astype(BF16)
            hc_sc[:nrows] = _dot(sel, h_ref[win, :]).astype(BF16)

        @pl.when(s_hi - s_lo <= nwin)
        def _():
            start = pl.multiple_of(jnp.minimum(s_lo, nsub - nwin) * MOE_SUB, MOE_SUB)
            compact(pl.ds(start, nwin * MOE_SUB), nwin * MOE_SUB)

        if nwin < nsub:
            @pl.when(s_hi - s_lo > nwin)
            def _():
                compact(slice(None), nsub * MOE_SUB)

        hc = hc_sc[:nrows]
        a = _dot(hc, w1_ref[0])
        b = _dot(hc, w3_ref[0])
        g = (a * jax.nn.sigmoid(a) * b).astype(BF16)
        ys_sc[j, :nrows] = _dot(g, w2_ref[0]).astype(BF16)
        if nrows < MOE_ROWS:
            ys_sc[j, nrows:] = jnp.zeros((MOE_ROWS - nrows, ys_sc.shape[2]), BF16)

    def full_block(j, carry):
        expert_block(j, MOE_ROWS)
        return carry

    lax.fori_loop(0, nfull, full_block, 0)

    @pl.when(rem > MOE_ROWS // 2)
    def _():
        expert_block(nfull, MOE_ROWS)

    @pl.when((rem > 0) & (rem <= MOE_ROWS // 2))
    def _():
        expert_block(nfull, MOE_ROWS // 2)

    ys_sc[nb] = jnp.zeros(ys_sc.shape[1:], BF16)
    ys_sc[nb + 1] = jnp.zeros(ys_sc.shape[1:], BF16)
    ne = pc_ref.shape[1] // 2
    for s in range(nsub):
        sub = slice(s * MOE_SUB, (s + 1) * MOE_SUB)
        lo = cum(s) - base
        j_lo = jnp.minimum(lo // MOE_ROWS, nb)
        one_block = cum(s + 1) - base <= (lo // MOE_ROWS + 1) * MOE_ROWS

        def add_back(nblk, sub=sub, j_lo=j_lo):
            liota = lax.broadcasted_iota(jnp.int32, (MOE_SUB, nblk * MOE_ROWS), 1).astype(F32)
            pos = pc_ref[sub, col:col + 1]
            off = (base + j_lo * MOE_ROWS).astype(F32)
            sel = jnp.where(pos - off == liota, 1.0, 0.0).astype(BF16)
            rows = ys_sc[pl.ds(j_lo, nblk)].reshape(nblk * MOE_ROWS, ys_sc.shape[2])
            y = pc_ref[sub, ne + col:ne + col + 1] * _dot(sel, rows)
            if not first:
                y = yin_ref[sub, :].astype(F32) + y
            o_ref[sub, :] = y.astype(BF16)

        pl.when(one_block)(functools.partial(add_back, 1))
        pl.when(jnp.logical_not(one_block))(functools.partial(add_back, 2))


def moe(h, comb_t, pos_t, cum_incl, w1, w3, w2):
    t, d = h.shape
    ne = comb_t.shape[0]
    tc = min(MOE_CHUNK, t)
    assert t % tc == 0 and tc % MOE_SUB == 0
    nsub = tc // MOE_SUB
    nt = t // MOE_SUB
    cum = jnp.concatenate([jnp.zeros((ne, 1), jnp.int32), cum_incl[:, ::LANES]], axis=1)
    cum = cum.reshape(-1)
    posr = pos_t.reshape(ne, 1, t)
    pc = jnp.concatenate([pos_t.astype(F32).T, comb_t.T], axis=1)
    once = pl.Buffered(1)
    y = None
    for e in range(ne):
        first = y is None
        chunk = pl.BlockSpec((tc, d), lambda c, cum: (c, 0))
        in_specs = [chunk,
                    pl.BlockSpec((1, 1, tc), lambda c, cum, e=e: (e, 0, c)),
                    pl.BlockSpec((tc, 2 * ne), lambda c, cum: (c, 0))]
        in_specs += [] if first else [chunk]
        in_specs += [pl.BlockSpec((1,) + w.shape[1:], lambda c, cum, e=e: (e, 0, 0), pipeline_mode=once)
                     for w in (w1, w3, w2)]
        grid_spec = pltpu.PrefetchScalarGridSpec(
            num_scalar_prefetch=1, grid=(t // tc,), in_specs=in_specs, out_specs=chunk,
            scratch_shapes=[pltpu.VMEM((MOE_ROWS, d), BF16),
                            pltpu.VMEM((tc // MOE_ROWS + 2, MOE_ROWS, d), BF16)])
        y = pl.pallas_call(
            functools.partial(_moe_kernel, nsub=nsub, cum0=e * (nt + 1), first=first, col=e),
            grid_spec=grid_spec,
            out_shape=jax.ShapeDtypeStruct((t, d), BF16),
            input_output_aliases={} if first else {4: 0},
            compiler_params=pltpu.CompilerParams(dimension_semantics=("arbitrary",),
                                                 vmem_limit_bytes=MOE_VMEM_LIMIT),
            name="moe_expert",
        )(*([cum, h, posr, pc] + ([] if first else [y]) + [w1, w3, w2]))
    return y


def _residual_kernel(x_ref, y_ref, gt_ref, g_ref, o_ref, *, norm):
    x = x_ref[...] + gt_ref[0] * y_ref[...].astype(F32)
    o_ref[...] = _rms(x, g_ref[...]) if norm else x


def residual(x, y, gt, g, batch, norm, tm=512):
    t, d = x.shape
    tm = min(tm, t // batch)
    npb = (t // batch) // tm
    return pl.pallas_call(
        functools.partial(_residual_kernel, norm=norm),
        grid=(t // tm,),
        in_specs=[pl.BlockSpec((tm, d), lambda i: (i, 0)), pl.BlockSpec((tm, d), lambda i: (i, 0)),
                  pl.BlockSpec((1, 1, d), lambda i: (i // npb, 0, 0)), _full(g.shape)],
        out_specs=pl.BlockSpec((tm, d), lambda i: (i, 0)),
        out_shape=jax.ShapeDtypeStruct((t, d), F32),
        compiler_params=_cparams(("parallel",)),
        name="residual_norm" if norm else "residual",
    )(x, y, gt, g)


def _final_norm_kernel(x_ref, g_ref, o_ref):
    o_ref[...] = _rms(x_ref[...], g_ref[...])


def final_norm(x, g, tm=512):
    t, d = x.shape
    return pl.pallas_call(
        _final_norm_kernel,
        grid=(t // tm,),
        in_specs=[pl.BlockSpec((tm, d), lambda i: (i, 0)), _full(g.shape)],
        out_specs=pl.BlockSpec((tm, d), lambda i: (i, 0)),
        out_shape=jax.ShapeDtypeStruct((t, d), F32),
        compiler_params=_cparams(("parallel",)),
        name="final_norm",
    )(x, g)


def _rope_tables(seq, rotate):
    half = ROPE_DIM // 2
    nf = ROPE_DIM // 4
    if rotate:
        t = jnp.arange(seq, dtype=jnp.int32)
        rows = (t // GRID_WIDTH).astype(F32)
        cols = (t % GRID_WIDTH).astype(F32)
        inv = ROPE_THETA ** (-jnp.arange(0, half, 2, dtype=F32) / half)
        ang = jnp.stack([rows[:, None] * inv, cols[:, None] * inv], axis=1)
        cos, sin = jnp.cos(ang), jnp.sin(ang)
    else:
        cos = jnp.ones((seq, 2, nf), F32)
        sin = jnp.zeros((seq, 2, nf), F32)
    cpat = jnp.concatenate([cos, cos], axis=2).reshape(seq, ROPE_DIM)
    spat = jnp.concatenate([-sin, sin], axis=2).reshape(seq, ROPE_DIM)
    return cpat, spat


def _rope_inputs(seq, rotate):
    cpat, spat = _rope_tables(seq, rotate)
    tabq = jnp.concatenate([jnp.tile(cpat, (1, NUM_HEADS)), jnp.tile(spat, (1, NUM_HEADS))], axis=1)
    tabk = jnp.concatenate([cpat, spat, jnp.zeros((seq, LANES - 2 * ROPE_DIM), F32)], axis=1)
    return tabq, tabk


def _swap_perm():
    j = np.arange(ROPE_DIM)
    half = ROPE_DIM // 4
    return np.where((j % (2 * half)) < half, j + half, j - half)


def _placement(shared):
    nsrc = ROPE_DIM if shared else NUM_HEADS * ROPE_DIM
    e = np.zeros((LANES if shared else 2 * nsrc, NUM_HEADS * HEAD_PAD), np.float32)
    for hd in range(NUM_HEADS):
        for j in range(ROPE_DIM):
            col = hd * HEAD_PAD + NOPE_DIM + j
            src = j if shared else hd * ROPE_DIM + j
            e[src, col] = 1.0
            e[nsrc + src, col] = 1.0
    return jnp.asarray(e, dtype=BF16)


def _prep_layer(w_in, b_gate, g_q, w_uq, g_kv, w_ukv, w_oa, w_ob, w_oc, w_out, w_pool, s_pool):
    d = w_in.shape[0]
    q_lora, kv_lora = g_q.shape[0], g_kv.shape[0]
    perm = _swap_perm()
    off_kr = q_lora + kv_lora
    off_pool = off_kr + ROPE_DIM
    pool_w = w_pool.shape[0] * w_pool.shape[1]
    off_fft = off_pool + pool_w
    off_gate = w_in.shape[1] - NUM_BRANCH * d
    w_in = w_in.astype(BF16)
    kr = w_in[:, off_kr:off_pool]
    wa = jnp.concatenate([w_in[:, :off_kr], kr, kr[:, perm],
                          jnp.zeros((d, LANES - 2 * ROPE_DIM), BF16)], axis=1)
    qscale = (NOPE_DIM + ROPE_DIM) ** -0.5 * math.log2(math.e)
    wq = w_uq * qscale
    wqn = jnp.pad(wq[:, :, :NOPE_DIM], ((0, 0), (0, 0), (0, HEAD_PAD - NOPE_DIM)))
    wqr = wq[:, :, NOPE_DIM:]
    wqr = jnp.concatenate([wqr.reshape(q_lora, -1), wqr[:, :, perm].reshape(q_lora, -1)], axis=1)
    wkn = jnp.pad(w_ukv[:, :, :NOPE_DIM], ((0, 0), (0, 0), (0, HEAD_PAD - NOPE_DIM)))
    wv = jnp.pad(w_ukv[:, :, NOPE_DIM:], ((0, 0), (0, 0), (0, VT_ROWS - VAL_DIM)))
    vone = np.zeros((NUM_HEADS, VT_ROWS, 1), np.float32)
    vone[:, VAL_DIM] = 1.0
    return dict(
        wa=wa, wpool=w_in[:, off_pool:off_fft],
        wfft=w_in[:, off_fft:off_gate], wgate=w_in[:, off_gate:],
        bgate=b_gate.reshape(1, -1),
        gq=g_q.reshape(1, -1), wqn=wqn.reshape(q_lora, -1).astype(BF16), wqr=wqr.astype(BF16),
        eq=_placement(False), gkv=g_kv.reshape(1, -1),
        wkn=wkn.reshape(kv_lora, -1).astype(BF16), ek=_placement(True),
        wvt=wv.reshape(kv_lora, -1).T.astype(BF16), vone=jnp.asarray(vone.reshape(-1, 1)),
        woa=w_oa.astype(BF16), wob=w_ob.astype(BF16), woc=w_oc.astype(BF16), wout=w_out.astype(BF16),
        wp=w_pool.astype(BF16), sp=s_pool.reshape(1, -1),
    )


def _token_mixer(xs, modv1, modv2, lw, g1, g2, ropes, dft, batch, kv_ctx, tm):
    tabq, tabk = ropes
    q, k, vt, pool_in, fft_in, gate = proj_in(xs, modv1, g1, tabq, tabk, lw, batch, False, tm)
    if kv_ctx is None:
        attn = attention(q, k, vt, None, None, batch)
    else:
        attn = attention(q, kv_ctx[0], kv_ctx[1], k, vt, batch)
    pooled = pool_mixer(pool_in, lw["wp"], lw["sp"], batch)
    four = fourier_mixer(fft_in, batch, dft)
    x_new, h2 = merge(attn, pooled, four, gate, xs, modv2, lw, g2, batch, tm)
    return x_new, h2, (k, vt)


def kernel(x, c, ctx, c_ctx, w_mod, b_mod, g_norm1, w_in, b_gate, g_q, w_uq, g_kv, w_ukv, w_pool, s_pool,
           w_oa, w_ob, w_oc, w_out, g_norm2, ffn_w1, ffn_w3, ffn_w2, moe_router, moe_bias,
           moe_w1, moe_w3, moe_w2, g_final):
    batch, seq, d = x.shape
    n_ctx = ctx.shape[1]
    depth = w_mod.shape[0]
    n_exp = moe_router.shape[-1] if moe_router.shape[0] else 0
    xl = x.reshape(batch * seq, d)
    xc = ctx.reshape(batch * n_ctx, d)

    cvec = jnp.concatenate([c, c_ctx[None], jnp.zeros((8 - batch - 1, d), F32)], axis=0)
    mods = modulation(cvec, w_mod, b_mod)

    rope_l = _rope_inputs(seq, True)
    rope_c = _rope_inputs(n_ctx, False)
    dft_l = _dft_tables(seq)
    dft_c = _dft_tables(n_ctx)

    for layer in range(depth):
        last = layer == depth - 1
        lw = _prep_layer(w_in[layer], b_gate[layer], g_q[layer], w_uq[layer], g_kv[layer], w_ukv[layer],
                         w_oa[layer], w_ob[layer], w_oc[layer], w_out[layer], w_pool[layer], s_pool[layer])
        g1 = g_norm1[layer].reshape(1, d)
        g2 = g_norm2[layer].reshape(1, d)
        ml = mods[layer, :batch].reshape(batch, 6, d)
        mc = jnp.broadcast_to(mods[layer, batch].reshape(1, 6, d), (batch, 6, d))

        if last:
            k_c, vt_c = proj_in(xc, mc[:, 0:2], g1, rope_c[0], rope_c[1], lw, batch, True)
            kv_c = (k_c, vt_c)
        else:
            xc_new, h2_c, kv_c = _token_mixer(xc, mc[:, 0:2], mc[:, 2:5], lw, g1, g2, rope_c, dft_c,
                                              batch, None, 256)
        xl, h2_l, _ = _token_mixer(xl, ml[:, 0:2], ml[:, 2:5], lw, g1, g2, rope_l, dft_l, batch, kv_c, 512)
        if not last:
            xc = xc_new

        streams = [(xl, h2_l, ml)]
        if not last:
            streams.append((xc, h2_c, mc))
        fuse_final = last and layer % 2 == 1
        new = []
        for xs, h2, mm in streams:
            gt2 = mm[:, 5:6]
            li = layer // 2
            if layer % 2 == 0:
                xs = ffn(h2, xs, gt2, ffn_w1[li].astype(BF16), ffn_w3[li].astype(BF16),
                         ffn_w2[li].astype(BF16), batch)
            else:
                comb_t, pos_t, cum_incl = router(xs, mm[:, 3:5], g2, moe_router[li].T,
                                                 moe_bias[li].reshape(n_exp, 1), batch)
                y = moe(h2, comb_t, pos_t, cum_incl, moe_w1[li].astype(BF16), moe_w3[li].astype(BF16),
                        moe_w2[li].astype(BF16))
                xs = residual(xs, y, gt2, g_final.reshape(1, d), batch, fuse_final)
            new.append(xs)
        xl = new[0]
        if not last:
            xc = new[1]

    if not fuse_final:
        xl = final_norm(xl, g_final.reshape(1, d))
    return xl.reshape(batch, seq, d)
```

```python
import functools
import math

import numpy as np
import jax
import jax.numpy as jnp
from jax import lax
from jax.experimental import pallas as pl
from jax.experimental.pallas import tpu as pltpu

GRID_WIDTH = 64
NUM_HEADS = 8
NOPE_DIM = 64
ROPE_DIM = 32
VAL_DIM = 64
ROPE_THETA = 10000.0
POOL_WINS = (2, 4, 8, 16)
GROUP_CH = 128
NUM_BRANCH = 3
TOPK = 2
NORM_EPS = 1e-6

LANES = 128
HEAD_PAD = 128
VT_ROWS = 80
POOL_HALO = 16
DFT_N2 = 128
DFT_GROUP = 8
VMEM_LIMIT = 56 * 1024 * 1024
MOE_ROWS = 256
MOE_SUB = 256
MOE_CHUNK = 2048
MOE_WINDOW = 6
MOE_VMEM_LIMIT = 60 * 1024 * 1024
STALE_MAX_JUMP = 64.0

BF16 = jnp.bfloat16
F32 = jnp.float32


def _cparams(sem):
    return pltpu.CompilerParams(dimension_semantics=sem, vmem_limit_bytes=VMEM_LIMIT)


def _full(shape):
    n = len(shape)
    return pl.BlockSpec(shape, lambda *_: (0,) * n)


def _dot(a, b):
    return jnp.dot(a, b, preferred_element_type=F32)


def _dot_nt(a, b):
    return lax.dot_general(a, b, (((1,), (1,)), ((), ())), preferred_element_type=F32)


def _rms(x, g):
    return x * lax.rsqrt(jnp.mean(x * x, axis=-1, keepdims=True) + NORM_EPS) * g


def _mod_kernel(c_ref, w_ref, b_ref, o_ref):
    c = c_ref[...]
    s = c * jax.nn.sigmoid(c)
    w = w_ref[0]
    s_hi = s.astype(BF16)
    s_lo = (s - s_hi.astype(F32)).astype(BF16)
    w_hi = w.astype(BF16)
    w_lo = (w - w_hi.astype(F32)).astype(BF16)
    part = _dot(jnp.concatenate([s_hi, s_lo], axis=0), w_hi)
    rows = s.shape[0]
    o_ref[0] = part[:rows] + part[rows:] + _dot(s_hi, w_lo) + b_ref[0]


def modulation(cvec, w_mod, b_mod):
    depth, d, n = w_mod.shape
    tn = 1536
    return pl.pallas_call(
        _mod_kernel,
        grid=(depth, n // tn),
        in_specs=[pl.BlockSpec((8, d), lambda l, j: (0, 0)),
                  pl.BlockSpec((1, d, tn), lambda l, j: (l, 0, j)),
                  pl.BlockSpec((1, 1, tn), lambda l, j: (l, 0, j))],
        out_specs=pl.BlockSpec((1, 8, tn), lambda l, j: (l, 0, j)),
        out_shape=jax.ShapeDtypeStruct((depth, 8, n), F32),
        compiler_params=_cparams(("parallel", "parallel")),
        name="modulation",
    )(cvec, w_mod, b_mod.reshape(depth, 1, n))


def _proj_in_kernel(x_ref, mod_ref, g1_ref, tabq_ref, tabk_ref,
                    wa_ref, wpool_ref, wfft_ref, wgate_ref, bgate_ref,
                    gq_ref, wqn_ref, wqr_ref, eq_ref, gkv_ref, wkn_ref, ek_ref, wvt_ref, vone_ref,
                    *out_refs, q_lora, kv_lora, kv_only):
    if kv_only:
        k_ref, vt_ref = out_refs
    else:
        q_ref, k_ref, vt_ref, pool_ref, fft_ref, gate_ref = out_refs
    x = x_ref[...]
    mod = mod_ref[0]
    h = (_rms(x, g1_ref[...]) * (1.0 + mod[1:2]) + mod[0:1]).astype(BF16)
    pa = _dot(h, wa_ref[...])
    kvn = _rms(pa[:, q_lora:q_lora + kv_lora], gkv_ref[...]).astype(BF16)
    kr = (pa[:, q_lora + kv_lora:] * tabk_ref[...]).astype(BF16)
    k_all = (_dot(kvn, wkn_ref[...]) + _dot(kr, ek_ref[...])).astype(BF16)
    for hd in range(NUM_HEADS):
        k_ref[hd] = k_all[:, hd * HEAD_PAD:(hd + 1) * HEAD_PAD]
    vt_ref[0] = (_dot_nt(wvt_ref[...], kvn) + vone_ref[...]).astype(BF16)
    if kv_only:
        return
    qn = _rms(pa[:, :q_lora], gq_ref[...]).astype(BF16)
    qr = (_dot(qn, wqr_ref[...]) * tabq_ref[...]).astype(BF16)
    q_all = (_dot(qn, wqn_ref[...]) + _dot(qr, eq_ref[...])).astype(BF16)
    for hd in range(NUM_HEADS):
        q_ref[hd] = q_all[:, hd * HEAD_PAD:(hd + 1) * HEAD_PAD]
    pool_ref[...] = _dot(h, wpool_ref[...])
    fft_ref[...] = _dot(h, wfft_ref[...])
    gate_ref[...] = jax.nn.sigmoid(_dot(h, wgate_ref[...]) + bgate_ref[...]).astype(BF16)


def proj_in(x, modv, g1, tabq, tabk, lw, batch, kv_only, tm=256):
    t, d = x.shape
    seq = t // batch
    npb = seq // tm
    q_lora = lw["gq"].shape[1]
    kv_lora = lw["gkv"].shape[1]
    hw = NUM_HEADS * HEAD_PAD
    weights = [lw["wa"], lw["wpool"], lw["wfft"], lw["wgate"], lw["bgate"],
               lw["gq"], lw["wqn"], lw["wqr"], lw["eq"], lw["gkv"], lw["wkn"], lw["ek"],
               lw["wvt"], lw["vone"]]
    in_specs = [pl.BlockSpec((tm, d), lambda i: (i, 0)),
                pl.BlockSpec((1, 2, d), lambda i: (i // npb, 0, 0)),
                _full(g1.shape),
                pl.BlockSpec((tm, tabq.shape[1]), lambda i: (i % npb, 0)),
                pl.BlockSpec((tm, tabk.shape[1]), lambda i: (i % npb, 0))]
    in_specs += [_full(w.shape) for w in weights]
    head_spec = pl.BlockSpec((NUM_HEADS, tm, HEAD_PAD), lambda i: (0, i, 0))
    head_shape = jax.ShapeDtypeStruct((NUM_HEADS, t, HEAD_PAD), BF16)
    vt_spec = pl.BlockSpec((1, NUM_HEADS * VT_ROWS, tm), lambda i: (i // npb, 0, i % npb))
    vt_shape = jax.ShapeDtypeStruct((batch, NUM_HEADS * VT_ROWS, seq), BF16)
    if kv_only:
        out_specs = [head_spec, vt_spec]
        out_shape = [head_shape, vt_shape]
    else:
        pw, fw, gw = lw["wpool"].shape[1], lw["wfft"].shape[1], lw["wgate"].shape[1]
        out_specs = [head_spec, head_spec, vt_spec,
                     pl.BlockSpec((tm, pw), lambda i: (i, 0)),
                     pl.BlockSpec((tm, fw), lambda i: (i, 0)),
                     pl.BlockSpec((tm, gw), lambda i: (i, 0))]
        out_shape = [head_shape, head_shape, vt_shape,
                     jax.ShapeDtypeStruct((t, pw), F32),
                     jax.ShapeDtypeStruct((t, fw), F32),
                     jax.ShapeDtypeStruct((t, gw), BF16)]
    return pl.pallas_call(
        functools.partial(_proj_in_kernel, q_lora=q_lora, kv_lora=kv_lora, kv_only=kv_only),
        grid=(t // tm,),
        in_specs=in_specs, out_specs=out_specs, out_shape=out_shape,
        compiler_params=_cparams(("parallel",)),
        name="proj_in_kv" if kv_only else "proj_in",
    )(x, modv, g1, tabq, tabk, *weights)


def _attn_kernel(q_ref, kc_ref, vtc_ref, *rest, tk, tk_exact, n_lat_tiles):
    if n_lat_tiles:
        kl_ref, vtl_ref, o_ref = rest
    else:
        (o_ref,) = rest
    nh = q_ref.shape[0]

    def rows(hd):
        return slice(hd * VT_ROWS, (hd + 1) * VT_ROWS)

    def scores(ks):
        return tuple(_dot_nt(ks[hd], q_ref[hd]) for hd in range(nh))

    def update(ss, vts, state):
        out = []
        for hd in range(nh):
            smax = jnp.max(ss[hd], axis=0, keepdims=True)
            if state is None:
                m_new = smax
                acc = _dot(vts[hd], jnp.exp2(ss[hd] - m_new).astype(BF16))
            else:
                m_old, acc_old = state[hd]
                m_new = jnp.maximum(m_old, smax)
                pv = _dot(vts[hd], jnp.exp2(ss[hd] - m_new).astype(BF16))
                acc = acc_old * jnp.exp2(m_old - m_new) + pv
            out.append((m_new, acc))
        return tuple(out)

    def lat_k(off, n):
        return [kl_ref[hd, pl.ds(off, n), :] for hd in range(nh)]

    def lat_vt(off, n):
        return [vtl_ref[0, rows(hd), pl.ds(off, n)] for hd in range(nh)]

    def update_stale(ss, vts, state, jump):
        out = []
        for hd in range(nh):
            m_old, acc_old = state[hd]
            pv = _dot(vts[hd], jnp.exp2(ss[hd] - m_old).astype(BF16))
            smax = jnp.max(ss[hd], axis=0, keepdims=True)
            m_new = jnp.maximum(m_old, smax)
            out.append((m_new, (acc_old + pv) * jnp.exp2(m_old - m_new)))
            jump = jnp.maximum(jump, smax - m_old)
        return tuple(out), jump

    state = update(scores([kc_ref[hd] for hd in range(nh)]),
                   [vtc_ref[0, rows(hd), :] for hd in range(nh)], None)
    if n_lat_tiles:
        def fast_body(i, carry):
            off = pl.multiple_of(i * tk, tk)
            return update_stale(scores(lat_k(off, tk)), lat_vt(off, tk), *carry)
        fast, jump = lax.fori_loop(0, n_lat_tiles, fast_body,
                                   (state, jnp.zeros((1, q_ref.shape[1]), F32)))

        def exact_loop():
            def body(i, st):
                off = pl.multiple_of(i * tk_exact, tk_exact)
                return update(scores(lat_k(off, tk_exact)), lat_vt(off, tk_exact), st)
            return lax.fori_loop(0, n_lat_tiles * (tk // tk_exact), body, state)

        state = lax.cond(jnp.max(jump) > STALE_MAX_JUMP, exact_loop, lambda: fast)
    for pair in range(nh // 2):
        outs = []
        for hd in (2 * pair, 2 * pair + 1):
            acc = state[hd][1]
            outs.append(acc[:VAL_DIM] / acc[VAL_DIM:VAL_DIM + 1])
        o_ref[:, pair * 2 * VAL_DIM:(pair + 1) * 2 * VAL_DIM] = (
            jnp.concatenate(outs, axis=0).T.astype(BF16))


def attention(q, kc, vtc, kl, vtl, batch, tq=256, tk=2048):
    h, t, _ = q.shape
    lq = t // batch
    nq = lq // tq
    lc = kc.shape[1] // batch
    once = pl.Buffered(1)
    in_specs = [pl.BlockSpec((h, tq, HEAD_PAD), lambda b, i: (0, b * nq + i, 0)),
                pl.BlockSpec((h, lc, HEAD_PAD), lambda b, i: (0, b, 0), pipeline_mode=once),
                pl.BlockSpec((1, h * VT_ROWS, lc), lambda b, i: (b, 0, 0), pipeline_mode=once)]
    args = [q, kc, vtc]
    n_lat_tiles = 0
    if kl is not None:
        ll = kl.shape[1] // batch
        tk = min(tk, ll)
        n_lat_tiles = ll // tk
        in_specs += [pl.BlockSpec((h, ll, HEAD_PAD), lambda b, i: (0, b, 0), pipeline_mode=once),
                     pl.BlockSpec((1, h * VT_ROWS, ll), lambda b, i: (b, 0, 0), pipeline_mode=once)]
        args += [kl, vtl]
    return pl.pallas_call(
        functools.partial(_attn_kernel, tk=tk, tk_exact=min(tk, 512), n_lat_tiles=n_lat_tiles),
        grid=(batch, nq),
        in_specs=in_specs,
        out_specs=pl.BlockSpec((tq, h * VAL_DIM), lambda b, i: (b * nq + i, 0)),
        out_shape=jax.ShapeDtypeStruct((t, h * VAL_DIM), BF16),
        compiler_params=_cparams(("parallel", "arbitrary")),
        name="attention",
    )(*args)


def _pool_kernel(x_ref, prev_ref, next_ref, band_ref, wp_ref, sp_ref, o_ref, *, seq, tm):
    i = pl.program_id(0)
    npb = seq // tm
    pos0 = (i % npb) * tm
    has_prev = (pos0 > 0).astype(F32)
    has_next = (pos0 + tm < seq).astype(F32)
    x = x_ref[...]
    ext = jnp.concatenate([prev_ref[...] * has_prev, x, next_ref[...] * has_next], axis=0).astype(BF16)
    t = pos0 + lax.broadcasted_iota(jnp.int32, (tm, GROUP_CH), 0)
    outs = []
    for g, win in enumerate(POOL_WINS):
        left = win // 2
        right = win - 1 - left
        cnt = (jnp.minimum(t + right + 1, seq) - jnp.maximum(t - left, 0)).astype(F32)
        sl = slice(g * GROUP_CH, (g + 1) * GROUP_CH)
        wsum = _dot(band_ref[g], ext[:, sl])
        mixed = (wsum / cnt - x[:, sl]).astype(BF16)
        outs.append(_dot(mixed, wp_ref[g]))
    o_ref[...] = (jnp.concatenate(outs, axis=1) * sp_ref[...]).astype(BF16)


def _pool_bands(tm):
    r = np.arange(tm)[:, None]
    c = np.arange(tm + 2 * POOL_HALO)[None, :]
    d = c - POOL_HALO - r
    bands = [((d >= -(w // 2)) & (d <= w - 1 - w // 2)) for w in POOL_WINS]
    return jnp.asarray(np.stack(bands).astype(np.float32), dtype=BF16)


def pool_mixer(p, wp, sp, batch, tm=256):
    t, w = p.shape
    seq = t // batch
    tm = min(tm, seq)
    hb = tm // POOL_HALO
    nh = t // POOL_HALO
    return pl.pallas_call(
        functools.partial(_pool_kernel, seq=seq, tm=tm),
        grid=(t // tm,),
        in_specs=[pl.BlockSpec((tm, w), lambda i: (i, 0)),
                  pl.BlockSpec((POOL_HALO, w), lambda i: (jnp.maximum(i * hb - 1, 0), 0)),
                  pl.BlockSpec((POOL_HALO, w), lambda i: (jnp.minimum((i + 1) * hb, nh - 1), 0)),
                  _full((len(POOL_WINS), tm, tm + 2 * POOL_HALO)),
                  _full(wp.shape), _full(sp.shape)],
        out_specs=pl.BlockSpec((tm, w), lambda i: (i, 0)),
        out_shape=jax.ShapeDtypeStruct((t, w), BF16),
        compiler_params=_cparams(("parallel",)),
        name="pool_mixer",
    )(p, p, p, _pool_bands(tm), wp, sp)


def _dft_outer_kernel(f_ref, x_ref, o_ref):
    _, n1, grp, w = x_ref.shape
    out = _dot(f_ref[...], x_ref[0].reshape(n1 * grp, w).astype(BF16))
    o_ref[0] = out.reshape(o_ref.shape[1], grp, w)


def _dft_inner_kernel(a_ref, g_ref, cs_ref, o_ref, *, scale):
    n2 = g_ref.shape[1] // 2
    for k in range(g_ref.shape[0]):
        rhs = a_ref[0, :, k].reshape(2 * n2, a_ref.shape[-1]).astype(BF16)
        p = _dot(g_ref[k], rhs)
        outs = []
        for g in range(p.shape[1] // GROUP_CH):
            sl = slice(g * GROUP_CH, (g + 1) * GROUP_CH)
            lhs = jnp.concatenate([p[:n2, sl], p[n2:, sl]], axis=1).astype(BF16)
            outs.append(_dot(lhs, cs_ref[...]))
        o_ref[0, :, k, :] = jnp.concatenate(outs, axis=1) * scale


def _dft_tables(seq):
    n2 = seq if seq <= 2 * DFT_N2 else DFT_N2
    n1 = seq // n2
    k2 = jnp.arange(n2, dtype=jnp.int32)[:, None]
    t2 = jnp.arange(n2, dtype=jnp.int32)[None, :]
    k1 = jnp.arange(n1, dtype=jnp.int32)[:, None, None]
    ang_a = ((t2 * k2) % n2).astype(F32) * (2.0 * math.pi / n2)
    ang_b = ((t2 * k1) % seq).astype(F32) * (2.0 * math.pi / seq)
    ca, sa, cb, sb = jnp.cos(ang_a), jnp.sin(ang_a), jnp.cos(ang_b), jnp.sin(ang_b)
    gr = ca * cb - sa * sb
    gi = -(sa * cb + ca * sb)
    gmat = jnp.concatenate([jnp.concatenate([gr, -gi], axis=2),
                            jnp.concatenate([gi, gr], axis=2)], axis=1).astype(BF16)
    r = jnp.arange(2 * n1 * DFT_GROUP, dtype=jnp.int32)[:, None]
    q = jnp.arange(n1 * DFT_GROUP, dtype=jnp.int32)[None, :]
    ang1 = ((((r // DFT_GROUP) % n1) * (q // DFT_GROUP)) % n1).astype(F32) * (2.0 * math.pi / n1)
    f1 = jnp.where(r // DFT_GROUP >= n1, -jnp.sin(ang1), jnp.cos(ang1))
    f1 = jnp.where(r % DFT_GROUP == q % DFT_GROUP, f1, 0.0).astype(BF16)
    c = jnp.arange(GROUP_CH, dtype=jnp.int32)
    angc = ((c[:, None] * c[None, :]) % GROUP_CH).astype(F32) * (2.0 * math.pi / GROUP_CH)
    cs = jnp.concatenate([jnp.cos(angc), jnp.sin(angc)], axis=0).astype(BF16)
    return gmat, f1, cs


def fourier_mixer(f, batch, tables):
    gmat, f1, cs = tables
    t, w = f.shape
    seq = t // batch
    n1, n2x2, _ = gmat.shape
    n2 = n2x2 // 2
    scale = 1.0 / math.sqrt(seq * GROUP_CH)
    if n1 > 1:
        a = pl.pallas_call(
            _dft_outer_kernel,
            grid=(batch, n2 // DFT_GROUP),
            in_specs=[_full(f1.shape), pl.BlockSpec((1, n1, DFT_GROUP, w), lambda b, j: (b, 0, j, 0))],
            out_specs=pl.BlockSpec((1, 2 * n1, DFT_GROUP, w), lambda b, j: (b, 0, j, 0)),
            out_shape=jax.ShapeDtypeStruct((batch, 2 * n1, n2, w), F32),
            compiler_params=_cparams(("parallel", "parallel")),
            name="dft_outer",
        )(f1, f.reshape(batch, n1, n2, w))
        a = a.reshape(batch, 2, n1, n2, w)
    else:
        fr = f.reshape(batch, 1, 1, n2, w)
        a = jnp.concatenate([fr, jnp.zeros_like(fr)], axis=1)
    kg = min(DFT_GROUP, n1)
    out = pl.pallas_call(
        functools.partial(_dft_inner_kernel, scale=scale),
        grid=(batch, n1 // kg),
        in_specs=[pl.BlockSpec((1, 2, kg, n2, w), lambda b, k: (b, 0, k, 0, 0)),
                  pl.BlockSpec((kg, 2 * n2, 2 * n2), lambda b, k: (k, 0, 0)),
                  _full(cs.shape)],
        out_specs=pl.BlockSpec((1, n2, kg, w), lambda b, k: (b, 0, k, 0)),
        out_shape=jax.ShapeDtypeStruct((batch, n2, n1, w), F32),
        compiler_params=_cparams(("parallel", "parallel")),
        name="dft_inner",
    )(a, gmat, cs)
    return out.reshape(t, w)


def _merge_kernel(a_ref, p_ref, f_ref, g_ref, x_ref, mod_ref, woa_ref, wob_ref, woc_ref, wout_ref,
                  g2_ref, xo_ref, h_ref):
    d = x_ref.shape[1]
    gate = g_ref[...].astype(F32)
    y = (gate[:, :d] * _dot(a_ref[...], woa_ref[...])
         + gate[:, d:2 * d] * _dot(p_ref[...], wob_ref[...])
         + gate[:, 2 * d:] * _dot(f_ref[...].astype(BF16), woc_ref[...]))
    mod = mod_ref[0]
    xn = x_ref[...] + mod[0:1] * _dot(y.astype(BF16), wout_ref[...])
    xo_ref[...] = xn
    h_ref[...] = (_rms(xn, g2_ref[...]) * (1.0 + mod[2:3]) + mod[1:2]).astype(BF16)


def merge(attn, pooled, four, gate, x, modv, lw, g2, batch, tm=256):
    t, d = x.shape
    npb = (t // batch) // tm
    row = lambda w: pl.BlockSpec((tm, w), lambda i: (i, 0))
    weights = [lw["woa"], lw["wob"], lw["woc"], lw["wout"], g2]
    return pl.pallas_call(
        _merge_kernel,
        grid=(t // tm,),
        in_specs=[row(attn.shape[1]), row(pooled.shape[1]), row(four.shape[1]), row(gate.shape[1]),
                  row(d), pl.BlockSpec((1, 3, d), lambda i: (i // npb, 0, 0))]
                 + [_full(w.shape) for w in weights],
        out_specs=[row(d), row(d)],
        out_shape=[jax.ShapeDtypeStruct((t, d), F32), jax.ShapeDtypeStruct((t, d), BF16)],
        compiler_params=_cparams(("parallel",)),
        name="merge",
    )(attn, pooled, four, gate, x, modv, *weights)


def _ffn_kernel(h_ref, x_ref, gt_ref, w1_ref, w3_ref, w2_ref, o_ref):
    h = h_ref[...]
    a = _dot(h, w1_ref[...])
    b = _dot(h, w3_ref[...])
    g = (a * jax.nn.sigmoid(a) * b).astype(BF16)
    o_ref[...] = x_ref[...] + gt_ref[0] * _dot(g, w2_ref[...])


def ffn(h, x, gt, w1, w3, w2, batch, tm=256):
    t, d = x.shape
    npb = (t // batch) // tm
    return pl.pallas_call(
        _ffn_kernel,
        grid=(t // tm,),
        in_specs=[pl.BlockSpec((tm, d), lambda i: (i, 0)),
                  pl.BlockSpec((tm, d), lambda i: (i, 0)),
                  pl.BlockSpec((1, 1, d), lambda i: (i // npb, 0, 0)),
                  _full(w1.shape), _full(w3.shape), _full(w2.shape)],
        out_specs=pl.BlockSpec((tm, d), lambda i: (i, 0)),
        out_shape=jax.ShapeDtypeStruct((t, d), F32),
        compiler_params=_cparams(("parallel",)),
        name="ffn",
    )(h, x, gt, w1, w3, w2)


def _router_kernel(x_ref, mod_ref, g2_ref, wr_ref, br_ref, tri_ref, comb_ref, pos_ref, cum_ref, cnt_sc):
    @pl.when(pl.program_id(0) == 0)
    def _():
        cnt_sc[...] = jnp.zeros_like(cnt_sc)

    mod = mod_ref[0]
    h = _rms(x_ref[...], g2_ref[...]) * (1.0 + mod[1:2]) + mod[0:1]
    h_hi = h.astype(BF16)
    h_lo = (h - h_hi.astype(F32)).astype(BF16)
    w = wr_ref[...]
    w_hi = w.astype(BF16)
    w_lo = (w - w_hi.astype(F32)).astype(BF16)
    ne = w.shape[0]
    part = _dot_nt(jnp.concatenate([w_hi, w_lo], axis=0), h_hi)
    logits = part[:ne] + part[ne:] + _dot_nt(w_hi, h_lo) + br_ref[...]
    ne = logits.shape[0]
    eidx = lax.broadcasted_iota(jnp.int32, logits.shape, 0)
    m1 = jnp.max(logits, axis=0, keepdims=True)
    i1 = jnp.min(jnp.where(logits == m1, eidx, ne), axis=0, keepdims=True)
    sel1 = eidx == i1
    rest = jnp.where(sel1, -jnp.inf, logits)
    m2 = jnp.max(rest, axis=0, keepdims=True)
    i2 = jnp.min(jnp.where(rest == m2, eidx, ne), axis=0, keepdims=True)
    sel2 = eidx == i2
    e2 = jnp.exp(m2 - m1)
    w1 = 1.0 / (1.0 + e2)
    comb_ref[...] = jnp.where(sel1, w1, 0.0) + jnp.where(sel2, e2 * w1, 0.0)
    self = jnp.where(sel1 | sel2, 1.0, 0.0)
    before = cnt_sc[...][:, 0:1]
    rank = before + _dot(self.astype(BF16), tri_ref[...])
    pos_ref[...] = jnp.where(self > 0.0, rank, -1.0).astype(jnp.int32)
    after = before + jnp.sum(self, axis=1, keepdims=True)
    cnt_sc[...] = jnp.broadcast_to(after, cnt_sc.shape)
    cum_ref[...] = jnp.broadcast_to(after, cum_ref.shape).astype(jnp.int32)


def router(x, modv, g2, wr_t, br, batch, tm=256):
    t, d = x.shape
    ne = wr_t.shape[0]
    npb = (t // batch) // tm
    nt = t // tm
    tri = jnp.asarray(np.triu(np.ones((tm, tm), np.float32), 1), dtype=BF16)
    return pl.pallas_call(
        _router_kernel,
        grid=(nt,),
        in_specs=[pl.BlockSpec((tm, d), lambda i: (i, 0)),
                  pl.BlockSpec((1, 2, d), lambda i: (i // npb, 0, 0)),
                  _full(g2.shape), _full(wr_t.shape), _full(br.shape), _full(tri.shape)],
        out_specs=[pl.BlockSpec((ne, tm), lambda i: (0, i)),
                   pl.BlockSpec((ne, tm), lambda i: (0, i)),
                   pl.BlockSpec((ne, LANES), lambda i: (0, i))],
        out_shape=[jax.ShapeDtypeStruct((ne, t), F32), jax.ShapeDtypeStruct((ne, t), jnp.int32),
                   jax.ShapeDtypeStruct((ne, nt * LANES), jnp.int32)],
        scratch_shapes=[pltpu.VMEM((ne, LANES), F32)],
        compiler_params=_cparams(("arbitrary",)),
        name="router",
    )(x, modv, g2, wr_t, br, tri)


def _moe_kernel(cum_ref, h_ref, posr_ref, pc_ref, *rest, nsub, cum0, first, col):
    if first:
        w1_ref, w3_ref, w2_ref, o_ref, hc_sc, ys_sc = rest
    else:
        yin_ref, w1_ref, w3_ref, w2_ref, o_ref, hc_sc, ys_sc = rest
    c = pl.program_id(0)

    def cum(i):
        return cum_ref[cum0 + c * nsub + i]

    base = cum(0)
    total = cum(nsub) - base
    nfull = total // MOE_ROWS
    rem = total - nfull * MOE_ROWS
    nb = nfull + (rem > 0).astype(jnp.int32)
    nwin = min(MOE_WINDOW, nsub)

    def expert_block(j, nrows):
        off = base + j * MOE_ROWS
        s_lo = sum((cum(s + 1) <= off).astype(jnp.int32) for s in range(nsub))
        s_hi = sum((cum(s) < off + nrows).astype(jnp.int32) for s in range(nsub))

        def compact(win, width):
            iota = lax.broadcasted_iota(jnp.int32, (nrows, width), 0)
            sel = jnp.where(posr_ref[0, :, win] - off == iota, 1.0, 0.0).astype(BF16)
            hc_sc[:nrows] = _dot(sel, h_ref[win, :]).astype(BF16)

        @pl.when(s_hi - s_lo <= nwin)
        def _():
            start = pl.multiple_of(jnp.minimum(s_lo, nsub - nwin) * MOE_SUB, MOE_SUB)
            compact(pl.ds(start, nwin * MOE_SUB), nwin * MOE_SUB)

        if nwin < nsub:
            @pl.when(s_hi - s_lo > nwin)
            def _():
                compact(slice(None), nsub * MOE_SUB)

        hc = hc_sc[:nrows]
        a = _dot(hc, w1_ref[0])
        b = _dot(hc, w3_ref[0])
        g = (a * jax.nn.sigmoid(a) * b).astype(BF16)
        ys_sc[j, :nrows] = _dot(g, w2_ref[0]).astype(BF16)
        if nrows < MOE_ROWS:
            ys_sc[j, nrows:] = jnp.zeros((MOE_ROWS - nrows, ys_sc.shape[2]), BF16)

    def full_block(j, carry):
        expert_block(j, MOE_ROWS)
        return carry

    lax.fori_loop(0, nfull, full_block, 0)

    @pl.when(rem > MOE_ROWS // 2)
    def _():
        expert_block(nfull, MOE_ROWS)

    @pl.when((rem > 0) & (rem <= MOE_ROWS // 2))
    def _():
        expert_block(nfull, MOE_ROWS // 2)

    ys_sc[nb] = jnp.zeros(ys_sc.shape[1:], BF16)
    ys_sc[nb + 1] = jnp.zeros(ys_sc.shape[1:], BF16)
    ne = pc_ref.shape[1] // 2
    liota = lax.broadcasted_iota(jnp.int32, (MOE_SUB, 2 * MOE_ROWS), 1).astype(F32)
    for s in range(nsub):
        sub = slice(s * MOE_SUB, (s + 1) * MOE_SUB)
        pos = pc_ref[sub, col:col + 1]
        j_lo = jnp.minimum((cum(s) - base) // MOE_ROWS, nb)
        off = (base + j_lo * MOE_ROWS).astype(F32)
        sel = jnp.where(pos - off == liota, 1.0, 0.0).astype(BF16)
        rows = ys_sc[pl.ds(j_lo, 2)].reshape(2 * MOE_ROWS, ys_sc.shape[2])
        y = pc_ref[sub, ne + col:ne + col + 1] * _dot(sel, rows)
        if not first:
            y = yin_ref[sub, :].astype(F32) + y
        o_ref[sub, :] = y.astype(BF16)


def moe(h, comb_t, pos_t, cum_incl, w1, w3, w2):
    t, d = h.shape
    ne = comb_t.shape[0]
    tc = min(MOE_CHUNK, t)
    assert t % tc == 0 and tc % MOE_SUB == 0
    nsub = tc // MOE_SUB
    nt = t // MOE_SUB
    cum = jnp.concatenate([jnp.zeros((ne, 1), jnp.int32), cum_incl[:, ::LANES]], axis=1)
    cum = cum.reshape(-1)
    posr = pos_t.reshape(ne, 1, t)
    pc = jnp.concatenate([pos_t.astype(F32).T, comb_t.T], axis=1)
    once = pl.Buffered(1)
    y = None
    for e in range(ne):
        first = y is None
        chunk = pl.BlockSpec((tc, d), lambda c, cum: (c, 0))
        in_specs = [chunk,
                    pl.BlockSpec((1, 1, tc), lambda c, cum, e=e: (e, 0, c)),
                    pl.BlockSpec((tc, 2 * ne), lambda c, cum: (c, 0))]
        in_specs += [] if first else [chunk]
        in_specs += [pl.BlockSpec((1,) + w.shape[1:], lambda c, cum, e=e: (e, 0, 0), pipeline_mode=once)
                     for w in (w1, w3, w2)]
        grid_spec = pltpu.PrefetchScalarGridSpec(
            num_scalar_prefetch=1, grid=(t // tc,), in_specs=in_specs, out_specs=chunk,
            scratch_shapes=[pltpu.VMEM((MOE_ROWS, d), BF16),
                            pltpu.VMEM((tc // MOE_ROWS + 2, MOE_ROWS, d), BF16)])
        y = pl.pallas_call(
            functools.partial(_moe_kernel, nsub=nsub, cum0=e * (nt + 1), first=first, col=e),
            grid_spec=grid_spec,
            out_shape=jax.ShapeDtypeStruct((t, d), BF16),
            input_output_aliases={} if first else {4: 0},
            compiler_params=pltpu.CompilerParams(dimension_semantics=("arbitrary",),
                                                 vmem_limit_bytes=MOE_VMEM_LIMIT),
            name="moe_expert",
        )(*([cum, h, posr, pc] + ([] if first else [y]) + [w1, w3, w2]))
    return y


def _residual_kernel(x_ref, y_ref, gt_ref, g_ref, o_ref, *, norm):
    x = x_ref[...] + gt_ref[0] * y_ref[...].astype(F32)
    o_ref[...] = _rms(x, g_ref[...]) if norm else x


def residual(x, y, gt, g, batch, norm, tm=512):
    t, d = x.shape
    tm = min(tm, t // batch)
    npb = (t // batch) // tm
    return pl.pallas_call(
        functools.partial(_residual_kernel, norm=norm),
        grid=(t // tm,),
        in_specs=[pl.BlockSpec((tm, d), lambda i: (i, 0)), pl.BlockSpec((tm, d), lambda i: (i, 0)),
                  pl.BlockSpec((1, 1, d), lambda i: (i // npb, 0, 0)), _full(g.shape)],
        out_specs=pl.BlockSpec((tm, d), lambda i: (i, 0)),
        out_shape=jax.ShapeDtypeStruct((t, d), F32),
        compiler_params=_cparams(("parallel",)),
        name="residual_norm" if norm else "residual",
    )(x, y, gt, g)


def _final_norm_kernel(x_ref, g_ref, o_ref):
    o_ref[...] = _rms(x_ref[...], g_ref[...])


def final_norm(x, g, tm=512):
    t, d = x.shape
    return pl.pallas_call(
        _final_norm_kernel,
        grid=(t // tm,),
        in_specs=[pl.BlockSpec((tm, d), lambda i: (i, 0)), _full(g.shape)],
        out_specs=pl.BlockSpec((tm, d), lambda i: (i, 0)),
        out_shape=jax.ShapeDtypeStruct((t, d), F32),
        compiler_params=_cparams(("parallel",)),
        name="final_norm",
    )(x, g)


def _rope_tables(seq, rotate):
    half = ROPE_DIM // 2
    nf = ROPE_DIM // 4
    if rotate:
        t = jnp.arange(seq, dtype=jnp.int32)
        rows = (t // GRID_WIDTH).astype(F32)
        cols = (t % GRID_WIDTH).astype(F32)
        inv = ROPE_THETA ** (-jnp.arange(0, half, 2, dtype=F32) / half)
        ang = jnp.stack([rows[:, None] * inv, cols[:, None] * inv], axis=1)
        cos, sin = jnp.cos(ang), jnp.sin(ang)
    else:
        cos = jnp.ones((seq, 2, nf), F32)
        sin = jnp.zeros((seq, 2, nf), F32)
    cpat = jnp.concatenate([cos, cos], axis=2).reshape(seq, ROPE_DIM)
    spat = jnp.concatenate([-sin, sin], axis=2).reshape(seq, ROPE_DIM)
    return cpat, spat


def _rope_inputs(seq, rotate):
    cpat, spat = _rope_tables(seq, rotate)
    tabq = jnp.concatenate([jnp.tile(cpat, (1, NUM_HEADS)), jnp.tile(spat, (1, NUM_HEADS))], axis=1)
    tabk = jnp.concatenate([cpat, spat, jnp.zeros((seq, LANES - 2 * ROPE_DIM), F32)], axis=1)
    return tabq, tabk


def _swap_perm():
    j = np.arange(ROPE_DIM)
    half = ROPE_DIM // 4
    return np.where((j % (2 * half)) < half, j + half, j - half)


def _placement(shared):
    nsrc = ROPE_DIM if shared else NUM_HEADS * ROPE_DIM
    e = np.zeros((LANES if shared else 2 * nsrc, NUM_HEADS * HEAD_PAD), np.float32)
    for hd in range(NUM_HEADS):
        for j in range(ROPE_DIM):
            col = hd * HEAD_PAD + NOPE_DIM + j
            src = j if shared else hd * ROPE_DIM + j
            e[src, col] = 1.0
            e[nsrc + src, col] = 1.0
    return jnp.asarray(e, dtype=BF16)


def _prep_layer(w_in, b_gate, g_q, w_uq, g_kv, w_ukv, w_oa, w_ob, w_oc, w_out, w_pool, s_pool):
    d = w_in.shape[0]
    q_lora, kv_lora = g_q.shape[0], g_kv.shape[0]
    perm = _swap_perm()
    off_kr = q_lora + kv_lora
    off_pool = off_kr + ROPE_DIM
    pool_w = w_pool.shape[0] * w_pool.shape[1]
    off_fft = off_pool + pool_w
    off_gate = w_in.shape[1] - NUM_BRANCH * d
    kr = w_in[:, off_kr:off_pool]
    wa = jnp.concatenate([w_in[:, :off_kr], kr, kr[:, perm],
                          jnp.zeros((d, LANES - 2 * ROPE_DIM), F32)], axis=1)
    qscale = (NOPE_DIM + ROPE_DIM) ** -0.5 * math.log2(math.e)
    wq = w_uq * qscale
    wqn = jnp.pad(wq[:, :, :NOPE_DIM], ((0, 0), (0, 0), (0, HEAD_PAD - NOPE_DIM)))
    wqr = wq[:, :, NOPE_DIM:]
    wqr = jnp.concatenate([wqr.reshape(q_lora, -1), wqr[:, :, perm].reshape(q_lora, -1)], axis=1)
    wkn = jnp.pad(w_ukv[:, :, :NOPE_DIM], ((0, 0), (0, 0), (0, HEAD_PAD - NOPE_DIM)))
    wv = jnp.pad(w_ukv[:, :, NOPE_DIM:], ((0, 0), (0, 0), (0, VT_ROWS - VAL_DIM)))
    vone = np.zeros((NUM_HEADS, VT_ROWS, 1), np.float32)
    vone[:, VAL_DIM] = 1.0
    return dict(
        wa=wa.astype(BF16), wpool=w_in[:, off_pool:off_fft].astype(BF16),
        wfft=w_in[:, off_fft:off_gate].astype(BF16), wgate=w_in[:, off_gate:].astype(BF16),
        bgate=b_gate.reshape(1, -1),
        gq=g_q.reshape(1, -1), wqn=wqn.reshape(q_lora, -1).astype(BF16), wqr=wqr.astype(BF16),
        eq=_placement(False), gkv=g_kv.reshape(1, -1),
        wkn=wkn.reshape(kv_lora, -1).astype(BF16), ek=_placement(True),
        wvt=wv.reshape(kv_lora, -1).T.astype(BF16), vone=jnp.asarray(vone.reshape(-1, 1)),
        woa=w_oa.astype(BF16), wob=w_ob.astype(BF16), woc=w_oc.astype(BF16), wout=w_out.astype(BF16),
        wp=w_pool.astype(BF16), sp=s_pool.reshape(1, -1),
    )


def _token_mixer(xs, modv1, modv2, lw, g1, g2, ropes, dft, batch, kv_ctx, tm):
    tabq, tabk = ropes
    q, k, vt, pool_in, fft_in, gate = proj_in(xs, modv1, g1, tabq, tabk, lw, batch, False, tm)
    if kv_ctx is None:
        attn = attention(q, k, vt, None, None, batch)
    else:
        attn = attention(q, kv_ctx[0], kv_ctx[1], k, vt, batch)
    pooled = pool_mixer(pool_in, lw["wp"], lw["sp"], batch)
    four = fourier_mixer(fft_in, batch, dft)
    x_new, h2 = merge(attn, pooled, four, gate, xs, modv2, lw, g2, batch, tm)
    return x_new, h2, (k, vt)


def kernel(x, c, ctx, c_ctx, w_mod, b_mod, g_norm1, w_in, b_gate, g_q, w_uq, g_kv, w_ukv, w_pool, s_pool,
           w_oa, w_ob, w_oc, w_out, g_norm2, ffn_w1, ffn_w3, ffn_w2, moe_router, moe_bias,
           moe_w1, moe_w3, moe_w2, g_final):
    batch, seq, d = x.shape
    n_ctx = ctx.shape[1]
    depth = w_mod.shape[0]
    n_exp = moe_router.shape[-1] if moe_router.shape[0] else 0
    xl = x.reshape(batch * seq, d)
    xc = ctx.reshape(batch * n_ctx, d)

    cvec = jnp.concatenate([c, c_ctx[None], jnp.zeros((8 - batch - 1, d), F32)], axis=0)
    mods = modulation(cvec, w_mod, b_mod)

    rope_l = _rope_inputs(seq, True)
    rope_c = _rope_inputs(n_ctx, False)
    dft_l = _dft_tables(seq)
    dft_c = _dft_tables(n_ctx)

    for layer in range(depth):
        last = layer == depth - 1
        lw = _prep_layer(w_in[layer], b_gate[layer], g_q[layer], w_uq[layer], g_kv[layer], w_ukv[layer],
                         w_oa[layer], w_ob[layer], w_oc[layer], w_out[layer], w_pool[layer], s_pool[layer])
        g1 = g_norm1[layer].reshape(1, d)
        g2 = g_norm2[layer].reshape(1, d)
        ml = mods[layer, :batch].reshape(batch, 6, d)
        mc = jnp.broadcast_to(mods[layer, batch].reshape(1, 6, d), (batch, 6, d))

        if last:
            k_c, vt_c = proj_in(xc, mc[:, 0:2], g1, rope_c[0], rope_c[1], lw, batch, True)
            kv_c = (k_c, vt_c)
        else:
            xc_new, h2_c, kv_c = _token_mixer(xc, mc[:, 0:2], mc[:, 2:5], lw, g1, g2, rope_c, dft_c,
                                              batch, None, 256)
        xl, h2_l, _ = _token_mixer(xl, ml[:, 0:2], ml[:, 2:5], lw, g1, g2, rope_l, dft_l, batch, kv_c, 512)
        if not last:
            xc = xc_new

        streams = [(xl, h2_l, ml)]
        if not last:
            streams.append((xc, h2_c, mc))
        fuse_final = last and layer % 2 == 1
        new = []
        for xs, h2, mm in streams:
            gt2 = mm[:, 5:6]
            li = layer // 2
            if layer % 2 == 0:
                xs = ffn(h2, xs, gt2, ffn_w1[li].astype(BF16), ffn_w3[li].astype(BF16),
                         ffn_w2[li].astype(BF16), batch)
            else:
                comb_t, pos_t, cum_incl = router(xs, mm[:, 3:5], g2, moe_router[li].T,
                                                 moe_bias[li].reshape(n_exp, 1), batch)
                y = moe(h2, comb_t, pos_t, cum_incl, moe_w1[li].astype(BF16), moe_w3[li].astype(BF16),
                        moe_w2[li].astype(BF16))
                xs = residual(xs, y, gt2, g_final.reshape(1, d), batch, fuse_final)
            new.append(xs)
        xl = new[0]
        if not last:
            xc = new[1]

    if not fuse_final:
        xl = final_norm(xl, g_final.reshape(1, d))
    return xl.reshape(batch, seq, d)
```

```python
import functools
import math

import numpy as np
import jax
import jax.numpy as jnp
from jax import lax
from jax.experimental import pallas as pl
from jax.experimental.pallas import tpu as pltpu

GRID_WIDTH = 64
NUM_HEADS = 8
NOPE_DIM = 64
ROPE_DIM = 32
VAL_DIM = 64
ROPE_THETA = 10000.0
POOL_WINS = (2, 4, 8, 16)
GROUP_CH = 128
NUM_BRANCH = 3
NORM_EPS = 1e-6

LANES = 128
HEAD_PAD = 128
VT_ROWS = 80
POOL_HALO = 16
DFT_N2 = 128
DFT_GROUP = 8
VMEM_LIMIT = 56 * 1024 * 1024
MOE_ROWS = 256
MOE_SUB = 256
MOE_CHUNK = 2048
MOE_WINDOW = 6
MOE_VMEM_LIMIT = 60 * 1024 * 1024
MOD_COLS = 1536
ATTN_EXACT_KEYS = 512
STALE_MAX_JUMP = 64.0

BF16 = jnp.bfloat16
F32 = jnp.float32


def _cparams(sem):
    return pltpu.CompilerParams(dimension_semantics=sem, vmem_limit_bytes=VMEM_LIMIT)


def _full(shape):
    n = len(shape)
    return pl.BlockSpec(shape, lambda *_: (0,) * n)


def _dot(a, b):
    return jnp.dot(a, b, preferred_element_type=F32)


def _dot_nt(a, b):
    return lax.dot_general(a, b, (((1,), (1,)), ((), ())), preferred_element_type=F32)


def _rms(x, g):
    return x * lax.rsqrt(jnp.mean(x * x, axis=-1, keepdims=True) + NORM_EPS) * g


def _mod_kernel(c_ref, w_ref, b_ref, o_ref):
    c = c_ref[...]
    s = c * jax.nn.sigmoid(c)
    w = w_ref[0]
    s_hi = s.astype(BF16)
    s_lo = (s - s_hi.astype(F32)).astype(BF16)
    w_hi = w.astype(BF16)
    w_lo = (w - w_hi.astype(F32)).astype(BF16)
    part = _dot(jnp.concatenate([s_hi, s_lo], axis=0), w_hi)
    rows = s.shape[0]
    o_ref[0] = part[:rows] + part[rows:] + _dot(s_hi, w_lo) + b_ref[0]


def modulation(cvec, w_mod, b_mod):
    depth, d, n = w_mod.shape
    tn = MOD_COLS
    return pl.pallas_call(
        _mod_kernel,
        grid=(depth, n // tn),
        in_specs=[pl.BlockSpec((8, d), lambda l, j: (0, 0)),
                  pl.BlockSpec((1, d, tn), lambda l, j: (l, 0, j)),
                  pl.BlockSpec((1, 1, tn), lambda l, j: (l, 0, j))],
        out_specs=pl.BlockSpec((1, 8, tn), lambda l, j: (l, 0, j)),
        out_shape=jax.ShapeDtypeStruct((depth, 8, n), F32),
        compiler_params=_cparams(("parallel", "parallel")),
        name="modulation",
    )(cvec, w_mod, b_mod.reshape(depth, 1, n))


def _proj_in_kernel(x_ref, mod_ref, g1_ref, tabq_ref, tabk_ref,
                    wa_ref, wpool_ref, wfft_ref, wgate_ref, bgate_ref,
                    gq_ref, wqn_ref, wqr_ref, eq_ref, gkv_ref, wkn_ref, ek_ref, wvt_ref, vone_ref,
                    *out_refs, q_lora, kv_lora, kv_only):
    if kv_only:
        k_ref, vt_ref = out_refs
    else:
        q_ref, k_ref, vt_ref, pool_ref, fft_ref, gate_ref = out_refs
    x = x_ref[...]
    mod = mod_ref[0]
    h = (_rms(x, g1_ref[...]) * (1.0 + mod[1:2]) + mod[0:1]).astype(BF16)
    pa = _dot(h, wa_ref[...])
    kvn = _rms(pa[:, q_lora:q_lora + kv_lora], gkv_ref[...]).astype(BF16)
    kr = (pa[:, q_lora + kv_lora:] * tabk_ref[...]).astype(BF16)
    k_all = (_dot(kvn, wkn_ref[...]) + _dot(kr, ek_ref[...])).astype(BF16)
    for hd in range(NUM_HEADS):
        k_ref[hd] = k_all[:, hd * HEAD_PAD:(hd + 1) * HEAD_PAD]
    vt_ref[0] = (_dot_nt(wvt_ref[...], kvn) + vone_ref[...]).astype(BF16)
    if kv_only:
        return
    qn = _rms(pa[:, :q_lora], gq_ref[...]).astype(BF16)
    qr = (_dot(qn, wqr_ref[...]) * tabq_ref[...]).astype(BF16)
    q_all = (_dot(qn, wqn_ref[...]) + _dot(qr, eq_ref[...])).astype(BF16)
    for hd in range(NUM_HEADS):
        q_ref[hd] = q_all[:, hd * HEAD_PAD:(hd + 1) * HEAD_PAD]
    pool_ref[...] = _dot(h, wpool_ref[...])
    fft_ref[...] = _dot(h, wfft_ref[...])
    gate_ref[...] = jax.nn.sigmoid(_dot(h, wgate_ref[...]) + bgate_ref[...]).astype(BF16)


def proj_in(x, modv, g1, tabq, tabk, lw, batch, kv_only, tm=256):
    t, d = x.shape
    seq = t // batch
    npb = seq // tm
    q_lora = lw["gq"].shape[1]
    kv_lora = lw["gkv"].shape[1]
    weights = [lw["wa"], lw["wpool"], lw["wfft"], lw["wgate"], lw["bgate"],
               lw["gq"], lw["wqn"], lw["wqr"], lw["eq"], lw["gkv"], lw["wkn"], lw["ek"],
               lw["wvt"], lw["vone"]]
    in_specs = [pl.BlockSpec((tm, d), lambda i: (i, 0)),
                pl.BlockSpec((1, 2, d), lambda i: (i // npb, 0, 0)),
                _full(g1.shape),
                pl.BlockSpec((tm, tabq.shape[1]), lambda i: (i % npb, 0)),
                pl.BlockSpec((tm, tabk.shape[1]), lambda i: (i % npb, 0))]
    in_specs += [_full(w.shape) for w in weights]
    head_spec = pl.BlockSpec((NUM_HEADS, tm, HEAD_PAD), lambda i: (0, i, 0))
    head_shape = jax.ShapeDtypeStruct((NUM_HEADS, t, HEAD_PAD), BF16)
    vt_spec = pl.BlockSpec((1, NUM_HEADS * VT_ROWS, tm), lambda i: (i // npb, 0, i % npb))
    vt_shape = jax.ShapeDtypeStruct((batch, NUM_HEADS * VT_ROWS, seq), BF16)
    if kv_only:
        out_specs = [head_spec, vt_spec]
        out_shape = [head_shape, vt_shape]
    else:
        pw, fw, gw = lw["wpool"].shape[1], lw["wfft"].shape[1], lw["wgate"].shape[1]
        out_specs = [head_spec, head_spec, vt_spec,
                     pl.BlockSpec((tm, pw), lambda i: (i, 0)),
                     pl.BlockSpec((tm, fw), lambda i: (i, 0)),
                     pl.BlockSpec((tm, gw), lambda i: (i, 0))]
        out_shape = [head_shape, head_shape, vt_shape,
                     jax.ShapeDtypeStruct((t, pw), F32),
                     jax.ShapeDtypeStruct((t, fw), F32),
                     jax.ShapeDtypeStruct((t, gw), BF16)]
    return pl.pallas_call(
        functools.partial(_proj_in_kernel, q_lora=q_lora, kv_lora=kv_lora, kv_only=kv_only),
        grid=(t // tm,),
        in_specs=in_specs, out_specs=out_specs, out_shape=out_shape,
        compiler_params=_cparams(("parallel",)),
        name="proj_in_kv" if kv_only else "proj_in",
    )(x, modv, g1, tabq, tabk, *weights)


def _attn_kernel(q_ref, kc_ref, vtc_ref, *rest, tk, tk_exact, n_lat_tiles):
    if n_lat_tiles:
        kl_ref, vtl_ref, o_ref = rest
    else:
        (o_ref,) = rest
    nh = q_ref.shape[0]

    def rows(hd):
        return slice(hd * VT_ROWS, (hd + 1) * VT_ROWS)

    def scores(ks):
        return tuple(_dot_nt(ks[hd], q_ref[hd]) for hd in range(nh))

    def update(ss, vts, state):
        out = []
        for hd in range(nh):
            smax = jnp.max(ss[hd], axis=0, keepdims=True)
            if state is None:
                m_new = smax
                acc = _dot(vts[hd], jnp.exp2(ss[hd] - m_new).astype(BF16))
            else:
                m_old, acc_old = state[hd]
                m_new = jnp.maximum(m_old, smax)
                pv = _dot(vts[hd], jnp.exp2(ss[hd] - m_new).astype(BF16))
                acc = acc_old * jnp.exp2(m_old - m_new) + pv
            out.append((m_new, acc))
        return tuple(out)

    def lat_k(off, n):
        return [kl_ref[hd, pl.ds(off, n), :] for hd in range(nh)]

    def lat_vt(off, n):
        return [vtl_ref[0, rows(hd), pl.ds(off, n)] for hd in range(nh)]

    def update_stale(ss, vts, state, jump):
        out = []
        for hd in range(nh):
            m_old, acc_old = state[hd]
            pv = _dot(vts[hd], jnp.exp2(ss[hd] - m_old).astype(BF16))
            smax = jnp.max(ss[hd], axis=0, keepdims=True)
            m_new = jnp.maximum(m_old, smax)
            out.append((m_new, (acc_old + pv) * jnp.exp2(m_old - m_new)))
            jump = jnp.maximum(jump, smax - m_old)
        return tuple(out), jump

    state = update(scores([kc_ref[hd] for hd in range(nh)]),
                   [vtc_ref[0, rows(hd), :] for hd in range(nh)], None)
    if n_lat_tiles:
        def fast_body(i, carry):
            off = pl.multiple_of(i * tk, tk)
            return update_stale(scores(lat_k(off, tk)), lat_vt(off, tk), *carry)
        fast, jump = lax.fori_loop(0, n_lat_tiles, fast_body,
                                   (state, jnp.zeros((1, q_ref.shape[1]), F32)))

        def exact_loop():
            def body(i, st):
                off = pl.multiple_of(i * tk_exact, tk_exact)
                return update(scores(lat_k(off, tk_exact)), lat_vt(off, tk_exact), st)
            return lax.fori_loop(0, n_lat_tiles * (tk // tk_exact), body, state)

        state = lax.cond(jnp.max(jump) > STALE_MAX_JUMP, exact_loop, lambda: fast)
    for pair in range(nh // 2):
        outs = []
        for hd in (2 * pair, 2 * pair + 1):
            acc = state[hd][1]
            outs.append(acc[:VAL_DIM] / acc[VAL_DIM:VAL_DIM + 1])
        o_ref[:, pair * 2 * VAL_DIM:(pair + 1) * 2 * VAL_DIM] = (
            jnp.concatenate(outs, axis=0).T.astype(BF16))


def attention(q, kc, vtc, kl, vtl, batch, tq=256, tk=2048):
    h, t, _ = q.shape
    lq = t // batch
    nq = lq // tq
    lc = kc.shape[1] // batch
    once = pl.Buffered(1)
    in_specs = [pl.BlockSpec((h, tq, HEAD_PAD), lambda b, i: (0, b * nq + i, 0)),
                pl.BlockSpec((h, lc, HEAD_PAD), lambda b, i: (0, b, 0), pipeline_mode=once),
                pl.BlockSpec((1, h * VT_ROWS, lc), lambda b, i: (b, 0, 0), pipeline_mode=once)]
    args = [q, kc, vtc]
    n_lat_tiles = 0
    if kl is not None:
        ll = kl.shape[1] // batch
        tk = min(tk, ll)
        n_lat_tiles = ll // tk
        in_specs += [pl.BlockSpec((h, ll, HEAD_PAD), lambda b, i: (0, b, 0), pipeline_mode=once),
                     pl.BlockSpec((1, h * VT_ROWS, ll), lambda b, i: (b, 0, 0), pipeline_mode=once)]
        args += [kl, vtl]
    return pl.pallas_call(
        functools.partial(_attn_kernel, tk=tk, tk_exact=min(tk, ATTN_EXACT_KEYS), n_lat_tiles=n_lat_tiles),
        grid=(batch, nq),
        in_specs=in_specs,
        out_specs=pl.BlockSpec((tq, h * VAL_DIM), lambda b, i: (b * nq + i, 0)),
        out_shape=jax.ShapeDtypeStruct((t, h * VAL_DIM), BF16),
        compiler_params=_cparams(("parallel", "arbitrary")),
        name="attention",
    )(*args)


def _pool_kernel(x_ref, prev_ref, next_ref, band_ref, wp_ref, sp_ref, o_ref, *, seq, tm):
    i = pl.program_id(0)
    npb = seq // tm
    pos0 = (i % npb) * tm
    has_prev = (pos0 > 0).astype(F32)
    has_next = (pos0 + tm < seq).astype(F32)
    x = x_ref[...]
    ext = jnp.concatenate([prev_ref[...] * has_prev, x, next_ref[...] * has_next], axis=0).astype(BF16)
    t = pos0 + lax.broadcasted_iota(jnp.int32, (tm, GROUP_CH), 0)
    outs = []
    for g, win in enumerate(POOL_WINS):
        left = win // 2
        right = win - 1 - left
        cnt = (jnp.minimum(t + right + 1, seq) - jnp.maximum(t - left, 0)).astype(F32)
        sl = slice(g * GROUP_CH, (g + 1) * GROUP_CH)
        wsum = _dot(band_ref[g], ext[:, sl])
        mixed = (wsum / cnt - x[:, sl]).astype(BF16)
        outs.append(_dot(mixed, wp_ref[g]))
    o_ref[...] = (jnp.concatenate(outs, axis=1) * sp_ref[...]).astype(BF16)


def _pool_bands(tm):
    r = np.arange(tm)[:, None]
    c = np.arange(tm + 2 * POOL_HALO)[None, :]
    d = c - POOL_HALO - r
    bands = [((d >= -(w // 2)) & (d <= w - 1 - w // 2)) for w in POOL_WINS]
    return jnp.asarray(np.stack(bands).astype(np.float32), dtype=BF16)


def pool_mixer(p, wp, sp, batch, tm=256):
    t, w = p.shape
    seq = t // batch
    tm = min(tm, seq)
    hb = tm // POOL_HALO
    nh = t // POOL_HALO
    return pl.pallas_call(
        functools.partial(_pool_kernel, seq=seq, tm=tm),
        grid=(t // tm,),
        in_specs=[pl.BlockSpec((tm, w), lambda i: (i, 0)),
                  pl.BlockSpec((POOL_HALO, w), lambda i: (jnp.maximum(i * hb - 1, 0), 0)),
                  pl.BlockSpec((POOL_HALO, w), lambda i: (jnp.minimum((i + 1) * hb, nh - 1), 0)),
                  _full((len(POOL_WINS), tm, tm + 2 * POOL_HALO)),
                  _full(wp.shape), _full(sp.shape)],
        out_specs=pl.BlockSpec((tm, w), lambda i: (i, 0)),
        out_shape=jax.ShapeDtypeStruct((t, w), BF16),
        compiler_params=_cparams(("parallel",)),
        name="pool_mixer",
    )(p, p, p, _pool_bands(tm), wp, sp)


def _dft_outer_kernel(f_ref, x_ref, o_ref):
    _, n1, grp, w = x_ref.shape
    out = _dot(f_ref[...], x_ref[0].reshape(n1 * grp, w).astype(BF16))
    o_ref[0] = out.reshape(o_ref.shape[1], grp, w)


def _dft_inner_kernel(a_ref, g_ref, cs_ref, o_ref, *, scale):
    n2 = g_ref.shape[1] // 2
    for k in range(g_ref.shape[0]):
        rhs = a_ref[0, :, k].reshape(2 * n2, a_ref.shape[-1]).astype(BF16)
        p = _dot(g_ref[k], rhs)
        outs = []
        for g in range(p.shape[1] // GROUP_CH):
            sl = slice(g * GROUP_CH, (g + 1) * GROUP_CH)
            lhs = jnp.concatenate([p[:n2, sl], p[n2:, sl]], axis=1).astype(BF16)
            outs.append(_dot(lhs, cs_ref[...]))
        o_ref[0, :, k, :] = jnp.concatenate(outs, axis=1) * scale


def _dft_tables(seq):
    n2 = seq if seq <= 2 * DFT_N2 else DFT_N2
    n1 = seq // n2
    k2 = jnp.arange(n2, dtype=jnp.int32)[:, None]
    t2 = jnp.arange(n2, dtype=jnp.int32)[None, :]
    k1 = jnp.arange(n1, dtype=jnp.int32)[:, None, None]
    ang_a = ((t2 * k2) % n2).astype(F32) * (2.0 * math.pi / n2)
    ang_b = ((t2 * k1) % seq).astype(F32) * (2.0 * math.pi / seq)
    ca, sa, cb, sb = jnp.cos(ang_a), jnp.sin(ang_a), jnp.cos(ang_b), jnp.sin(ang_b)
    gr = ca * cb - sa * sb
    gi = -(sa * cb + ca * sb)
    gmat = jnp.concatenate([jnp.concatenate([gr, -gi], axis=2),
                            jnp.concatenate([gi, gr], axis=2)], axis=1).astype(BF16)
    r = jnp.arange(2 * n1 * DFT_GROUP, dtype=jnp.int32)[:, None]
    q = jnp.arange(n1 * DFT_GROUP, dtype=jnp.int32)[None, :]
    ang1 = ((((r // DFT_GROUP) % n1) * (q // DFT_GROUP)) % n1).astype(F32) * (2.0 * math.pi / n1)
    f1 = jnp.where(r // DFT_GROUP >= n1, -jnp.sin(ang1), jnp.cos(ang1))
    f1 = jnp.where(r % DFT_GROUP == q % DFT_GROUP, f1, 0.0).astype(BF16)
    c = jnp.arange(GROUP_CH, dtype=jnp.int32)
    angc = ((c[:, None] * c[None, :]) % GROUP_CH).astype(F32) * (2.0 * math.pi / GROUP_CH)
    cs = jnp.concatenate([jnp.cos(angc), jnp.sin(angc)], axis=0).astype(BF16)
    return gmat, f1, cs


def fourier_mixer(f, batch, tables):
    gmat, f1, cs = tables
    t, w = f.shape
    seq = t // batch
    n1, n2x2, _ = gmat.shape
    n2 = n2x2 // 2
    scale = 1.0 / math.sqrt(seq * GROUP_CH)
    if n1 > 1:
        a = pl.pallas_call(
            _dft_outer_kernel,
            grid=(batch, n2 // DFT_GROUP),
            in_specs=[_full(f1.shape), pl.BlockSpec((1, n1, DFT_GROUP, w), lambda b, j: (b, 0, j, 0))],
            out_specs=pl.BlockSpec((1, 2 * n1, DFT_GROUP, w), lambda b, j: (b, 0, j, 0)),
            out_shape=jax.ShapeDtypeStruct((batch, 2 * n1, n2, w), F32),
            compiler_params=_cparams(("parallel", "parallel")),
            name="dft_outer",
        )(f1, f.reshape(batch, n1, n2, w))
        a = a.reshape(batch, 2, n1, n2, w)
    else:
        fr = f.reshape(batch, 1, 1, n2, w)
        a = jnp.concatenate([fr, jnp.zeros_like(fr)], axis=1)
    kg = min(DFT_GROUP, n1)
    out = pl.pallas_call(
        functools.partial(_dft_inner_kernel, scale=scale),
        grid=(batch, n1 // kg),
        in_specs=[pl.BlockSpec((1, 2, kg, n2, w), lambda b, k: (b, 0, k, 0, 0)),
                  pl.BlockSpec((kg, 2 * n2, 2 * n2), lambda b, k: (k, 0, 0)),
                  _full(cs.shape)],
        out_specs=pl.BlockSpec((1, n2, kg, w), lambda b, k: (b, 0, k, 0)),
        out_shape=jax.ShapeDtypeStruct((batch, n2, n1, w), F32),
        compiler_params=_cparams(("parallel", "parallel")),
        name="dft_inner",
    )(a, gmat, cs)
    return out.reshape(t, w)


def _merge_kernel(a_ref, p_ref, f_ref, g_ref, x_ref, mod_ref, woa_ref, wob_ref, woc_ref, wout_ref,
                  g2_ref, xo_ref, h_ref):
    d = x_ref.shape[1]
    gate = g_ref[...].astype(F32)
    y = (gate[:, :d] * _dot(a_ref[...], woa_ref[...])
         + gate[:, d:2 * d] * _dot(p_ref[...], wob_ref[...])
         + gate[:, 2 * d:] * _dot(f_ref[...].astype(BF16), woc_ref[...]))
    mod = mod_ref[0]
    xn = x_ref[...] + mod[0:1] * _dot(y.astype(BF16), wout_ref[...])
    xo_ref[...] = xn
    h_ref[...] = (_rms(xn, g2_ref[...]) * (1.0 + mod[2:3]) + mod[1:2]).astype(BF16)


def merge(attn, pooled, four, gate, x, modv, lw, g2, batch, tm=256):
    t, d = x.shape
    npb = (t // batch) // tm
    row = lambda w: pl.BlockSpec((tm, w), lambda i: (i, 0))
    weights = [lw["woa"], lw["wob"], lw["woc"], lw["wout"], g2]
    return pl.pallas_call(
        _merge_kernel,
        grid=(t // tm,),
        in_specs=[row(attn.shape[1]), row(pooled.shape[1]), row(four.shape[1]), row(gate.shape[1]),
                  row(d), pl.BlockSpec((1, 3, d), lambda i: (i // npb, 0, 0))]
                 + [_full(w.shape) for w in weights],
        out_specs=[row(d), row(d)],
        out_shape=[jax.ShapeDtypeStruct((t, d), F32), jax.ShapeDtypeStruct((t, d), BF16)],
        compiler_params=_cparams(("parallel",)),
        name="merge",
    )(attn, pooled, four, gate, x, modv, *weights)


def _ffn_kernel(h_ref, x_ref, gt_ref, w1_ref, w3_ref, w2_ref, o_ref):
    h = h_ref[...]
    a = _dot(h, w1_ref[...])
    b = _dot(h, w3_ref[...])
    g = (a * jax.nn.sigmoid(a) * b).astype(BF16)
    o_ref[...] = x_ref[...] + gt_ref[0] * _dot(g, w2_ref[...])


def ffn(h, x, gt, w1, w3, w2, batch, tm=256):
    t, d = x.shape
    npb = (t // batch) // tm
    return pl.pallas_call(
        _ffn_kernel,
        grid=(t // tm,),
        in_specs=[pl.BlockSpec((tm, d), lambda i: (i, 0)),
                  pl.BlockSpec((tm, d), lambda i: (i, 0)),
                  pl.BlockSpec((1, 1, d), lambda i: (i // npb, 0, 0)),
                  _full(w1.shape), _full(w3.shape), _full(w2.shape)],
        out_specs=pl.BlockSpec((tm, d), lambda i: (i, 0)),
        out_shape=jax.ShapeDtypeStruct((t, d), F32),
        compiler_params=_cparams(("parallel",)),
        name="ffn",
    )(h, x, gt, w1, w3, w2)


def _router_kernel(x_ref, mod_ref, g2_ref, wr_ref, br_ref, tri_ref, comb_ref, pos_ref, cum_ref, cnt_sc):
    @pl.when(pl.program_id(0) == 0)
    def _():
        cnt_sc[...] = jnp.zeros_like(cnt_sc)

    mod = mod_ref[0]
    h = _rms(x_ref[...], g2_ref[...]) * (1.0 + mod[1:2]) + mod[0:1]
    h_hi = h.astype(BF16)
    h_lo = (h - h_hi.astype(F32)).astype(BF16)
    w = wr_ref[...]
    w_hi = w.astype(BF16)
    w_lo = (w - w_hi.astype(F32)).astype(BF16)
    ne = w.shape[0]
    part = _dot_nt(jnp.concatenate([w_hi, w_lo], axis=0), h_hi)
    logits = part[:ne] + part[ne:] + _dot_nt(w_hi, h_lo) + br_ref[...]
    ne = logits.shape[0]
    eidx = lax.broadcasted_iota(jnp.int32, logits.shape, 0)
    m1 = jnp.max(logits, axis=0, keepdims=True)
    i1 = jnp.min(jnp.where(logits == m1, eidx, ne), axis=0, keepdims=True)
    sel1 = eidx == i1
    rest = jnp.where(sel1, -jnp.inf, logits)
    m2 = jnp.max(rest, axis=0, keepdims=True)
    i2 = jnp.min(jnp.where(rest == m2, eidx, ne), axis=0, keepdims=True)
    sel2 = eidx == i2
    e2 = jnp.exp(m2 - m1)
    w1 = 1.0 / (1.0 + e2)
    comb_ref[...] = jnp.where(sel1, w1, 0.0) + jnp.where(sel2, e2 * w1, 0.0)
    self = jnp.where(sel1 | sel2, 1.0, 0.0)
    before = cnt_sc[...][:, 0:1]
    rank = before + _dot(self.astype(BF16), tri_ref[...])
    pos_ref[...] = jnp.where(self > 0.0, rank, -1.0).astype(jnp.int32)
    after = before + jnp.sum(self, axis=1, keepdims=True)
    cnt_sc[...] = jnp.broadcast_to(after, cnt_sc.shape)
    cum_ref[...] = jnp.broadcast_to(after, cum_ref.shape).astype(jnp.int32)


def router(x, modv, g2, wr_t, br, batch, tm=256):
    t, d = x.shape
    ne = wr_t.shape[0]
    npb = (t // batch) // tm
    nt = t // tm
    tri = jnp.asarray(np.triu(np.ones((tm, tm), np.float32), 1), dtype=BF16)
    return pl.pallas_call(
        _router_kernel,
        grid=(nt,),
        in_specs=[pl.BlockSpec((tm, d), lambda i: (i, 0)),
                  pl.BlockSpec((1, 2, d), lambda i: (i // npb, 0, 0)),
                  _full(g2.shape), _full(wr_t.shape), _full(br.shape), _full(tri.shape)],
        out_specs=[pl.BlockSpec((ne, tm), lambda i: (0, i)),
                   pl.BlockSpec((ne, tm), lambda i: (0, i)),
                   pl.BlockSpec((ne, LANES), lambda i: (0, i))],
        out_shape=[jax.ShapeDtypeStruct((ne, t), F32), jax.ShapeDtypeStruct((ne, t), jnp.int32),
                   jax.ShapeDtypeStruct((ne, nt * LANES), jnp.int32)],
        scratch_shapes=[pltpu.VMEM((ne, LANES), F32)],
        compiler_params=_cparams(("arbitrary",)),
        name="router",
    )(x, modv, g2, wr_t, br, tri)


def _moe_kernel(cum_ref, h_ref, posr_ref, pc_ref, *rest, nsub, cum0, first, col):
    if first:
        w1_ref, w3_ref, w2_ref, o_ref, hc_sc, ys_sc = rest
    else:
        yin_ref, w1_ref, w3_ref, w2_ref, o_ref, hc_sc, ys_sc = rest
    c = pl.program_id(0)

    def cum(i):
        return cum_ref[cum0 + c * nsub + i]

    base = cum(0)
    total = cum(nsub) - base
    nfull = total // MOE_ROWS
    rem = total - nfull * MOE_ROWS
    nb = nfull + (rem > 0).astype(jnp.int32)
    nwin = min(MOE_WINDOW, nsub)

    def expert_block(j, nrows):
        off = base + j * MOE_ROWS
        s_lo = sum((cum(s + 1) <= off).astype(jnp.int32) for s in range(nsub))
        s_hi = sum((cum(s) < off + nrows).astype(jnp.int32) for s in range(nsub))

        def compact(win, width):
            iota = lax.broadcasted_iota(jnp.int32, (nrows, width), 0)
            sel = jnp.where(posr_ref[0, :, win] - off == iota, 1.0, 0.0).astype(BF16)
            hc_sc[:nrows] = _dot(sel, h_ref[win, :]).astype(BF16)

        @pl.when(s_hi - s_lo <= nwin)
        def _():
            start = pl.multiple_of(jnp.minimum(s_lo, nsub - nwin) * MOE_SUB, MOE_SUB)
            compact(pl.ds(start, nwin * MOE_SUB), nwin * MOE_SUB)

        if nwin < nsub:
            @pl.when(s_hi - s_lo > nwin)
            def _():
                compact(slice(None), nsub * MOE_SUB)

        hc = hc_sc[:nrows]
        a = _dot(hc, w1_ref[0])
        b = _dot(hc, w3_ref[0])
        g = (a * jax.nn.sigmoid(a) * b).astype(BF16)
        ys_sc[j, :nrows] = _dot(g, w2_ref[0]).astype(BF16)
        if nrows < MOE_ROWS:
            ys_sc[j, nrows:] = jnp.zeros((MOE_ROWS - nrows, ys_sc.shape[2]), BF16)

    def full_block(j, carry):
        expert_block(j, MOE_ROWS)
        return carry

    lax.fori_loop(0, nfull, full_block, 0)

    @pl.when(rem > MOE_ROWS // 2)
    def _():
        expert_block(nfull, MOE_ROWS)

    @pl.when((rem > 0) & (rem <= MOE_ROWS // 2))
    def _():
        expert_block(nfull, MOE_ROWS // 2)

    ys_sc[nb] = jnp.zeros(ys_sc.shape[1:], BF16)
    ys_sc[nb + 1] = jnp.zeros(ys_sc.shape[1:], BF16)
    ne = pc_ref.shape[1] // 2
    liota = lax.broadcasted_iota(jnp.int32, (MOE_SUB, 2 * MOE_ROWS), 1).astype(F32)
    for s in range(nsub):
        sub = slice(s * MOE_SUB, (s + 1) * MOE_SUB)
        pos = pc_ref[sub, col:col + 1]
        j_lo = jnp.minimum((cum(s) - base) // MOE_ROWS, nb)
        off = (base + j_lo * MOE_ROWS).astype(F32)
        sel = jnp.where(pos - off == liota, 1.0, 0.0).astype(BF16)
        rows = ys_sc[pl.ds(j_lo, 2)].reshape(2 * MOE_ROWS, ys_sc.shape[2])
        y = pc_ref[sub, ne + col:ne + col + 1] * _dot(sel, rows)
        if not first:
            y = yin_ref[sub, :].astype(F32) + y
        o_ref[sub, :] = y.astype(BF16)


def moe(h, comb_t, pos_t, cum_incl, w1, w3, w2):
    t, d = h.shape
    ne = comb_t.shape[0]
    tc = min(MOE_CHUNK, t)
    assert t % tc == 0 and tc % MOE_SUB == 0
    nsub = tc // MOE_SUB
    nt = t // MOE_SUB
    cum = jnp.concatenate([jnp.zeros((ne, 1), jnp.int32), cum_incl[:, ::LANES]], axis=1)
    cum = cum.reshape(-1)
    posr = pos_t.reshape(ne, 1, t)
    pc = jnp.concatenate([pos_t.astype(F32).T, comb_t.T], axis=1)
    once = pl.Buffered(1)
    y = None
    for e in range(ne):
        first = y is None
        chunk = pl.BlockSpec((tc, d), lambda c, cum: (c, 0))
        in_specs = [chunk,
                    pl.BlockSpec((1, 1, tc), lambda c, cum, e=e: (e, 0, c)),
                    pl.BlockSpec((tc, 2 * ne), lambda c, cum: (c, 0))]
        in_specs += [] if first else [chunk]
        in_specs += [pl.BlockSpec((1,) + w.shape[1:], lambda c, cum, e=e: (e, 0, 0), pipeline_mode=once)
                     for w in (w1, w3, w2)]
        grid_spec = pltpu.PrefetchScalarGridSpec(
            num_scalar_prefetch=1, grid=(t // tc,), in_specs=in_specs, out_specs=chunk,
            scratch_shapes=[pltpu.VMEM((MOE_ROWS, d), BF16),
                            pltpu.VMEM((tc // MOE_ROWS + 2, MOE_ROWS, d), BF16)])
        y = pl.pallas_call(
            functools.partial(_moe_kernel, nsub=nsub, cum0=e * (nt + 1), first=first, col=e),
            grid_spec=grid_spec,
            out_shape=jax.ShapeDtypeStruct((t, d), BF16),
            input_output_aliases={} if first else {4: 0},
            compiler_params=pltpu.CompilerParams(dimension_semantics=("arbitrary",),
                                                 vmem_limit_bytes=MOE_VMEM_LIMIT),
            name="moe_expert",
        )(*([cum, h, posr, pc] + ([] if first else [y]) + [w1, w3, w2]))
    return y


def _residual_kernel(x_ref, y_ref, gt_ref, g_ref, o_ref, *, norm):
    x = x_ref[...] + gt_ref[0] * y_ref[...].astype(F32)
    o_ref[...] = _rms(x, g_ref[...]) if norm else x


def residual(x, y, gt, g, batch, norm, tm=512):
    t, d = x.shape
    tm = min(tm, t // batch)
    npb = (t // batch) // tm
    return pl.pallas_call(
        functools.partial(_residual_kernel, norm=norm),
        grid=(t // tm,),
        in_specs=[pl.BlockSpec((tm, d), lambda i: (i, 0)), pl.BlockSpec((tm, d), lambda i: (i, 0)),
                  pl.BlockSpec((1, 1, d), lambda i: (i // npb, 0, 0)), _full(g.shape)],
        out_specs=pl.BlockSpec((tm, d), lambda i: (i, 0)),
        out_shape=jax.ShapeDtypeStruct((t, d), F32),
        compiler_params=_cparams(("parallel",)),
        name="residual_norm" if norm else "residual",
    )(x, y, gt, g)


def _final_norm_kernel(x_ref, g_ref, o_ref):
    o_ref[...] = _rms(x_ref[...], g_ref[...])


def final_norm(x, g, tm=512):
    t, d = x.shape
    return pl.pallas_call(
        _final_norm_kernel,
        grid=(t // tm,),
        in_specs=[pl.BlockSpec((tm, d), lambda i: (i, 0)), _full(g.shape)],
        out_specs=pl.BlockSpec((tm, d), lambda i: (i, 0)),
        out_shape=jax.ShapeDtypeStruct((t, d), F32),
        compiler_params=_cparams(("parallel",)),
        name="final_norm",
    )(x, g)


def _rope_tables(seq, rotate):
    half = ROPE_DIM // 2
    nf = ROPE_DIM // 4
    if rotate:
        t = jnp.arange(seq, dtype=jnp.int32)
        rows = (t // GRID_WIDTH).astype(F32)
        cols = (t % GRID_WIDTH).astype(F32)
        inv = ROPE_THETA ** (-jnp.arange(0, half, 2, dtype=F32) / half)
        ang = jnp.stack([rows[:, None] * inv, cols[:, None] * inv], axis=1)
        cos, sin = jnp.cos(ang), jnp.sin(ang)
    else:
        cos = jnp.ones((seq, 2, nf), F32)
        sin = jnp.zeros((seq, 2, nf), F32)
    cpat = jnp.concatenate([cos, cos], axis=2).reshape(seq, ROPE_DIM)
    spat = jnp.concatenate([-sin, sin], axis=2).reshape(seq, ROPE_DIM)
    return cpat, spat


def _rope_inputs(seq, rotate):
    cpat, spat = _rope_tables(seq, rotate)
    tabq = jnp.concatenate([jnp.tile(cpat, (1, NUM_HEADS)), jnp.tile(spat, (1, NUM_HEADS))], axis=1)
    tabk = jnp.concatenate([cpat, spat, jnp.zeros((seq, LANES - 2 * ROPE_DIM), F32)], axis=1)
    return tabq, tabk


def _swap_perm():
    j = np.arange(ROPE_DIM)
    half = ROPE_DIM // 4
    return np.where((j % (2 * half)) < half, j + half, j - half)


def _placement(shared):
    nsrc = ROPE_DIM if shared else NUM_HEADS * ROPE_DIM
    e = np.zeros((LANES if shared else 2 * nsrc, NUM_HEADS * HEAD_PAD), np.float32)
    for hd in range(NUM_HEADS):
        for j in range(ROPE_DIM):
            col = hd * HEAD_PAD + NOPE_DIM + j
            src = j if shared else hd * ROPE_DIM + j
            e[src, col] = 1.0
            e[nsrc + src, col] = 1.0
    return jnp.asarray(e, dtype=BF16)


def _prep_layer(w_in, b_gate, g_q, w_uq, g_kv, w_ukv, w_oa, w_ob, w_oc, w_out, w_pool, s_pool):
    d = w_in.shape[0]
    q_lora, kv_lora = g_q.shape[0], g_kv.shape[0]
    perm = _swap_perm()
    off_kr = q_lora + kv_lora
    off_pool = off_kr + ROPE_DIM
    pool_w = w_pool.shape[0] * w_pool.shape[1]
    off_fft = off_pool + pool_w
    off_gate = w_in.shape[1] - NUM_BRANCH * d
    kr = w_in[:, off_kr:off_pool]
    wa = jnp.concatenate([w_in[:, :off_kr], kr, kr[:, perm],
                          jnp.zeros((d, LANES - 2 * ROPE_DIM), F32)], axis=1)
    qscale = (NOPE_DIM + ROPE_DIM) ** -0.5 * math.log2(math.e)
    wq = w_uq * qscale
    wqn = jnp.pad(wq[:, :, :NOPE_DIM], ((0, 0), (0, 0), (0, HEAD_PAD - NOPE_DIM)))
    wqr = wq[:, :, NOPE_DIM:]
    wqr = jnp.concatenate([wqr.reshape(q_lora, -1), wqr[:, :, perm].reshape(q_lora, -1)], axis=1)
    wkn = jnp.pad(w_ukv[:, :, :NOPE_DIM], ((0, 0), (0, 0), (0, HEAD_PAD - NOPE_DIM)))
    wv = jnp.pad(w_ukv[:, :, NOPE_DIM:], ((0, 0), (0, 0), (0, VT_ROWS - VAL_DIM)))
    vone = np.zeros((NUM_HEADS, VT_ROWS, 1), np.float32)
    vone[:, VAL_DIM] = 1.0
    return dict(
        wa=wa.astype(BF16), wpool=w_in[:, off_pool:off_fft].astype(BF16),
        wfft=w_in[:, off_fft:off_gate].astype(BF16), wgate=w_in[:, off_gate:].astype(BF16),
        bgate=b_gate.reshape(1, -1),
        gq=g_q.reshape(1, -1), wqn=wqn.reshape(q_lora, -1).astype(BF16), wqr=wqr.astype(BF16),
        eq=_placement(False), gkv=g_kv.reshape(1, -1),
        wkn=wkn.reshape(kv_lora, -1).astype(BF16), ek=_placement(True),
        wvt=wv.reshape(kv_lora, -1).T.astype(BF16), vone=jnp.asarray(vone.reshape(-1, 1)),
        woa=w_oa.astype(BF16), wob=w_ob.astype(BF16), woc=w_oc.astype(BF16), wout=w_out.astype(BF16),
        wp=w_pool.astype(BF16), sp=s_pool.reshape(1, -1),
    )


def _token_mixer(xs, modv1, modv2, lw, g1, g2, ropes, dft, batch, kv_ctx, tm):
    tabq, tabk = ropes
    q, k, vt, pool_in, fft_in, gate = proj_in(xs, modv1, g1, tabq, tabk, lw, batch, False, tm)
    if kv_ctx is None:
        attn = attention(q, k, vt, None, None, batch)
    else:
        attn = attention(q, kv_ctx[0], kv_ctx[1], k, vt, batch)
    pooled = pool_mixer(pool_in, lw["wp"], lw["sp"], batch)
    four = fourier_mixer(fft_in, batch, dft)
    x_new, h2 = merge(attn, pooled, four, gate, xs, modv2, lw, g2, batch, tm)
    return x_new, h2, (k, vt)


def kernel(x, c, ctx, c_ctx, w_mod, b_mod, g_norm1, w_in, b_gate, g_q, w_uq, g_kv, w_ukv, w_pool, s_pool,
           w_oa, w_ob, w_oc, w_out, g_norm2, ffn_w1, ffn_w3, ffn_w2, moe_router, moe_bias,
           moe_w1, moe_w3, moe_w2, g_final):
    batch, seq, d = x.shape
    n_ctx = ctx.shape[1]
    depth = w_mod.shape[0]
    n_exp = moe_router.shape[-1] if moe_router.shape[0] else 0
    xl = x.reshape(batch * seq, d)
    xc = ctx.reshape(batch * n_ctx, d)

    cvec = jnp.concatenate([c, c_ctx[None], jnp.zeros((8 - batch - 1, d), F32)], axis=0)
    mods = modulation(cvec, w_mod, b_mod)

    rope_l = _rope_inputs(seq, True)
    rope_c = _rope_inputs(n_ctx, False)
    dft_l = _dft_tables(seq)
    dft_c = _dft_tables(n_ctx)

    for layer in range(depth):
        last = layer == depth - 1
        lw = _prep_layer(w_in[layer], b_gate[layer], g_q[layer], w_uq[layer], g_kv[layer], w_ukv[layer],
                         w_oa[layer], w_ob[layer], w_oc[layer], w_out[layer], w_pool[layer], s_pool[layer])
        g1 = g_norm1[layer].reshape(1, d)
        g2 = g_norm2[layer].reshape(1, d)
        ml = mods[layer, :batch].reshape(batch, 6, d)
        mc = jnp.broadcast_to(mods[layer, batch].reshape(1, 6, d), (batch, 6, d))

        if last:
            k_c, vt_c = proj_in(xc, mc[:, 0:2], g1, rope_c[0], rope_c[1], lw, batch, True)
            kv_c = (k_c, vt_c)
        else:
            xc_new, h2_c, kv_c = _token_mixer(xc, mc[:, 0:2], mc[:, 2:5], lw, g1, g2, rope_c, dft_c,
                                              batch, None, 256)
        xl, h2_l, _ = _token_mixer(xl, ml[:, 0:2], ml[:, 2:5], lw, g1, g2, rope_l, dft_l, batch, kv_c, 512)
        if not last:
            xc = xc_new

        streams = [(xl, h2_l, ml)]
        if not last:
            streams.append((xc, h2_c, mc))
        fuse_final = last and layer % 2 == 1
        new = []
        for xs, h2, mm in streams:
            gt2 = mm[:, 5:6]
            li = layer // 2
            if layer % 2 == 0:
                xs = ffn(h2, xs, gt2, ffn_w1[li].astype(BF16), ffn_w3[li].astype(BF16),
                         ffn_w2[li].astype(BF16), batch)
            else:
                comb_t, pos_t, cum_incl = router(xs, mm[:, 3:5], g2, moe_router[li].T,
                                                 moe_bias[li].reshape(n_exp, 1), batch)
                y = moe(h2, comb_t, pos_t, cum_incl, moe_w1[li].astype(BF16), moe_w3[li].astype(BF16),
                        moe_w2[li].astype(BF16))
                xs = residual(xs, y, gt2, g_final.reshape(1, d), batch, fuse_final)
            new.append(xs)
        xl = new[0]
        if not last:
            xc = new[1]

    if not fuse_final:
        xl = final_norm(xl, g_final.reshape(1, d))
    return xl.reshape(batch, seq, d)
```

```python
import functools
import math

import numpy as np
import jax
import jax.numpy as jnp
from jax import lax
from jax.experimental import pallas as pl
from jax.experimental.pallas import tpu as pltpu

GRID_WIDTH = 64
NUM_HEADS = 8
NOPE_DIM = 64
ROPE_DIM = 32
VAL_DIM = 64
ROPE_THETA = 10000.0
POOL_WINS = (2, 4, 8, 16)
GROUP_CH = 128
NUM_BRANCH = 3
NORM_EPS = 1e-6

LANES = 128
HEAD_PAD = 128
VT_ROWS = 80
POOL_HALO = 16
DFT_N2 = 128
DFT_GROUP = 8
VMEM_LIMIT = 56 * 1024 * 1024
MOE_ROWS = 256
MOE_SUB = 256
MOE_CHUNK = 2048
MOE_WINDOW = 6
MOE_VMEM_LIMIT = 60 * 1024 * 1024
MOD_COLS = 1536
ATTN_EXACT_KEYS = 512
STALE_MAX_JUMP = 64.0

BF16 = jnp.bfloat16
F32 = jnp.float32


def _cparams(sem):
    return pltpu.CompilerParams(dimension_semantics=sem, vmem_limit_bytes=VMEM_LIMIT)


def _full(shape):
    n = len(shape)
    return pl.BlockSpec(shape, lambda *_: (0,) * n)


def _dot(a, b):
    return jnp.dot(a, b, preferred_element_type=F32)


def _dot_nt(a, b):
    return lax.dot_general(a, b, (((1,), (1,)), ((), ())), preferred_element_type=F32)


def _rms(x, g):
    return x * lax.rsqrt(jnp.mean(x * x, axis=-1, keepdims=True) + NORM_EPS) * g


def _mod_kernel(c_ref, w_ref, b_ref, o_ref):
    c = c_ref[...]
    s = c * jax.nn.sigmoid(c)
    w = w_ref[0]
    s_hi = s.astype(BF16)
    s_lo = (s - s_hi.astype(F32)).astype(BF16)
    w_hi = w.astype(BF16)
    w_lo = (w - w_hi.astype(F32)).astype(BF16)
    part = _dot(jnp.concatenate([s_hi, s_lo], axis=0), w_hi)
    rows = s.shape[0]
    o_ref[0] = part[:rows] + part[rows:] + _dot(s_hi, w_lo) + b_ref[0]


def modulation(cvec, w_mod, b_mod):
    depth, d, n = w_mod.shape
    tn = MOD_COLS
    return pl.pallas_call(
        _mod_kernel,
        grid=(depth, n // tn),
        in_specs=[pl.BlockSpec((8, d), lambda l, j: (0, 0)),
                  pl.BlockSpec((1, d, tn), lambda l, j: (l, 0, j)),
                  pl.BlockSpec((1, 1, tn), lambda l, j: (l, 0, j))],
        out_specs=pl.BlockSpec((1, 8, tn), lambda l, j: (l, 0, j)),
        out_shape=jax.ShapeDtypeStruct((depth, 8, n), F32),
        compiler_params=_cparams(("parallel", "parallel")),
        name="modulation",
    )(cvec, w_mod, b_mod.reshape(depth, 1, n))


def _proj_in_kernel(x_ref, mod_ref, g1_ref, tabq_ref, tabk_ref,
                    wa_ref, wpool_ref, wfft_ref, wgate_ref, bgate_ref,
                    gq_ref, wqn_ref, wqr_ref, eq_ref, gkv_ref, wkn_ref, ek_ref, wvt_ref, vone_ref,
                    *out_refs, q_lora, kv_lora, kv_only):
    if kv_only:
        k_ref, vt_ref = out_refs
    else:
        q_ref, k_ref, vt_ref, pool_ref, fft_ref, gate_ref = out_refs
    x = x_ref[...]
    mod = mod_ref[0]
    h = (_rms(x, g1_ref[...]) * (1.0 + mod[1:2]) + mod[0:1]).astype(BF16)
    pa = _dot(h, wa_ref[...])
    kvn = _rms(pa[:, q_lora:q_lora + kv_lora], gkv_ref[...]).astype(BF16)
    kr = (pa[:, q_lora + kv_lora:] * tabk_ref[...]).astype(BF16)
    k_all = (_dot(kvn, wkn_ref[...]) + _dot(kr, ek_ref[...])).astype(BF16)
    for hd in range(NUM_HEADS):
        k_ref[hd] = k_all[:, hd * HEAD_PAD:(hd + 1) * HEAD_PAD]
    vt_ref[0] = (_dot_nt(wvt_ref[...], kvn) + vone_ref[...]).astype(BF16)
    if kv_only:
        return
    qn = _rms(pa[:, :q_lora], gq_ref[...]).astype(BF16)
    qr = (_dot(qn, wqr_ref[...]) * tabq_ref[...]).astype(BF16)
    q_all = (_dot(qn, wqn_ref[...]) + _dot(qr, eq_ref[...])).astype(BF16)
    for hd in range(NUM_HEADS):
        q_ref[hd] = q_all[:, hd * HEAD_PAD:(hd + 1) * HEAD_PAD]
    pool_ref[...] = _dot(h, wpool_ref[...])
    fft_ref[...] = _dot(h, wfft_ref[...])
    gate_ref[...] = jax.nn.sigmoid(_dot(h, wgate_ref[...]) + bgate_ref[...]).astype(BF16)


def proj_in(x, modv, g1, tabq, tabk, lw, batch, kv_only, tm=256):
    t, d = x.shape
    seq = t // batch
    npb = seq // tm
    q_lora = lw["gq"].shape[1]
    kv_lora = lw["gkv"].shape[1]
    weights = [lw["wa"], lw["wpool"], lw["wfft"], lw["wgate"], lw["bgate"],
               lw["gq"], lw["wqn"], lw["wqr"], lw["eq"], lw["gkv"], lw["wkn"], lw["ek"],
               lw["wvt"], lw["vone"]]
    in_specs = [pl.BlockSpec((tm, d), lambda i: (i, 0)),
                pl.BlockSpec((1, 2, d), lambda i: (i // npb, 0, 0)),
                _full(g1.shape),
                pl.BlockSpec((tm, tabq.shape[1]), lambda i: (i % npb, 0)),
                pl.BlockSpec((tm, tabk.shape[1]), lambda i: (i % npb, 0))]
    in_specs += [_full(w.shape) for w in weights]
    head_spec = pl.BlockSpec((NUM_HEADS, tm, HEAD_PAD), lambda i: (0, i, 0))
    head_shape = jax.ShapeDtypeStruct((NUM_HEADS, t, HEAD_PAD), BF16)
    vt_spec = pl.BlockSpec((1, NUM_HEADS * VT_ROWS, tm), lambda i: (i // npb, 0, i % npb))
    vt_shape = jax.ShapeDtypeStruct((batch, NUM_HEADS * VT_ROWS, seq), BF16)
    if kv_only:
        out_specs = [head_spec, vt_spec]
        out_shape = [head_shape, vt_shape]
    else:
        pw, fw, gw = lw["wpool"].shape[1], lw["wfft"].shape[1], lw["wgate"].shape[1]
        out_specs = [head_spec, head_spec, vt_spec,
                     pl.BlockSpec((tm, pw), lambda i: (i, 0)),
                     pl.BlockSpec((tm, fw), lambda i: (i, 0)),
                     pl.BlockSpec((tm, gw), lambda i: (i, 0))]
        out_shape = [head_shape, head_shape, vt_shape,
                     jax.ShapeDtypeStruct((t, pw), F32),
                     jax.ShapeDtypeStruct((t, fw), F32),
                     jax.ShapeDtypeStruct((t, gw), BF16)]
    return pl.pallas_call(
        functools.partial(_proj_in_kernel, q_lora=q_lora, kv_lora=kv_lora, kv_only=kv_only),
        grid=(t // tm,),
        in_specs=in_specs, out_specs=out_specs, out_shape=out_shape,
        compiler_params=_cparams(("parallel",)),
        name="proj_in_kv" if kv_only else "proj_in",
    )(x, modv, g1, tabq, tabk, *weights)


def _attn_kernel(q_ref, kc_ref, vtc_ref, *rest, tk, tk_exact, n_lat_tiles):
    if n_lat_tiles:
        kl_ref, vtl_ref, o_ref = rest
    else:
        (o_ref,) = rest
    nh = q_ref.shape[0]

    def rows(hd):
        return slice(hd * VT_ROWS, (hd + 1) * VT_ROWS)

    def scores(ks):
        return tuple(_dot_nt(ks[hd], q_ref[hd]) for hd in range(nh))

    def update(ss, vts, state):
        out = []
        for hd in range(nh):
            smax = jnp.max(ss[hd], axis=0, keepdims=True)
            if state is None:
                m_new = smax
                acc = _dot(vts[hd], jnp.exp2(ss[hd] - m_new).astype(BF16))
            else:
                m_old, acc_old = state[hd]
                m_new = jnp.maximum(m_old, smax)
                pv = _dot(vts[hd], jnp.exp2(ss[hd] - m_new).astype(BF16))
                acc = acc_old * jnp.exp2(m_old - m_new) + pv
            out.append((m_new, acc))
        return tuple(out)

    def lat_k(off, n):
        return [kl_ref[hd, pl.ds(off, n), :] for hd in range(nh)]

    def lat_vt(off, n):
        return [vtl_ref[0, rows(hd), pl.ds(off, n)] for hd in range(nh)]

    def update_stale(ss, vts, state, jump):
        out = []
        for hd in range(nh):
            m_old, acc_old = state[hd]
            pv = _dot(vts[hd], jnp.exp2(ss[hd] - m_old).astype(BF16))
            smax = jnp.max(ss[hd], axis=0, keepdims=True)
            m_new = jnp.maximum(m_old, smax)
            out.append((m_new, (acc_old + pv) * jnp.exp2(m_old - m_new)))
            jump = jnp.maximum(jump, smax - m_old)
        return tuple(out), jump

    state = update(scores([kc_ref[hd] for hd in range(nh)]),
                   [vtc_ref[0, rows(hd), :] for hd in range(nh)], None)
    if n_lat_tiles:
        def fast_body(i, carry):
            off = pl.multiple_of(i * tk, tk)
            return update_stale(scores(lat_k(off, tk)), lat_vt(off, tk), *carry)
        fast, jump = lax.fori_loop(0, n_lat_tiles, fast_body,
                                   (state, jnp.zeros((1, q_ref.shape[1]), F32)))

        def exact_loop():
            def body(i, st):
                off = pl.multiple_of(i * tk_exact, tk_exact)
                return update(scores(lat_k(off, tk_exact)), lat_vt(off, tk_exact), st)
            return lax.fori_loop(0, n_lat_tiles * (tk // tk_exact), body, state)

        state = lax.cond(jnp.max(jump) > STALE_MAX_JUMP, exact_loop, lambda: fast)
    for pair in range(nh // 2):
        outs = []
        for hd in (2 * pair, 2 * pair + 1):
            acc = state[hd][1]
            outs.append(acc[:VAL_DIM] / acc[VAL_DIM:VAL_DIM + 1])
        o_ref[:, pair * 2 * VAL_DIM:(pair + 1) * 2 * VAL_DIM] = (
            jnp.concatenate(outs, axis=0).T.astype(BF16))


def attention(q, kc, vtc, kl, vtl, batch, tq=256, tk=2048):
    h, t, _ = q.shape
    lq = t // batch
    nq = lq // tq
    lc = kc.shape[1] // batch
    once = pl.Buffered(1)
    in_specs = [pl.BlockSpec((h, tq, HEAD_PAD), lambda b, i: (0, b * nq + i, 0)),
                pl.BlockSpec((h, lc, HEAD_PAD), lambda b, i: (0, b, 0), pipeline_mode=once),
                pl.BlockSpec((1, h * VT_ROWS, lc), lambda b, i: (b, 0, 0), pipeline_mode=once)]
    args = [q, kc, vtc]
    n_lat_tiles = 0
    if kl is not None:
        ll = kl.shape[1] // batch
        tk = min(tk, ll)
        n_lat_tiles = ll // tk
        in_specs += [pl.BlockSpec((h, ll, HEAD_PAD), lambda b, i: (0, b, 0), pipeline_mode=once),
                     pl.BlockSpec((1, h * VT_ROWS, ll), lambda b, i: (b, 0, 0), pipeline_mode=once)]
        args += [kl, vtl]
    return pl.pallas_call(
        functools.partial(_attn_kernel, tk=tk, tk_exact=min(tk, ATTN_EXACT_KEYS), n_lat_tiles=n_lat_tiles),
        grid=(batch, nq),
        in_specs=in_specs,
        out_specs=pl.BlockSpec((tq, h * VAL_DIM), lambda b, i: (b * nq + i, 0)),
        out_shape=jax.ShapeDtypeStruct((t, h * VAL_DIM), BF16),
        compiler_params=_cparams(("parallel", "arbitrary")),
        name="attention",
    )(*args)


def _pool_kernel(x_ref, prev_ref, next_ref, wp_ref, sp_ref, o_ref, ext_sc, *, seq, tm):
    i = pl.program_id(0)
    npb = seq // tm
    pos0 = (i % npb) * tm
    has_prev = (pos0 > 0).astype(F32)
    has_next = (pos0 + tm < seq).astype(F32)
    x = x_ref[...]
    ext_sc[:POOL_HALO] = prev_ref[...] * has_prev
    ext_sc[POOL_HALO:POOL_HALO + tm] = x
    ext_sc[POOL_HALO + tm:] = next_ref[...] * has_next
    t = pos0 + lax.broadcasted_iota(jnp.int32, (tm, GROUP_CH), 0)
    outs = []
    for g, win in enumerate(POOL_WINS):
        left = win // 2
        right = win - 1 - left
        cnt = (jnp.minimum(t + right + 1, seq) - jnp.maximum(t - left, 0)).astype(F32)
        sl = slice(g * GROUP_CH, (g + 1) * GROUP_CH)
        wsum = x[:, sl]
        for dlt in range(-left, right + 1):
            if dlt:
                wsum = wsum + ext_sc[POOL_HALO + dlt:POOL_HALO + dlt + tm, sl]
        mixed = (wsum / cnt - x[:, sl]).astype(BF16)
        outs.append(_dot(mixed, wp_ref[g]))
    o_ref[...] = (jnp.concatenate(outs, axis=1) * sp_ref[...]).astype(BF16)


def pool_mixer(p, wp, sp, batch, tm=256):
    t, w = p.shape
    seq = t // batch
    tm = min(tm, seq)
    hb = tm // POOL_HALO
    nh = t // POOL_HALO
    return pl.pallas_call(
        functools.partial(_pool_kernel, seq=seq, tm=tm),
        grid=(t // tm,),
        in_specs=[pl.BlockSpec((tm, w), lambda i: (i, 0)),
                  pl.BlockSpec((POOL_HALO, w), lambda i: (jnp.maximum(i * hb - 1, 0), 0)),
                  pl.BlockSpec((POOL_HALO, w), lambda i: (jnp.minimum((i + 1) * hb, nh - 1), 0)),
                  _full(wp.shape), _full(sp.shape)],
        out_specs=pl.BlockSpec((tm, w), lambda i: (i, 0)),
        out_shape=jax.ShapeDtypeStruct((t, w), BF16),
        scratch_shapes=[pltpu.VMEM((tm + 2 * POOL_HALO, w), F32)],
        compiler_params=_cparams(("parallel",)),
        name="pool_mixer",
    )(p, p, p, wp, sp)


def _dft_outer_kernel(f_ref, x_ref, o_ref):
    _, n1, grp, w = x_ref.shape
    out = _dot(f_ref[...], x_ref[0].reshape(n1 * grp, w).astype(BF16))
    o_ref[0] = out.reshape(o_ref.shape[1], grp, w)


def _dft_inner_kernel(a_ref, g_ref, cs_ref, o_ref, *, scale):
    n2 = g_ref.shape[1] // 2
    for k in range(g_ref.shape[0]):
        rhs = a_ref[0, :, k].reshape(2 * n2, a_ref.shape[-1]).astype(BF16)
        p = _dot(g_ref[k], rhs)
        outs = []
        for g in range(p.shape[1] // GROUP_CH):
            sl = slice(g * GROUP_CH, (g + 1) * GROUP_CH)
            lhs = jnp.concatenate([p[:n2, sl], p[n2:, sl]], axis=1).astype(BF16)
            outs.append(_dot(lhs, cs_ref[...]))
        o_ref[0, :, k, :] = jnp.concatenate(outs, axis=1) * scale


def _dft_tables(seq):
    n2 = seq if seq <= 2 * DFT_N2 else DFT_N2
    n1 = seq // n2
    k2 = jnp.arange(n2, dtype=jnp.int32)[:, None]
    t2 = jnp.arange(n2, dtype=jnp.int32)[None, :]
    k1 = jnp.arange(n1, dtype=jnp.int32)[:, None, None]
    ang_a = ((t2 * k2) % n2).astype(F32) * (2.0 * math.pi / n2)
    ang_b = ((t2 * k1) % seq).astype(F32) * (2.0 * math.pi / seq)
    ca, sa, cb, sb = jnp.cos(ang_a), jnp.sin(ang_a), jnp.cos(ang_b), jnp.sin(ang_b)
    gr = ca * cb - sa * sb
    gi = -(sa * cb + ca * sb)
    gmat = jnp.concatenate([jnp.concatenate([gr, -gi], axis=2),
                            jnp.concatenate([gi, gr], axis=2)], axis=1).astype(BF16)
    r = jnp.arange(2 * n1 * DFT_GROUP, dtype=jnp.int32)[:, None]
    q = jnp.arange(n1 * DFT_GROUP, dtype=jnp.int32)[None, :]
    ang1 = ((((r // DFT_GROUP) % n1) * (q // DFT_GROUP)) % n1).astype(F32) * (2.0 * math.pi / n1)
    f1 = jnp.where(r // DFT_GROUP >= n1, -jnp.sin(ang1), jnp.cos(ang1))
    f1 = jnp.where(r % DFT_GROUP == q % DFT_GROUP, f1, 0.0).astype(BF16)
    c = jnp.arange(GROUP_CH, dtype=jnp.int32)
    angc = ((c[:, None] * c[None, :]) % GROUP_CH).astype(F32) * (2.0 * math.pi / GROUP_CH)
    cs = jnp.concatenate([jnp.cos(angc), jnp.sin(angc)], axis=0).astype(BF16)
    return gmat, f1, cs


def fourier_mixer(f, batch, tables):
    gmat, f1, cs = tables
    t, w = f.shape
    seq = t // batch
    n1, n2x2, _ = gmat.shape
    n2 = n2x2 // 2
    scale = 1.0 / math.sqrt(seq * GROUP_CH)
    if n1 > 1:
        a = pl.pallas_call(
            _dft_outer_kernel,
            grid=(batch, n2 // DFT_GROUP),
            in_specs=[_full(f1.shape), pl.BlockSpec((1, n1, DFT_GROUP, w), lambda b, j: (b, 0, j, 0))],
            out_specs=pl.BlockSpec((1, 2 * n1, DFT_GROUP, w), lambda b, j: (b, 0, j, 0)),
            out_shape=jax.ShapeDtypeStruct((batch, 2 * n1, n2, w), F32),
            compiler_params=_cparams(("parallel", "parallel")),
            name="dft_outer",
        )(f1, f.reshape(batch, n1, n2, w))
        a = a.reshape(batch, 2, n1, n2, w)
    else:
        fr = f.reshape(batch, 1, 1, n2, w)
        a = jnp.concatenate([fr, jnp.zeros_like(fr)], axis=1)
    kg = min(DFT_GROUP, n1)
    out = pl.pallas_call(
        functools.partial(_dft_inner_kernel, scale=scale),
        grid=(batch, n1 // kg),
        in_specs=[pl.BlockSpec((1, 2, kg, n2, w), lambda b, k: (b, 0, k, 0, 0)),
                  pl.BlockSpec((kg, 2 * n2, 2 * n2), lambda b, k: (k, 0, 0)),
                  _full(cs.shape)],
        out_specs=pl.BlockSpec((1, n2, kg, w), lambda b, k: (b, 0, k, 0)),
        out_shape=jax.ShapeDtypeStruct((batch, n2, n1, w), F32),
        compiler_params=_cparams(("parallel", "parallel")),
        name="dft_inner",
    )(a, gmat, cs)
    return out.reshape(t, w)


def _merge_kernel(a_ref, p_ref, f_ref, g_ref, x_ref, mod_ref, woa_ref, wob_ref, woc_ref, wout_ref,
                  g2_ref, xo_ref, h_ref):
    d = x_ref.shape[1]
    gate = g_ref[...].astype(F32)
    y = (gate[:, :d] * _dot(a_ref[...], woa_ref[...])
         + gate[:, d:2 * d] * _dot(p_ref[...], wob_ref[...])
         + gate[:, 2 * d:] * _dot(f_ref[...].astype(BF16), woc_ref[...]))
    mod = mod_ref[0]
    xn = x_ref[...] + mod[0:1] * _dot(y.astype(BF16), wout_ref[...])
    xo_ref[...] = xn
    h_ref[...] = (_rms(xn, g2_ref[...]) * (1.0 + mod[2:3]) + mod[1:2]).astype(BF16)


def merge(attn, pooled, four, gate, x, modv, lw, g2, batch, tm=256):
    t, d = x.shape
    npb = (t // batch) // tm
    row = lambda w: pl.BlockSpec((tm, w), lambda i: (i, 0))
    weights = [lw["woa"], lw["wob"], lw["woc"], lw["wout"], g2]
    return pl.pallas_call(
        _merge_kernel,
        grid=(t // tm,),
        in_specs=[row(attn.shape[1]), row(pooled.shape[1]), row(four.shape[1]), row(gate.shape[1]),
                  row(d), pl.BlockSpec((1, 3, d), lambda i: (i // npb, 0, 0))]
                 + [_full(w.shape) for w in weights],
        out_specs=[row(d), row(d)],
        out_shape=[jax.ShapeDtypeStruct((t, d), F32), jax.ShapeDtypeStruct((t, d), BF16)],
        compiler_params=_cparams(("parallel",)),
        name="merge",
    )(attn, pooled, four, gate, x, modv, *weights)


def _ffn_kernel(h_ref, x_ref, gt_ref, w1_ref, w3_ref, w2_ref, o_ref):
    h = h_ref[...]
    a = _dot(h, w1_ref[...])
    b = _dot(h, w3_ref[...])
    g = (a * jax.nn.sigmoid(a) * b).astype(BF16)
    o_ref[...] = x_ref[...] + gt_ref[0] * _dot(g, w2_ref[...])


def ffn(h, x, gt, w1, w3, w2, batch, tm=256):
    t, d = x.shape
    npb = (t // batch) // tm
    return pl.pallas_call(
        _ffn_kernel,
        grid=(t // tm,),
        in_specs=[pl.BlockSpec((tm, d), lambda i: (i, 0)),
                  pl.BlockSpec((tm, d), lambda i: (i, 0)),
                  pl.BlockSpec((1, 1, d), lambda i: (i // npb, 0, 0)),
                  _full(w1.shape), _full(w3.shape), _full(w2.shape)],
        out_specs=pl.BlockSpec((tm, d), lambda i: (i, 0)),
        out_shape=jax.ShapeDtypeStruct((t, d), F32),
        compiler_params=_cparams(("parallel",)),
        name="ffn",
    )(h, x, gt, w1, w3, w2)


def _router_kernel(x_ref, mod_ref, g2_ref, wr_ref, br_ref, tri_ref, comb_ref, pos_ref, cum_ref, cnt_sc):
    @pl.when(pl.program_id(0) == 0)
    def _():
        cnt_sc[...] = jnp.zeros_like(cnt_sc)

    mod = mod_ref[0]
    h = _rms(x_ref[...], g2_ref[...]) * (1.0 + mod[1:2]) + mod[0:1]
    h_hi = h.astype(BF16)
    h_lo = (h - h_hi.astype(F32)).astype(BF16)
    w = wr_ref[...]
    w_hi = w.astype(BF16)
    w_lo = (w - w_hi.astype(F32)).astype(BF16)
    ne = w.shape[0]
    part = _dot_nt(jnp.concatenate([w_hi, w_lo], axis=0), h_hi)
    logits = part[:ne] + part[ne:] + _dot_nt(w_hi, h_lo) + br_ref[...]
    ne = logits.shape[0]
    eidx = lax.broadcasted_iota(jnp.int32, logits.shape, 0)
    m1 = jnp.max(logits, axis=0, keepdims=True)
    i1 = jnp.min(jnp.where(logits == m1, eidx, ne), axis=0, keepdims=True)
    sel1 = eidx == i1
    rest = jnp.where(sel1, -jnp.inf, logits)
    m2 = jnp.max(rest, axis=0, keepdims=True)
    i2 = jnp.min(jnp.where(rest == m2, eidx, ne), axis=0, keepdims=True)
    sel2 = eidx == i2
    e2 = jnp.exp(m2 - m1)
    w1 = 1.0 / (1.0 + e2)
    comb_ref[...] = jnp.where(sel1, w1, 0.0) + jnp.where(sel2, e2 * w1, 0.0)
    self = jnp.where(sel1 | sel2, 1.0, 0.0)
    before = cnt_sc[...][:, 0:1]
    rank = before + _dot(self.astype(BF16), tri_ref[...])
    pos_ref[...] = jnp.where(self > 0.0, rank, -1.0).astype(jnp.int32)
    after = before + jnp.sum(self, axis=1, keepdims=True)
    cnt_sc[...] = jnp.broadcast_to(after, cnt_sc.shape)
    cum_ref[...] = jnp.broadcast_to(after, cum_ref.shape).astype(jnp.int32)


def router(x, modv, g2, wr_t, br, batch, tm=256):
    t, d = x.shape
    ne = wr_t.shape[0]
    npb = (t // batch) // tm
    nt = t // tm
    tri = jnp.asarray(np.triu(np.ones((tm, tm), np.float32), 1), dtype=BF16)
    return pl.pallas_call(
        _router_kernel,
        grid=(nt,),
        in_specs=[pl.BlockSpec((tm, d), lambda i: (i, 0)),
                  pl.BlockSpec((1, 2, d), lambda i: (i // npb, 0, 0)),
                  _full(g2.shape), _full(wr_t.shape), _full(br.shape), _full(tri.shape)],
        out_specs=[pl.BlockSpec((ne, tm), lambda i: (0, i)),
                   pl.BlockSpec((ne, tm), lambda i: (0, i)),
                   pl.BlockSpec((ne, LANES), lambda i: (0, i))],
        out_shape=[jax.ShapeDtypeStruct((ne, t), F32), jax.ShapeDtypeStruct((ne, t), jnp.int32),
                   jax.ShapeDtypeStruct((ne, nt * LANES), jnp.int32)],
        scratch_shapes=[pltpu.VMEM((ne, LANES), F32)],
        compiler_params=_cparams(("arbitrary",)),
        name="router",
    )(x, modv, g2, wr_t, br, tri)


def _moe_kernel(cum_ref, h_ref, posr_ref, pc_ref, *rest, nsub, cum0, first, col):
    if first:
        w1_ref, w3_ref, w2_ref, o_ref, hc_sc, ys_sc = rest
    else:
        yin_ref, w1_ref, w3_ref, w2_ref, o_ref, hc_sc, ys_sc = rest
    c = pl.program_id(0)

    def cum(i):
        return cum_ref[cum0 + c * nsub + i]

    base = cum(0)
    total = cum(nsub) - base
    nfull = total // MOE_ROWS
    rem = total - nfull * MOE_ROWS
    nb = nfull + (rem > 0).astype(jnp.int32)
    nwin = min(MOE_WINDOW, nsub)

    def expert_block(j, nrows):
        off = base + j * MOE_ROWS
        s_lo = sum((cum(s + 1) <= off).astype(jnp.int32) for s in range(nsub))
        s_hi = sum((cum(s) < off + nrows).astype(jnp.int32) for s in range(nsub))

        def compact(win, width):
            iota = lax.broadcasted_iota(jnp.int32, (nrows, width), 0)
            sel = jnp.where(posr_ref[0, :, win] - off == iota, 1.0, 0.0).astype(BF16)
            hc_sc[:nrows] = _dot(sel, h_ref[win, :]).astype(BF16)

        @pl.when(s_hi - s_lo <= nwin)
        def _():
            start = pl.multiple_of(jnp.minimum(s_lo, nsub - nwin) * MOE_SUB, MOE_SUB)
            compact(pl.ds(start, nwin * MOE_SUB), nwin * MOE_SUB)

        if nwin < nsub:
            @pl.when(s_hi - s_lo > nwin)
            def _():
                compact(slice(None), nsub * MOE_SUB)

        hc = hc_sc[:nrows]
        a = _dot(hc, w1_ref[0])
        b = _dot(hc, w3_ref[0])
        g = (a * jax.nn.sigmoid(a) * b).astype(BF16)
        ys_sc[j, :nrows] = _dot(g, w2_ref[0]).astype(BF16)
        if nrows < MOE_ROWS:
            ys_sc[j, nrows:] = jnp.zeros((MOE_ROWS - nrows, ys_sc.shape[2]), BF16)

    def full_block(j, carry):
        expert_block(j, MOE_ROWS)
        return carry

    lax.fori_loop(0, nfull, full_block, 0)

    @pl.when(rem > MOE_ROWS // 2)
    def _():
        expert_block(nfull, MOE_ROWS)

    @pl.when((rem > 0) & (rem <= MOE_ROWS // 2))
    def _():
        expert_block(nfull, MOE_ROWS // 2)

    ys_sc[nb] = jnp.zeros(ys_sc.shape[1:], BF16)
    ys_sc[nb + 1] = jnp.zeros(ys_sc.shape[1:], BF16)
    ne = pc_ref.shape[1] // 2
    liota = lax.broadcasted_iota(jnp.int32, (MOE_SUB, 2 * MOE_ROWS), 1).astype(F32)
    for s in range(nsub):
        sub = slice(s * MOE_SUB, (s + 1) * MOE_SUB)
        pos = pc_ref[sub, col:col + 1]
        j_lo = jnp.minimum((cum(s) - base) // MOE_ROWS, nb)
        off = (base + j_lo * MOE_ROWS).astype(F32)
        sel = jnp.where(pos - off == liota, 1.0, 0.0).astype(BF16)
        rows = ys_sc[pl.ds(j_lo, 2)].reshape(2 * MOE_ROWS, ys_sc.shape[2])
        y = pc_ref[sub, ne + col:ne + col + 1] * _dot(sel, rows)
        if not first:
            y = yin_ref[sub, :].astype(F32) + y
        o_ref[sub, :] = y.astype(BF16)


def moe(h, comb_t, pos_t, cum_incl, w1, w3, w2):
    t, d = h.shape
    ne = comb_t.shape[0]
    tc = min(MOE_CHUNK, t)
    assert t % tc == 0 and tc % MOE_SUB == 0
    nsub = tc // MOE_SUB
    nt = t // MOE_SUB
    cum = jnp.concatenate([jnp.zeros((ne, 1), jnp.int32), cum_incl[:, ::LANES]], axis=1)
    cum = cum.reshape(-1)
    posr = pos_t.reshape(ne, 1, t)
    pc = jnp.concatenate([pos_t.astype(F32).T, comb_t.T], axis=1)
    once = pl.Buffered(1)
    y = None
    for e in range(ne):
        first = y is None
        chunk = pl.BlockSpec((tc, d), lambda c, cum: (c, 0))
        in_specs = [chunk,
                    pl.BlockSpec((1, 1, tc), lambda c, cum, e=e: (e, 0, c)),
                    pl.BlockSpec((tc, 2 * ne), lambda c, cum: (c, 0))]
        in_specs += [] if first else [chunk]
        in_specs += [pl.BlockSpec((1,) + w.shape[1:], lambda c, cum, e=e: (e, 0, 0), pipeline_mode=once)
                     for w in (w1, w3, w2)]
        grid_spec = pltpu.PrefetchScalarGridSpec(
            num_scalar_prefetch=1, grid=(t // tc,), in_specs=in_specs, out_specs=chunk,
            scratch_shapes=[pltpu.VMEM((MOE_ROWS, d), BF16),
                            pltpu.VMEM((tc // MOE_ROWS + 2, MOE_ROWS, d), BF16)])
        y = pl.pallas_call(
            functools.partial(_moe_kernel, nsub=nsub, cum0=e * (nt + 1), first=first, col=e),
            grid_spec=grid_spec,
            out_shape=jax.ShapeDtypeStruct((t, d), BF16),
            input_output_aliases={} if first else {4: 0},
            compiler_params=pltpu.CompilerParams(dimension_semantics=("arbitrary",),
                                                 vmem_limit_bytes=MOE_VMEM_LIMIT),
            name="moe_expert",
        )(*([cum, h, posr, pc] + ([] if first else [y]) + [w1, w3, w2]))
    return y


def _residual_kernel(x_ref, y_ref, gt_ref, g_ref, o_ref, *, norm):
    x = x_ref[...] + gt_ref[0] * y_ref[...].astype(F32)
    o_ref[...] = _rms(x, g_ref[...]) if norm else x


def residual(x, y, gt, g, batch, norm, tm=512):
    t, d = x.shape
    tm = min(tm, t // batch)
    npb = (t // batch) // tm
    return pl.pallas_call(
        functools.partial(_residual_kernel, norm=norm),
        grid=(t // tm,),
        in_specs=[pl.BlockSpec((tm, d), lambda i: (i, 0)), pl.BlockSpec((tm, d), lambda i: (i, 0)),
                  pl.BlockSpec((1, 1, d), lambda i: (i // npb, 0, 0)), _full(g.shape)],
        out_specs=pl.BlockSpec((tm, d), lambda i: (i, 0)),
        out_shape=jax.ShapeDtypeStruct((t, d), F32),
        compiler_params=_cparams(("parallel",)),
        name="residual_norm" if norm else "residual",
    )(x, y, gt, g)


def _final_norm_kernel(x_ref, g_ref, o_ref):
    o_ref[...] = _rms(x_ref[...], g_ref[...])


def final_norm(x, g, tm=512):
    t, d = x.shape
    return pl.pallas_call(
        _final_norm_kernel,
        grid=(t // tm,),
        in_specs=[pl.BlockSpec((tm, d), lambda i: (i, 0)), _full(g.shape)],
        out_specs=pl.BlockSpec((tm, d), lambda i: (i, 0)),
        out_shape=jax.ShapeDtypeStruct((t, d), F32),
        compiler_params=_cparams(("parallel",)),
        name="final_norm",
    )(x, g)


def _rope_tables(seq, rotate):
    half = ROPE_DIM // 2
    nf = ROPE_DIM // 4
    if rotate:
        t = jnp.arange(seq, dtype=jnp.int32)
        rows = (t // GRID_WIDTH).astype(F32)
        cols = (t % GRID_WIDTH).astype(F32)
        inv = ROPE_THETA ** (-jnp.arange(0, half, 2, dtype=F32) / half)
        ang = jnp.stack([rows[:, None] * inv, cols[:, None] * inv], axis=1)
        cos, sin = jnp.cos(ang), jnp.sin(ang)
    else:
        cos = jnp.ones((seq, 2, nf), F32)
        sin = jnp.zeros((seq, 2, nf), F32)
    cpat = jnp.concatenate([cos, cos], axis=2).reshape(seq, ROPE_DIM)
    spat = jnp.concatenate([-sin, sin], axis=2).reshape(seq, ROPE_DIM)
    return cpat, spat


def _rope_inputs(seq, rotate):
    cpat, spat = _rope_tables(seq, rotate)
    tabq = jnp.concatenate([jnp.tile(cpat, (1, NUM_HEADS)), jnp.tile(spat, (1, NUM_HEADS))], axis=1)
    tabk = jnp.concatenate([cpat, spat, jnp.zeros((seq, LANES - 2 * ROPE_DIM), F32)], axis=1)
    return tabq, tabk


def _swap_perm():
    j = np.arange(ROPE_DIM)
    half = ROPE_DIM // 4
    return np.where((j % (2 * half)) < half, j + half, j - half)


def _placement(shared):
    nsrc = ROPE_DIM if shared else NUM_HEADS * ROPE_DIM
    e = np.zeros((LANES if shared else 2 * nsrc, NUM_HEADS * HEAD_PAD), np.float32)
    for hd in range(NUM_HEADS):
        for j in range(ROPE_DIM):
            col = hd * HEAD_PAD + NOPE_DIM + j
            src = j if shared else hd * ROPE_DIM + j
            e[src, col] = 1.0
            e[nsrc + src, col] = 1.0
    return jnp.asarray(e, dtype=BF16)


def _prep_layer(w_in, b_gate, g_q, w_uq, g_kv, w_ukv, w_oa, w_ob, w_oc, w_out, w_pool, s_pool):
    d = w_in.shape[0]
    q_lora, kv_lora = g_q.shape[0], g_kv.shape[0]
    perm = _swap_perm()
    off_kr = q_lora + kv_lora
    off_pool = off_kr + ROPE_DIM
    pool_w = w_pool.shape[0] * w_pool.shape[1]
    off_fft = off_pool + pool_w
    off_gate = w_in.shape[1] - NUM_BRANCH * d
    kr = w_in[:, off_kr:off_pool]
    wa = jnp.concatenate([w_in[:, :off_kr], kr, kr[:, perm],
                          jnp.zeros((d, LANES - 2 * ROPE_DIM), F32)], axis=1)
    qscale = (NOPE_DIM + ROPE_DIM) ** -0.5 * math.log2(math.e)
    wq = w_uq * qscale
    wqn = jnp.pad(wq[:, :, :NOPE_DIM], ((0, 0), (0, 0), (0, HEAD_PAD - NOPE_DIM)))
    wqr = wq[:, :, NOPE_DIM:]
    wqr = jnp.concatenate([wqr.reshape(q_lora, -1), wqr[:, :, perm].reshape(q_lora, -1)], axis=1)
    wkn = jnp.pad(w_ukv[:, :, :NOPE_DIM], ((0, 0), (0, 0), (0, HEAD_PAD - NOPE_DIM)))
    wv = jnp.pad(w_ukv[:, :, NOPE_DIM:], ((0, 0), (0, 0), (0, VT_ROWS - VAL_DIM)))
    vone = np.zeros((NUM_HEADS, VT_ROWS, 1), np.float32)
    vone[:, VAL_DIM] = 1.0
    return dict(
        wa=wa.astype(BF16), wpool=w_in[:, off_pool:off_fft].astype(BF16),
        wfft=w_in[:, off_fft:off_gate].astype(BF16), wgate=w_in[:, off_gate:].astype(BF16),
        bgate=b_gate.reshape(1, -1),
        gq=g_q.reshape(1, -1), wqn=wqn.reshape(q_lora, -1).astype(BF16), wqr=wqr.astype(BF16),
        eq=_placement(False), gkv=g_kv.reshape(1, -1),
        wkn=wkn.reshape(kv_lora, -1).astype(BF16), ek=_placement(True),
        wvt=wv.reshape(kv_lora, -1).T.astype(BF16), vone=jnp.asarray(vone.reshape(-1, 1)),
        woa=w_oa.astype(BF16), wob=w_ob.astype(BF16), woc=w_oc.astype(BF16), wout=w_out.astype(BF16),
        wp=w_pool.astype(BF16), sp=s_pool.reshape(1, -1),
    )


def _token_mixer(xs, modv1, modv2, lw, g1, g2, ropes, dft, batch, kv_ctx, tm):
    tabq, tabk = ropes
    q, k, vt, pool_in, fft_in, gate = proj_in(xs, modv1, g1, tabq, tabk, lw, batch, False, tm)
    if kv_ctx is None:
        attn = attention(q, k, vt, None, None, batch)
    else:
        attn = attention(q, kv_ctx[0], kv_ctx[1], k, vt, batch)
    pooled = pool_mixer(pool_in, lw["wp"], lw["sp"], batch)
    four = fourier_mixer(fft_in, batch, dft)
    x_new, h2 = merge(attn, pooled, four, gate, xs, modv2, lw, g2, batch, tm)
    return x_new, h2, (k, vt)


def kernel(x, c, ctx, c_ctx, w_mod, b_mod, g_norm1, w_in, b_gate, g_q, w_uq, g_kv, w_ukv, w_pool, s_pool,
           w_oa, w_ob, w_oc, w_out, g_norm2, ffn_w1, ffn_w3, ffn_w2, moe_router, moe_bias,
           moe_w1, moe_w3, moe_w2, g_final):
    batch, seq, d = x.shape
    n_ctx = ctx.shape[1]
    depth = w_mod.shape[0]
    n_exp = moe_router.shape[-1] if moe_router.shape[0] else 0
    xl = x.reshape(batch * seq, d)
    xc = ctx.reshape(batch * n_ctx, d)

    cvec = jnp.concatenate([c, c_ctx[None], jnp.zeros((8 - batch - 1, d), F32)], axis=0)
    mods = modulation(cvec, w_mod, b_mod)

    rope_l = _rope_inputs(seq, True)
    rope_c = _rope_inputs(n_ctx, False)
    dft_l = _dft_tables(seq)
    dft_c = _dft_tables(n_ctx)

    for layer in range(depth):
        last = layer == depth - 1
        lw = _prep_layer(w_in[layer], b_gate[layer], g_q[layer], w_uq[layer], g_kv[layer], w_ukv[layer],
                         w_oa[layer], w_ob[layer], w_oc[layer], w_out[layer], w_pool[layer], s_pool[layer])
        g1 = g_norm1[layer].reshape(1, d)
        g2 = g_norm2[layer].reshape(1, d)
        ml = mods[layer, :batch].reshape(batch, 6, d)
        mc = jnp.broadcast_to(mods[layer, batch].reshape(1, 6, d), (batch, 6, d))

        if last:
            k_c, vt_c = proj_in(xc, mc[:, 0:2], g1, rope_c[0], rope_c[1], lw, batch, True)
            kv_c = (k_c, vt_c)
        else:
            xc_new, h2_c, kv_c = _token_mixer(xc, mc[:, 0:2], mc[:, 2:5], lw, g1, g2, rope_c, dft_c,
                                              batch, None, 256)
        xl, h2_l, _ = _token_mixer(xl, ml[:, 0:2], ml[:, 2:5], lw, g1, g2, rope_l, dft_l, batch, kv_c, 512)
        if not last:
            xc = xc_new

        streams = [(xl, h2_l, ml)]
        if not last:
            streams.append((xc, h2_c, mc))
        fuse_final = last and layer % 2 == 1
        new = []
        for xs, h2, mm in streams:
            gt2 = mm[:, 5:6]
            li = layer // 2
            if layer % 2 == 0:
                xs = ffn(h2, xs, gt2, ffn_w1[li].astype(BF16), ffn_w3[li].astype(BF16),
                         ffn_w2[li].astype(BF16), batch)
            else:
                comb_t, pos_t, cum_incl = router(xs, mm[:, 3:5], g2, moe_router[li].T,
                                                 moe_bias[li].reshape(n_exp, 1), batch)
                y = moe(h2, comb_t, pos_t, cum_incl, moe_w1[li].astype(BF16), moe_w3[li].astype(BF16),
                        moe_w2[li].astype(BF16))
                xs = residual(xs, y, gt2, g_final.reshape(1, d), batch, fuse_final)
            new.append(xs)
        xl = new[0]
        if not last:
            xc = new[1]

    if not fuse_final:
        xl = final_norm(xl, g_final.reshape(1, d))
    return xl.reshape(batch, seq, d)
```

```python
import functools
import math

import numpy as np
import jax
import jax.numpy as jnp
from jax import lax
from jax.experimental import pallas as pl
from jax.experimental.pallas import tpu as pltpu

GRID_WIDTH = 64
NUM_HEADS = 8
NOPE_DIM = 64
ROPE_DIM = 32
VAL_DIM = 64
ROPE_THETA = 10000.0
POOL_WINS = (2, 4, 8, 16)
GROUP_CH = 128
NUM_BRANCH = 3
NORM_EPS = 1e-6

LANES = 128
HEAD_PAD = 128
VT_ROWS = 80
POOL_HALO = 16
DFT_N2 = 128
DFT_GROUP = 8
VMEM_LIMIT = 56 * 1024 * 1024
MOE_ROWS = 256
MOE_SUB = 256
MOE_CHUNK = 2048
MOE_WINDOW = 6
MOE_VMEM_LIMIT = 60 * 1024 * 1024
MOD_COLS = 1536
ATTN_EXACT_KEYS = 512
STALE_MAX_JUMP = 64.0

BF16 = jnp.bfloat16
F32 = jnp.float32


def _cparams(sem):
    return pltpu.CompilerParams(dimension_semantics=sem, vmem_limit_bytes=VMEM_LIMIT)


def _full(shape):
    n = len(shape)
    return pl.BlockSpec(shape, lambda *_: (0,) * n)


def _dot(a, b):
    return jnp.dot(a, b, preferred_element_type=F32)


def _dot_nt(a, b):
    return lax.dot_general(a, b, (((1,), (1,)), ((), ())), preferred_element_type=F32)


def _rms(x, g):
    return x * lax.rsqrt(jnp.mean(x * x, axis=-1, keepdims=True) + NORM_EPS) * g


def _mod_kernel(c_ref, w_ref, b_ref, o_ref):
    c = c_ref[...]
    s = c * jax.nn.sigmoid(c)
    w = w_ref[0]
    s_hi = s.astype(BF16)
    s_lo = (s - s_hi.astype(F32)).astype(BF16)
    w_hi = w.astype(BF16)
    w_lo = (w - w_hi.astype(F32)).astype(BF16)
    part = _dot(jnp.concatenate([s_hi, s_lo], axis=0), w_hi)
    rows = s.shape[0]
    o_ref[0] = part[:rows] + part[rows:] + _dot(s_hi, w_lo) + b_ref[0]


def modulation(cvec, w_mod, b_mod):
    depth, d, n = w_mod.shape
    tn = MOD_COLS
    return pl.pallas_call(
        _mod_kernel,
        grid=(depth, n // tn),
        in_specs=[pl.BlockSpec((8, d), lambda l, j: (0, 0)),
                  pl.BlockSpec((1, d, tn), lambda l, j: (l, 0, j)),
                  pl.BlockSpec((1, 1, tn), lambda l, j: (l, 0, j))],
        out_specs=pl.BlockSpec((1, 8, tn), lambda l, j: (l, 0, j)),
        out_shape=jax.ShapeDtypeStruct((depth, 8, n), F32),
        compiler_params=_cparams(("parallel", "parallel")),
        name="modulation",
    )(cvec, w_mod, b_mod.reshape(depth, 1, n))


def _proj_in_kernel(x_ref, mod_ref, g1_ref, tabq_ref, tabk_ref,
                    wa_ref, wpool_ref, wfft_ref, wgate_ref, bgate_ref,
                    gq_ref, wqn_ref, wqr_ref, eq_ref, gkv_ref, wkn_ref, ek_ref, wvt_ref, vone_ref,
                    *out_refs, q_lora, kv_lora, kv_only):
    if kv_only:
        k_ref, vt_ref = out_refs
    else:
        q_ref, k_ref, vt_ref, pool_ref, fft_ref, gate_ref = out_refs
    x = x_ref[...]
    mod = mod_ref[0]
    h = (_rms(x, g1_ref[...]) * (1.0 + mod[1:2]) + mod[0:1]).astype(BF16)
    pa = _dot(h, wa_ref[...])
    kvn = _rms(pa[:, q_lora:q_lora + kv_lora], gkv_ref[...]).astype(BF16)
    kr = (pa[:, q_lora + kv_lora:] * tabk_ref[...]).astype(BF16)
    k_all = (_dot(kvn, wkn_ref[...]) + _dot(kr, ek_ref[...])).astype(BF16)
    for hd in range(NUM_HEADS):
        k_ref[hd] = k_all[:, hd * HEAD_PAD:(hd + 1) * HEAD_PAD]
    vt_ref[0] = (_dot_nt(wvt_ref[...], kvn) + vone_ref[...]).astype(BF16)
    if kv_only:
        return
    qn = _rms(pa[:, :q_lora], gq_ref[...]).astype(BF16)
    qr = (_dot(qn, wqr_ref[...]) * tabq_ref[...]).astype(BF16)
    q_all = (_dot(qn, wqn_ref[...]) + _dot(qr, eq_ref[...])).astype(BF16)
    for hd in range(NUM_HEADS):
        q_ref[hd] = q_all[:, hd * HEAD_PAD:(hd + 1) * HEAD_PAD]
    pool_ref[...] = _dot(h, wpool_ref[...])
    fft_ref[...] = _dot(h, wfft_ref[...])
    gate_ref[...] = jax.nn.sigmoid(_dot(h, wgate_ref[...]) + bgate_ref[...]).astype(BF16)


def proj_in(x, modv, g1, tabq, tabk, lw, batch, kv_only, tm=256):
    t, d = x.shape
    seq = t // batch
    npb = seq // tm
    q_lora = lw["gq"].shape[1]
    kv_lora = lw["gkv"].shape[1]
    weights = [lw["wa"], lw["wpool"], lw["wfft"], lw["wgate"], lw["bgate"],
               lw["gq"], lw["wqn"], lw["wqr"], lw["eq"], lw["gkv"], lw["wkn"], lw["ek"],
               lw["wvt"], lw["vone"]]
    in_specs = [pl.BlockSpec((tm, d), lambda i: (i, 0)),
                pl.BlockSpec((1, 2, d), lambda i: (i // npb, 0, 0)),
                _full(g1.shape),
                pl.BlockSpec((tm, tabq.shape[1]), lambda i: (i % npb, 0)),
                pl.BlockSpec((tm, tabk.shape[1]), lambda i: (i % npb, 0))]
    in_specs += [_full(w.shape) for w in weights]
    head_spec = pl.BlockSpec((NUM_HEADS, tm, HEAD_PAD), lambda i: (0, i, 0))
    head_shape = jax.ShapeDtypeStruct((NUM_HEADS, t, HEAD_PAD), BF16)
    vt_spec = pl.BlockSpec((1, NUM_HEADS * VT_ROWS, tm), lambda i: (i // npb, 0, i % npb))
    vt_shape = jax.ShapeDtypeStruct((batch, NUM_HEADS * VT_ROWS, seq), BF16)
    if kv_only:
        out_specs = [head_spec, vt_spec]
        out_shape = [head_shape, vt_shape]
    else:
        pw, fw, gw = lw["wpool"].shape[1], lw["wfft"].shape[1], lw["wgate"].shape[1]
        out_specs = [head_spec, head_spec, vt_spec,
                     pl.BlockSpec((tm, pw), lambda i: (i, 0)),
                     pl.BlockSpec((tm, fw), lambda i: (i, 0)),
                     pl.BlockSpec((tm, gw), lambda i: (i, 0))]
        out_shape = [head_shape, head_shape, vt_shape,
                     jax.ShapeDtypeStruct((t, pw), F32),
                     jax.ShapeDtypeStruct((t, fw), F32),
                     jax.ShapeDtypeStruct((t, gw), BF16)]
    return pl.pallas_call(
        functools.partial(_proj_in_kernel, q_lora=q_lora, kv_lora=kv_lora, kv_only=kv_only),
        grid=(t // tm,),
        in_specs=in_specs, out_specs=out_specs, out_shape=out_shape,
        compiler_params=_cparams(("parallel",)),
        name="proj_in_kv" if kv_only else "proj_in",
    )(x, modv, g1, tabq, tabk, *weights)


def _attn_kernel(q_ref, kc_ref, vtc_ref, *rest, tk, tk_exact, n_lat_tiles):
    if n_lat_tiles:
        kl_ref, vtl_ref, o_ref = rest
    else:
        (o_ref,) = rest
    nh = q_ref.shape[0]

    def rows(hd):
        return slice(hd * VT_ROWS, (hd + 1) * VT_ROWS)

    def scores(ks):
        return tuple(_dot_nt(ks[hd], q_ref[hd]) for hd in range(nh))

    def update(ss, vts, state):
        out = []
        for hd in range(nh):
            smax = jnp.max(ss[hd], axis=0, keepdims=True)
            if state is None:
                m_new = smax
                acc = _dot(vts[hd], jnp.exp2(ss[hd] - m_new).astype(BF16))
            else:
                m_old, acc_old = state[hd]
                m_new = jnp.maximum(m_old, smax)
                pv = _dot(vts[hd], jnp.exp2(ss[hd] - m_new).astype(BF16))
                acc = acc_old * jnp.exp2(m_old - m_new) + pv
            out.append((m_new, acc))
        return tuple(out)

    def lat_k(off, n):
        return [kl_ref[hd, pl.ds(off, n), :] for hd in range(nh)]

    def lat_vt(off, n):
        return [vtl_ref[0, rows(hd), pl.ds(off, n)] for hd in range(nh)]

    def update_stale(ss, vts, state, jump):
        out = []
        for hd in range(nh):
            m_old, acc_old = state[hd]
            pv = _dot(vts[hd], jnp.exp2(ss[hd] - m_old).astype(BF16))
            smax = jnp.max(ss[hd], axis=0, keepdims=True)
            m_new = jnp.maximum(m_old, smax)
            out.append((m_new, (acc_old + pv) * jnp.exp2(m_old - m_new)))
            jump = jnp.maximum(jump, smax - m_old)
        return tuple(out), jump

    state = update(scores([kc_ref[hd] for hd in range(nh)]),
                   [vtc_ref[0, rows(hd), :] for hd in range(nh)], None)
    if n_lat_tiles:
        def fast_body(i, carry):
            off = pl.multiple_of(i * tk, tk)
            return update_stale(scores(lat_k(off, tk)), lat_vt(off, tk), *carry)
        fast, jump = lax.fori_loop(0, n_lat_tiles, fast_body,
                                   (state, jnp.zeros((1, q_ref.shape[1]), F32)))

        def exact_loop():
            def body(i, st):
                off = pl.multiple_of(i * tk_exact, tk_exact)
                return update(scores(lat_k(off, tk_exact)), lat_vt(off, tk_exact), st)
            return lax.fori_loop(0, n_lat_tiles * (tk // tk_exact), body, state)

        state = lax.cond(jnp.max(jump) > STALE_MAX_JUMP, exact_loop, lambda: fast)
    for pair in range(nh // 2):
        outs = []
        for hd in (2 * pair, 2 * pair + 1):
            acc = state[hd][1]
            outs.append(acc[:VAL_DIM] / acc[VAL_DIM:VAL_DIM + 1])
        o_ref[:, pair * 2 * VAL_DIM:(pair + 1) * 2 * VAL_DIM] = (
            jnp.concatenate(outs, axis=0).T.astype(BF16))


def attention(q, kc, vtc, kl, vtl, batch, tq=256, tk=2048):
    h, t, _ = q.shape
    lq = t // batch
    nq = lq // tq
    lc = kc.shape[1] // batch
    once = pl.Buffered(1)
    in_specs = [pl.BlockSpec((h, tq, HEAD_PAD), lambda b, i: (0, b * nq + i, 0)),
                pl.BlockSpec((h, lc, HEAD_PAD), lambda b, i: (0, b, 0), pipeline_mode=once),
                pl.BlockSpec((1, h * VT_ROWS, lc), lambda b, i: (b, 0, 0), pipeline_mode=once)]
    args = [q, kc, vtc]
    n_lat_tiles = 0
    if kl is not None:
        ll = kl.shape[1] // batch
        tk = min(tk, ll)
        n_lat_tiles = ll // tk
        in_specs += [pl.BlockSpec((h, ll, HEAD_PAD), lambda b, i: (0, b, 0), pipeline_mode=once),
                     pl.BlockSpec((1, h * VT_ROWS, ll), lambda b, i: (b, 0, 0), pipeline_mode=once)]
        args += [kl, vtl]
    return pl.pallas_call(
        functools.partial(_attn_kernel, tk=tk, tk_exact=min(tk, ATTN_EXACT_KEYS), n_lat_tiles=n_lat_tiles),
        grid=(batch, nq),
        in_specs=in_specs,
        out_specs=pl.BlockSpec((tq, h * VAL_DIM), lambda b, i: (b * nq + i, 0)),
        out_shape=jax.ShapeDtypeStruct((t, h * VAL_DIM), BF16),
        compiler_params=_cparams(("parallel", "arbitrary")),
        name="attention",
    )(*args)


def _pool_kernel(x_ref, prev_ref, next_ref, wp_ref, sp_ref, o_ref, ext_sc, a_sc, b_sc, *, seq, tm):
    i = pl.program_id(0)
    npb = seq // tm
    pos0 = (i % npb) * tm
    has_prev = (pos0 > 0).astype(F32)
    has_next = (pos0 + tm < seq).astype(F32)
    x = x_ref[...]
    ext_sc[:POOL_HALO] = prev_ref[...] * has_prev
    ext_sc[POOL_HALO:POOL_HALO + tm] = x
    rows = tm + 2 * POOL_HALO
    ext_sc[POOL_HALO + tm:rows] = next_ref[...] * has_next
    ext_sc[rows:] = jnp.zeros((ext_sc.shape[0] - rows, ext_sc.shape[1]), F32)
    t = pos0 + lax.broadcasted_iota(jnp.int32, (tm, GROUP_CH), 0)
    outs = []
    for g, win in enumerate(POOL_WINS):
        left = win // 2
        right = win - 1 - left
        cnt = (jnp.minimum(t + right + 1, seq) - jnp.maximum(t - left, 0)).astype(F32)
        sl = slice(g * GROUP_CH, (g + 1) * GROUP_CH)
        if win < 8:
            wsum = x[:, sl]
            for dlt in range(-left, right + 1):
                if dlt:
                    wsum = wsum + ext_sc[POOL_HALO + dlt:POOL_HALO + dlt + tm, sl]
        else:
            a_sc[...] = ext_sc[0:rows, sl] + ext_sc[1:rows + 1, sl]
            b_sc[:rows - 8] = a_sc[0:rows - 8] + a_sc[2:rows - 6]
            a_sc[:rows - 16] = b_sc[0:rows - 16] + b_sc[4:rows - 12]
            lo = POOL_HALO - left
            wsum = a_sc[lo:lo + tm]
            if win == 16:
                wsum = wsum + a_sc[lo + 8:lo + 8 + tm]
        mixed = (wsum / cnt - x[:, sl]).astype(BF16)
        outs.append(_dot(mixed, wp_ref[g]))
    o_ref[...] = (jnp.concatenate(outs, axis=1) * sp_ref[...]).astype(BF16)


def pool_mixer(p, wp, sp, batch, tm=256):
    t, w = p.shape
    seq = t // batch
    tm = min(tm, seq)
    hb = tm // POOL_HALO
    nh = t // POOL_HALO
    return pl.pallas_call(
        functools.partial(_pool_kernel, seq=seq, tm=tm),
        grid=(t // tm,),
        in_specs=[pl.BlockSpec((tm, w), lambda i: (i, 0)),
                  pl.BlockSpec((POOL_HALO, w), lambda i: (jnp.maximum(i * hb - 1, 0), 0)),
                  pl.BlockSpec((POOL_HALO, w), lambda i: (jnp.minimum((i + 1) * hb, nh - 1), 0)),
                  _full(wp.shape), _full(sp.shape)],
        out_specs=pl.BlockSpec((tm, w), lambda i: (i, 0)),
        out_shape=jax.ShapeDtypeStruct((t, w), BF16),
        scratch_shapes=[pltpu.VMEM((tm + 2 * POOL_HALO + 8, w), F32),
                        pltpu.VMEM((tm + 2 * POOL_HALO, GROUP_CH), F32),
                        pltpu.VMEM((tm + 2 * POOL_HALO, GROUP_CH), F32)],
        compiler_params=_cparams(("parallel",)),
        name="pool_mixer",
    )(p, p, p, wp, sp)


def _dft_outer_kernel(f_ref, x_ref, o_ref):
    _, n1, grp, w = x_ref.shape
    out = _dot(f_ref[...], x_ref[0].reshape(n1 * grp, w).astype(BF16))
    o_ref[0] = out.reshape(o_ref.shape[1], grp, w)


def _dft_inner_kernel(a_ref, g_ref, cs_ref, o_ref, *, scale):
    n2 = g_ref.shape[1] // 2
    for k in range(g_ref.shape[0]):
        rhs = a_ref[0, :, k].reshape(2 * n2, a_ref.shape[-1]).astype(BF16)
        p = _dot(g_ref[k], rhs)
        outs = []
        for g in range(p.shape[1] // GROUP_CH):
            sl = slice(g * GROUP_CH, (g + 1) * GROUP_CH)
            lhs = jnp.concatenate([p[:n2, sl], p[n2:, sl]], axis=1).astype(BF16)
            outs.append(_dot(lhs, cs_ref[...]))
        o_ref[0, :, k, :] = jnp.concatenate(outs, axis=1) * scale


def _dft_tables(seq):
    n2 = seq if seq <= 2 * DFT_N2 else DFT_N2
    n1 = seq // n2
    k2 = jnp.arange(n2, dtype=jnp.int32)[:, None]
    t2 = jnp.arange(n2, dtype=jnp.int32)[None, :]
    k1 = jnp.arange(n1, dtype=jnp.int32)[:, None, None]
    ang_a = ((t2 * k2) % n2).astype(F32) * (2.0 * math.pi / n2)
    ang_b = ((t2 * k1) % seq).astype(F32) * (2.0 * math.pi / seq)
    ca, sa, cb, sb = jnp.cos(ang_a), jnp.sin(ang_a), jnp.cos(ang_b), jnp.sin(ang_b)
    gr = ca * cb - sa * sb
    gi = -(sa * cb + ca * sb)
    gmat = jnp.concatenate([jnp.concatenate([gr, -gi], axis=2),
                            jnp.concatenate([gi, gr], axis=2)], axis=1).astype(BF16)
    r = jnp.arange(2 * n1 * DFT_GROUP, dtype=jnp.int32)[:, None]
    q = jnp.arange(n1 * DFT_GROUP, dtype=jnp.int32)[None, :]
    ang1 = ((((r // DFT_GROUP) % n1) * (q // DFT_GROUP)) % n1).astype(F32) * (2.0 * math.pi / n1)
    f1 = jnp.where(r // DFT_GROUP >= n1, -jnp.sin(ang1), jnp.cos(ang1))
    f1 = jnp.where(r % DFT_GROUP == q % DFT_GROUP, f1, 0.0).astype(BF16)
    c = jnp.arange(GROUP_CH, dtype=jnp.int32)
    angc = ((c[:, None] * c[None, :]) % GROUP_CH).astype(F32) * (2.0 * math.pi / GROUP_CH)
    cs = jnp.concatenate([jnp.cos(angc), jnp.sin(angc)], axis=0).astype(BF16)
    return gmat, f1, cs


def fourier_mixer(f, batch, tables):
    gmat, f1, cs = tables
    t, w = f.shape
    seq = t // batch
    n1, n2x2, _ = gmat.shape
    n2 = n2x2 // 2
    scale = 1.0 / math.sqrt(seq * GROUP_CH)
    if n1 > 1:
        a = pl.pallas_call(
            _dft_outer_kernel,
            grid=(batch, n2 // DFT_GROUP),
            in_specs=[_full(f1.shape), pl.BlockSpec((1, n1, DFT_GROUP, w), lambda b, j: (b, 0, j, 0))],
            out_specs=pl.BlockSpec((1, 2 * n1, DFT_GROUP, w), lambda b, j: (b, 0, j, 0)),
            out_shape=jax.ShapeDtypeStruct((batch, 2 * n1, n2, w), F32),
            compiler_params=_cparams(("parallel", "parallel")),
            name="dft_outer",
        )(f1, f.reshape(batch, n1, n2, w))
        a = a.reshape(batch, 2, n1, n2, w)
    else:
        fr = f.reshape(batch, 1, 1, n2, w)
        a = jnp.concatenate([fr, jnp.zeros_like(fr)], axis=1)
    kg = min(DFT_GROUP, n1)
    out = pl.pallas_call(
        functools.partial(_dft_inner_kernel, scale=scale),
        grid=(batch, n1 // kg),
        in_specs=[pl.BlockSpec((1, 2, kg, n2, w), lambda b, k: (b, 0, k, 0, 0)),
                  pl.BlockSpec((kg, 2 * n2, 2 * n2), lambda b, k: (k, 0, 0)),
                  _full(cs.shape)],
        out_specs=pl.BlockSpec((1, n2, kg, w), lambda b, k: (b, 0, k, 0)),
        out_shape=jax.ShapeDtypeStruct((batch, n2, n1, w), F32),
        compiler_params=_cparams(("parallel", "parallel")),
        name="dft_inner",
    )(a, gmat, cs)
    return out.reshape(t, w)


def _merge_kernel(a_ref, p_ref, f_ref, g_ref, x_ref, mod_ref, woa_ref, wob_ref, woc_ref, wout_ref,
                  g2_ref, xo_ref, h_ref):
    d = x_ref.shape[1]
    gate = g_ref[...].astype(F32)
    y = (gate[:, :d] * _dot(a_ref[...], woa_ref[...])
         + gate[:, d:2 * d] * _dot(p_ref[...], wob_ref[...])
         + gate[:, 2 * d:] * _dot(f_ref[...].astype(BF16), woc_ref[...]))
    mod = mod_ref[0]
    xn = x_ref[...] + mod[0:1] * _dot(y.astype(BF16), wout_ref[...])
    xo_ref[...] = xn
    h_ref[...] = (_rms(xn, g2_ref[...]) * (1.0 + mod[2:3]) + mod[1:2]).astype(BF16)


def merge(attn, pooled, four, gate, x, modv, lw, g2, batch, tm=256):
    t, d = x.shape
    npb = (t // batch) // tm
    row = lambda w: pl.BlockSpec((tm, w), lambda i: (i, 0))
    weights = [lw["woa"], lw["wob"], lw["woc"], lw["wout"], g2]
    return pl.pallas_call(
        _merge_kernel,
        grid=(t // tm,),
        in_specs=[row(attn.shape[1]), row(pooled.shape[1]), row(four.shape[1]), row(gate.shape[1]),
                  row(d), pl.BlockSpec((1, 3, d), lambda i: (i // npb, 0, 0))]
                 + [_full(w.shape) for w in weights],
        out_specs=[row(d), row(d)],
        out_shape=[jax.ShapeDtypeStruct((t, d), F32), jax.ShapeDtypeStruct((t, d), BF16)],
        compiler_params=_cparams(("parallel",)),
        name="merge",
    )(attn, pooled, four, gate, x, modv, *weights)


def _ffn_kernel(h_ref, x_ref, gt_ref, w1_ref, w3_ref, w2_ref, o_ref):
    h = h_ref[...]
    a = _dot(h, w1_ref[...])
    b = _dot(h, w3_ref[...])
    g = (a * jax.nn.sigmoid(a) * b).astype(BF16)
    o_ref[...] = x_ref[...] + gt_ref[0] * _dot(g, w2_ref[...])


def ffn(h, x, gt, w1, w3, w2, batch, tm=256):
    t, d = x.shape
    npb = (t // batch) // tm
    return pl.pallas_call(
        _ffn_kernel,
        grid=(t // tm,),
        in_specs=[pl.BlockSpec((tm, d), lambda i: (i, 0)),
                  pl.BlockSpec((tm, d), lambda i: (i, 0)),
                  pl.BlockSpec((1, 1, d), lambda i: (i // npb, 0, 0)),
                  _full(w1.shape), _full(w3.shape), _full(w2.shape)],
        out_specs=pl.BlockSpec((tm, d), lambda i: (i, 0)),
        out_shape=jax.ShapeDtypeStruct((t, d), F32),
        compiler_params=_cparams(("parallel",)),
        name="ffn",
    )(h, x, gt, w1, w3, w2)


def _router_kernel(x_ref, mod_ref, g2_ref, wr_ref, br_ref, tri_ref, comb_ref, pos_ref, cum_ref, cnt_sc):
    @pl.when(pl.program_id(0) == 0)
    def _():
        cnt_sc[...] = jnp.zeros_like(cnt_sc)

    mod = mod_ref[0]
    h = _rms(x_ref[...], g2_ref[...]) * (1.0 + mod[1:2]) + mod[0:1]
    h_hi = h.astype(BF16)
    h_lo = (h - h_hi.astype(F32)).astype(BF16)
    w = wr_ref[...]
    w_hi = w.astype(BF16)
    w_lo = (w - w_hi.astype(F32)).astype(BF16)
    ne = w.shape[0]
    part = _dot_nt(jnp.concatenate([w_hi, w_lo], axis=0), h_hi)
    logits = part[:ne] + part[ne:] + _dot_nt(w_hi, h_lo) + br_ref[...]
    ne = logits.shape[0]
    eidx = lax.broadcasted_iota(jnp.int32, logits.shape, 0)
    m1 = jnp.max(logits, axis=0, keepdims=True)
    i1 = jnp.min(jnp.where(logits == m1, eidx, ne), axis=0, keepdims=True)
    sel1 = eidx == i1
    rest = jnp.where(sel1, -jnp.inf, logits)
    m2 = jnp.max(rest, axis=0, keepdims=True)
    i2 = jnp.min(jnp.where(rest == m2, eidx, ne), axis=0, keepdims=True)
    sel2 = eidx == i2
    e2 = jnp.exp(m2 - m1)
    w1 = 1.0 / (1.0 + e2)
    comb_ref[...] = jnp.where(sel1, w1, 0.0) + jnp.where(sel2, e2 * w1, 0.0)
    self = jnp.where(sel1 | sel2, 1.0, 0.0)
    before = cnt_sc[...][:, 0:1]
    rank = before + _dot(self.astype(BF16), tri_ref[...])
    pos_ref[...] = jnp.where(self > 0.0, rank, -1.0).astype(jnp.int32)
    after = before + jnp.sum(self, axis=1, keepdims=True)
    cnt_sc[...] = jnp.broadcast_to(after, cnt_sc.shape)
    cum_ref[...] = jnp.broadcast_to(after, cum_ref.shape).astype(jnp.int32)


def router(x, modv, g2, wr_t, br, batch, tm=256):
    t, d = x.shape
    ne = wr_t.shape[0]
    npb = (t // batch) // tm
    nt = t // tm
    tri = jnp.asarray(np.triu(np.ones((tm, tm), np.float32), 1), dtype=BF16)
    return pl.pallas_call(
        _router_kernel,
        grid=(nt,),
        in_specs=[pl.BlockSpec((tm, d), lambda i: (i, 0)),
                  pl.BlockSpec((1, 2, d), lambda i: (i // npb, 0, 0)),
                  _full(g2.shape), _full(wr_t.shape), _full(br.shape), _full(tri.shape)],
        out_specs=[pl.BlockSpec((ne, tm), lambda i: (0, i)),
                   pl.BlockSpec((ne, tm), lambda i: (0, i)),
                   pl.BlockSpec((ne, LANES), lambda i: (0, i))],
        out_shape=[jax.ShapeDtypeStruct((ne, t), F32), jax.ShapeDtypeStruct((ne, t), jnp.int32),
                   jax.ShapeDtypeStruct((ne, nt * LANES), jnp.int32)],
        scratch_shapes=[pltpu.VMEM((ne, LANES), F32)],
        compiler_params=_cparams(("arbitrary",)),
        name="router",
    )(x, modv, g2, wr_t, br, tri)


def _moe_kernel(cum_ref, h_ref, posr_ref, pc_ref, *rest, nsub, cum0, first, col):
    if first:
        w1_ref, w3_ref, w2_ref, o_ref, hc_sc, ys_sc = rest
    else:
        yin_ref, w1_ref, w3_ref, w2_ref, o_ref, hc_sc, ys_sc = rest
    c = pl.program_id(0)

    def cum(i):
        return cum_ref[cum0 + c * nsub + i]

    base = cum(0)
    total = cum(nsub) - base
    nfull = total // MOE_ROWS
    rem = total - nfull * MOE_ROWS
    nb = nfull + (rem > 0).astype(jnp.int32)
    nwin = min(MOE_WINDOW, nsub)

    def expert_block(j, nrows):
        off = base + j * MOE_ROWS
        s_lo = sum((cum(s + 1) <= off).astype(jnp.int32) for s in range(nsub))
        s_hi = sum((cum(s) < off + nrows).astype(jnp.int32) for s in range(nsub))

        def compact(win, width):
            iota = lax.broadcasted_iota(jnp.int32, (nrows, width), 0)
            sel = jnp.where(posr_ref[0, :, win] - off == iota, 1.0, 0.0).astype(BF16)
            hc_sc[:nrows] = _dot(sel, h_ref[win, :]).astype(BF16)

        @pl.when(s_hi - s_lo <= nwin)
        def _():
            start = pl.multiple_of(jnp.minimum(s_lo, nsub - nwin) * MOE_SUB, MOE_SUB)
            compact(pl.ds(start, nwin * MOE_SUB), nwin * MOE_SUB)

        if nwin < nsub:
            @pl.when(s_hi - s_lo > nwin)
            def _():
                compact(slice(None), nsub * MOE_SUB)

        hc = hc_sc[:nrows]
        a = _dot(hc, w1_ref[0])
        b = _dot(hc, w3_ref[0])
        g = (a * jax.nn.sigmoid(a) * b).astype(BF16)
        ys_sc[j, :nrows] = _dot(g, w2_ref[0]).astype(BF16)
        if nrows < MOE_ROWS:
            ys_sc[j, nrows:] = jnp.zeros((MOE_ROWS - nrows, ys_sc.shape[2]), BF16)

    def full_block(j, carry):
        expert_block(j, MOE_ROWS)
        return carry

    lax.fori_loop(0, nfull, full_block, 0)

    @pl.when(rem > MOE_ROWS // 2)
    def _():
        expert_block(nfull, MOE_ROWS)

    @pl.when((rem > 0) & (rem <= MOE_ROWS // 2))
    def _():
        expert_block(nfull, MOE_ROWS // 2)

    ys_sc[nb] = jnp.zeros(ys_sc.shape[1:], BF16)
    ys_sc[nb + 1] = jnp.zeros(ys_sc.shape[1:], BF16)
    ne = pc_ref.shape[1] // 2
    liota = lax.broadcasted_iota(jnp.int32, (MOE_SUB, 2 * MOE_ROWS), 1).astype(F32)
    for s in range(nsub):
        sub = slice(s * MOE_SUB, (s + 1) * MOE_SUB)
        pos = pc_ref[sub, col:col + 1]
        j_lo = jnp.minimum((cum(s) - base) // MOE_ROWS, nb)
        off = (base + j_lo * MOE_ROWS).astype(F32)
        sel = jnp.where(pos - off == liota, 1.0, 0.0).astype(BF16)
        rows = ys_sc[pl.ds(j_lo, 2)].reshape(2 * MOE_ROWS, ys_sc.shape[2])
        y = pc_ref[sub, ne + col:ne + col + 1] * _dot(sel, rows)
        if not first:
            y = yin_ref[sub, :].astype(F32) + y
        o_ref[sub, :] = y.astype(BF16)


def moe(h, comb_t, pos_t, cum_incl, w1, w3, w2):
    t, d = h.shape
    ne = comb_t.shape[0]
    tc = min(MOE_CHUNK, t)
    assert t % tc == 0 and tc % MOE_SUB == 0
    nsub = tc // MOE_SUB
    nt = t // MOE_SUB
    cum = jnp.concatenate([jnp.zeros((ne, 1), jnp.int32), cum_incl[:, ::LANES]], axis=1)
    cum = cum.reshape(-1)
    posr = pos_t.reshape(ne, 1, t)
    pc = jnp.concatenate([pos_t.astype(F32).T, comb_t.T], axis=1)
    once = pl.Buffered(1)
    y = None
    for e in range(ne):
        first = y is None
        chunk = pl.BlockSpec((tc, d), lambda c, cum: (c, 0))
        in_specs = [chunk,
                    pl.BlockSpec((1, 1, tc), lambda c, cum, e=e: (e, 0, c)),
                    pl.BlockSpec((tc, 2 * ne), lambda c, cum: (c, 0))]
        in_specs += [] if first else [chunk]
        in_specs += [pl.BlockSpec((1,) + w.shape[1:], lambda c, cum, e=e: (e, 0, 0), pipeline_mode=once)
                     for w in (w1, w3, w2)]
        grid_spec = pltpu.PrefetchScalarGridSpec(
            num_scalar_prefetch=1, grid=(t // tc,), in_specs=in_specs, out_specs=chunk,
            scratch_shapes=[pltpu.VMEM((MOE_ROWS, d), BF16),
                            pltpu.VMEM((tc // MOE_ROWS + 2, MOE_ROWS, d), BF16)])
        y = pl.pallas_call(
            functools.partial(_moe_kernel, nsub=nsub, cum0=e * (nt + 1), first=first, col=e),
            grid_spec=grid_spec,
            out_shape=jax.ShapeDtypeStruct((t, d), BF16),
            input_output_aliases={} if first else {4: 0},
            compiler_params=pltpu.CompilerParams(dimension_semantics=("arbitrary",),
                                                 vmem_limit_bytes=MOE_VMEM_LIMIT),
            name="moe_expert",
        )(*([cum, h, posr, pc] + ([] if first else [y]) + [w1, w3, w2]))
    return y


def _residual_kernel(x_ref, y_ref, gt_ref, g_ref, o_ref, *, norm):
    x = x_ref[...] + gt_ref[0] * y_ref[...].astype(F32)
    o_ref[...] = _rms(x, g_ref[...]) if norm else x


def residual(x, y, gt, g, batch, norm, tm=512):
    t, d = x.shape
    tm = min(tm, t // batch)
    npb = (t // batch) // tm
    return pl.pallas_call(
        functools.partial(_residual_kernel, norm=norm),
        grid=(t // tm,),
        in_specs=[pl.BlockSpec((tm, d), lambda i: (i, 0)), pl.BlockSpec((tm, d), lambda i: (i, 0)),
                  pl.BlockSpec((1, 1, d), lambda i: (i // npb, 0, 0)), _full(g.shape)],
        out_specs=pl.BlockSpec((tm, d), lambda i: (i, 0)),
        out_shape=jax.ShapeDtypeStruct((t, d), F32),
        compiler_params=_cparams(("parallel",)),
        name="residual_norm" if norm else "residual",
    )(x, y, gt, g)


def _final_norm_kernel(x_ref, g_ref, o_ref):
    o_ref[...] = _rms(x_ref[...], g_ref[...])


def final_norm(x, g, tm=512):
    t, d = x.shape
    return pl.pallas_call(
        _final_norm_kernel,
        grid=(t // tm,),
        in_specs=[pl.BlockSpec((tm, d), lambda i: (i, 0)), _full(g.shape)],
        out_specs=pl.BlockSpec((tm, d), lambda i: (i, 0)),
        out_shape=jax.ShapeDtypeStruct((t, d), F32),
        compiler_params=_cparams(("parallel",)),
        name="final_norm",
    )(x, g)


def _rope_tables(seq, rotate):
    half = ROPE_DIM // 2
    nf = ROPE_DIM // 4
    if rotate:
        t = jnp.arange(seq, dtype=jnp.int32)
        rows = (t // GRID_WIDTH).astype(F32)
        cols = (t % GRID_WIDTH).astype(F32)
        inv = ROPE_THETA ** (-jnp.arange(0, half, 2, dtype=F32) / half)
        ang = jnp.stack([rows[:, None] * inv, cols[:, None] * inv], axis=1)
        cos, sin = jnp.cos(ang), jnp.sin(ang)
    else:
        cos = jnp.ones((seq, 2, nf), F32)
        sin = jnp.zeros((seq, 2, nf), F32)
    cpat = jnp.concatenate([cos, cos], axis=2).reshape(seq, ROPE_DIM)
    spat = jnp.concatenate([-sin, sin], axis=2).reshape(seq, ROPE_DIM)
    return cpat, spat


def _rope_inputs(seq, rotate):
    cpat, spat = _rope_tables(seq, rotate)
    tabq = jnp.concatenate([jnp.tile(cpat, (1, NUM_HEADS)), jnp.tile(spat, (1, NUM_HEADS))], axis=1)
    tabk = jnp.concatenate([cpat, spat, jnp.zeros((seq, LANES - 2 * ROPE_DIM), F32)], axis=1)
    return tabq, tabk


def _swap_perm():
    j = np.arange(ROPE_DIM)
    half = ROPE_DIM // 4
    return np.where((j % (2 * half)) < half, j + half, j - half)


def _placement(shared):
    nsrc = ROPE_DIM if shared else NUM_HEADS * ROPE_DIM
    e = np.zeros((LANES if shared else 2 * nsrc, NUM_HEADS * HEAD_PAD), np.float32)
    for hd in range(NUM_HEADS):
        for j in range(ROPE_DIM):
            col = hd * HEAD_PAD + NOPE_DIM + j
            src = j if shared else hd * ROPE_DIM + j
            e[src, col] = 1.0
            e[nsrc + src, col] = 1.0
    return jnp.asarray(e, dtype=BF16)


def _prep_layer(w_in, b_gate, g_q, w_uq, g_kv, w_ukv, w_oa, w_ob, w_oc, w_out, w_pool, s_pool):
    d = w_in.shape[0]
    q_lora, kv_lora = g_q.shape[0], g_kv.shape[0]
    perm = _swap_perm()
    off_kr = q_lora + kv_lora
    off_pool = off_kr + ROPE_DIM
    pool_w = w_pool.shape[0] * w_pool.shape[1]
    off_fft = off_pool + pool_w
    off_gate = w_in.shape[1] - NUM_BRANCH * d
    kr = w_in[:, off_kr:off_pool]
    wa = jnp.concatenate([w_in[:, :off_kr], kr, kr[:, perm],
                          jnp.zeros((d, LANES - 2 * ROPE_DIM), F32)], axis=1)
    qscale = (NOPE_DIM + ROPE_DIM) ** -0.5 * math.log2(math.e)
    wq = w_uq * qscale
    wqn = jnp.pad(wq[:, :, :NOPE_DIM], ((0, 0), (0, 0), (0, HEAD_PAD - NOPE_DIM)))
    wqr = wq[:, :, NOPE_DIM:]
    wqr = jnp.concatenate([wqr.reshape(q_lora, -1), wqr[:, :, perm].reshape(q_lora, -1)], axis=1)
    wkn = jnp.pad(w_ukv[:, :, :NOPE_DIM], ((0, 0), (0, 0), (0, HEAD_PAD - NOPE_DIM)))
    wv = jnp.pad(w_ukv[:, :, NOPE_DIM:], ((0, 0), (0, 0), (0, VT_ROWS - VAL_DIM)))
    vone = np.zeros((NUM_HEADS, VT_ROWS, 1), np.float32)
    vone[:, VAL_DIM] = 1.0
    return dict(
        wa=wa.astype(BF16), wpool=w_in[:, off_pool:off_fft].astype(BF16),
        wfft=w_in[:, off_fft:off_gate].astype(BF16), wgate=w_in[:, off_gate:].astype(BF16),
        bgate=b_gate.reshape(1, -1),
        gq=g_q.reshape(1, -1), wqn=wqn.reshape(q_lora, -1).astype(BF16), wqr=wqr.astype(BF16),
        eq=_placement(False), gkv=g_kv.reshape(1, -1),
        wkn=wkn.reshape(kv_lora, -1).astype(BF16), ek=_placement(True),
        wvt=wv.reshape(kv_lora, -1).T.astype(BF16), vone=jnp.asarray(vone.reshape(-1, 1)),
        woa=w_oa.astype(BF16), wob=w_ob.astype(BF16), woc=w_oc.astype(BF16), wout=w_out.astype(BF16),
        wp=w_pool.astype(BF16), sp=s_pool.reshape(1, -1),
    )


def _token_mixer(xs, modv1, modv2, lw, g1, g2, ropes, dft, batch, kv_ctx, tm):
    tabq, tabk = ropes
    q, k, vt, pool_in, fft_in, gate = proj_in(xs, modv1, g1, tabq, tabk, lw, batch, False, tm)
    if kv_ctx is None:
        attn = attention(q, k, vt, None, None, batch)
    else:
        attn = attention(q, kv_ctx[0], kv_ctx[1], k, vt, batch)
    pooled = pool_mixer(pool_in, lw["wp"], lw["sp"], batch)
    four = fourier_mixer(fft_in, batch, dft)
    x_new, h2 = merge(attn, pooled, four, gate, xs, modv2, lw, g2, batch, tm)
    return x_new, h2, (k, vt)


def kernel(x, c, ctx, c_ctx, w_mod, b_mod, g_norm1, w_in, b_gate, g_q, w_uq, g_kv, w_ukv, w_pool, s_pool,
           w_oa, w_ob, w_oc, w_out, g_norm2, ffn_w1, ffn_w3, ffn_w2, moe_router, moe_bias,
           moe_w1, moe_w3, moe_w2, g_final):
    batch, seq, d = x.shape
    n_ctx = ctx.shape[1]
    depth = w_mod.shape[0]
    n_exp = moe_router.shape[-1] if moe_router.shape[0] else 0
    xl = x.reshape(batch * seq, d)
    xc = ctx.reshape(batch * n_ctx, d)

    cvec = jnp.concatenate([c, c_ctx[None], jnp.zeros((8 - batch - 1, d), F32)], axis=0)
    mods = modulation(cvec, w_mod, b_mod)

    rope_l = _rope_inputs(seq, True)
    rope_c = _rope_inputs(n_ctx, False)
    dft_l = _dft_tables(seq)
    dft_c = _dft_tables(n_ctx)

    for layer in range(depth):
        last = layer == depth - 1
        lw = _prep_layer(w_in[layer], b_gate[layer], g_q[layer], w_uq[layer], g_kv[layer], w_ukv[layer],
                         w_oa[layer], w_ob[layer], w_oc[layer], w_out[layer], w_pool[layer], s_pool[layer])
        g1 = g_norm1[layer].reshape(1, d)
        g2 = g_norm2[layer].reshape(1, d)
        ml = mods[layer, :batch].reshape(batch, 6, d)
        mc = jnp.broadcast_to(mods[layer, batch].reshape(1, 6, d), (batch, 6, d))

        if last:
            k_c, vt_c = proj_in(xc, mc[:, 0:2], g1, rope_c[0], rope_c[1], lw, batch, True)
            kv_c = (k_c, vt_c)
        else:
            xc_new, h2_c, kv_c = _token_mixer(xc, mc[:, 0:2], mc[:, 2:5], lw, g1, g2, rope_c, dft_c,
                                              batch, None, 256)
        xl, h2_l, _ = _token_mixer(xl, ml[:, 0:2], ml[:, 2:5], lw, g1, g2, rope_l, dft_l, batch, kv_c, 512)
        if not last:
            xc = xc_new

        streams = [(xl, h2_l, ml)]
        if not last:
            streams.append((xc, h2_c, mc))
        fuse_final = last and layer % 2 == 1
        new = []
        for xs, h2, mm in streams:
            gt2 = mm[:, 5:6]
            li = layer // 2
            if layer % 2 == 0:
                xs = ffn(h2, xs, gt2, ffn_w1[li].astype(BF16), ffn_w3[li].astype(BF16),
                         ffn_w2[li].astype(BF16), batch)
            else:
                comb_t, pos_t, cum_incl = router(xs, mm[:, 3:5], g2, moe_router[li].T,
                                                 moe_bias[li].reshape(n_exp, 1), batch)
                y = moe(h2, comb_t, pos_t, cum_incl, moe_w1[li].astype(BF16), moe_w3[li].astype(BF16),
                        moe_w2[li].astype(BF16))
                xs = residual(xs, y, gt2, g_final.reshape(1, d), batch, fuse_final)
            new.append(xs)
        xl = new[0]
        if not last:
            xc = new[1]

    if not fuse_final:
        xl = final_norm(xl, g_final.reshape(1, d))
    return xl.reshape(batch, seq, d)
```

```python
import functools
import math

import numpy as np
import jax
import jax.numpy as jnp
from jax import lax
from jax.experimental import pallas as pl
from jax.experimental.pallas import tpu as pltpu

GRID_WIDTH = 64
NUM_HEADS = 8
NOPE_DIM = 64
ROPE_DIM = 32
VAL_DIM = 64
ROPE_THETA = 10000.0
POOL_WINS = (2, 4, 8, 16)
GROUP_CH = 128
NUM_BRANCH = 3
NORM_EPS = 1e-6

LANES = 128
HEAD_PAD = 128
VT_ROWS = 80
POOL_HALO = 16
DFT_N2 = 128
DFT_GROUP = 8
VMEM_LIMIT = 56 * 1024 * 1024
MOE_ROWS = 256
MOE_SUB = 256
MOE_CHUNK = 2048
MOE_WINDOW = 6
MOE_VMEM_LIMIT = 60 * 1024 * 1024
FFN_COL_TILE = 256
MOD_COLS = 1536
ATTN_EXACT_KEYS = 512
STALE_MAX_JUMP = 64.0

BF16 = jnp.bfloat16
F32 = jnp.float32


def _cparams(sem):
    return pltpu.CompilerParams(dimension_semantics=sem, vmem_limit_bytes=VMEM_LIMIT)


def _full(shape):
    n = len(shape)
    return pl.BlockSpec(shape, lambda *_: (0,) * n)


def _dot(a, b):
    return jnp.dot(a, b, preferred_element_type=F32)


def _dot_nt(a, b):
    return lax.dot_general(a, b, (((1,), (1,)), ((), ())), preferred_element_type=F32)


def _rms(x, g):
    return x * lax.rsqrt(jnp.mean(x * x, axis=-1, keepdims=True) + NORM_EPS) * g


def _mod_kernel(c_ref, w_ref, b_ref, o_ref):
    c = c_ref[...]
    s = c * jax.nn.sigmoid(c)
    w = w_ref[0]
    s_hi = s.astype(BF16)
    s_lo = (s - s_hi.astype(F32)).astype(BF16)
    w_hi = w.astype(BF16)
    w_lo = (w - w_hi.astype(F32)).astype(BF16)
    part = _dot(jnp.concatenate([s_hi, s_lo], axis=0), w_hi)
    rows = s.shape[0]
    o_ref[0] = part[:rows] + part[rows:] + _dot(s_hi, w_lo) + b_ref[0]


def modulation(cvec, w_mod, b_mod):
    depth, d, n = w_mod.shape
    tn = MOD_COLS
    return pl.pallas_call(
        _mod_kernel,
        grid=(depth, n // tn),
        in_specs=[pl.BlockSpec((8, d), lambda l, j: (0, 0)),
                  pl.BlockSpec((1, d, tn), lambda l, j: (l, 0, j)),
                  pl.BlockSpec((1, 1, tn), lambda l, j: (l, 0, j))],
        out_specs=pl.BlockSpec((1, 8, tn), lambda l, j: (l, 0, j)),
        out_shape=jax.ShapeDtypeStruct((depth, 8, n), F32),
        compiler_params=_cparams(("parallel", "parallel")),
        name="modulation",
    )(cvec, w_mod, b_mod.reshape(depth, 1, n))


def _proj_in_kernel(x_ref, mod_ref, g1_ref, tabq_ref, tabk_ref,
                    wa_ref, wpool_ref, wfft_ref, wgate_ref, bgate_ref,
                    gq_ref, wqn_ref, wqr_ref, eq_ref, gkv_ref, wkn_ref, ek_ref, wvt_ref, vone_ref,
                    *out_refs, q_lora, kv_lora, kv_only):
    if kv_only:
        k_ref, vt_ref = out_refs
    else:
        q_ref, k_ref, vt_ref, pool_ref, fft_ref, gate_ref = out_refs
    x = x_ref[...]
    mod = mod_ref[0]
    h = (_rms(x, g1_ref[...]) * (1.0 + mod[1:2]) + mod[0:1]).astype(BF16)
    pa = _dot(h, wa_ref[...])
    kvn = _rms(pa[:, q_lora:q_lora + kv_lora], gkv_ref[...]).astype(BF16)
    kr = (pa[:, q_lora + kv_lora:] * tabk_ref[...]).astype(BF16)
    k_all = (_dot(kvn, wkn_ref[...]) + _dot(kr, ek_ref[...])).astype(BF16)
    for hd in range(NUM_HEADS):
        k_ref[hd] = k_all[:, hd * HEAD_PAD:(hd + 1) * HEAD_PAD]
    vt_ref[0] = (_dot_nt(wvt_ref[...], kvn) + vone_ref[...]).astype(BF16)
    if kv_only:
        return
    qn = _rms(pa[:, :q_lora], gq_ref[...]).astype(BF16)
    qr = (_dot(qn, wqr_ref[...]) * tabq_ref[...]).astype(BF16)
    q_all = (_dot(qn, wqn_ref[...]) + _dot(qr, eq_ref[...])).astype(BF16)
    for hd in range(NUM_HEADS):
        q_ref[hd] = q_all[:, hd * HEAD_PAD:(hd + 1) * HEAD_PAD]
    pool_ref[...] = _dot(h, wpool_ref[...])
    fft_ref[...] = _dot(h, wfft_ref[...])
    gate_ref[...] = jax.nn.sigmoid(_dot(h, wgate_ref[...]) + bgate_ref[...]).astype(BF16)


def proj_in(x, modv, g1, tabq, tabk, lw, batch, kv_only, tm=256):
    t, d = x.shape
    seq = t // batch
    npb = seq // tm
    q_lora = lw["gq"].shape[1]
    kv_lora = lw["gkv"].shape[1]
    weights = [lw["wa"], lw["wpool"], lw["wfft"], lw["wgate"], lw["bgate"],
               lw["gq"], lw["wqn"], lw["wqr"], lw["eq"], lw["gkv"], lw["wkn"], lw["ek"],
               lw["wvt"], lw["vone"]]
    in_specs = [pl.BlockSpec((tm, d), lambda i: (i, 0)),
                pl.BlockSpec((1, 2, d), lambda i: (i // npb, 0, 0)),
                _full(g1.shape),
                pl.BlockSpec((tm, tabq.shape[1]), lambda i: (i % npb, 0)),
                pl.BlockSpec((tm, tabk.shape[1]), lambda i: (i % npb, 0))]
    in_specs += [_full(w.shape) for w in weights]
    head_spec = pl.BlockSpec((NUM_HEADS, tm, HEAD_PAD), lambda i: (0, i, 0))
    head_shape = jax.ShapeDtypeStruct((NUM_HEADS, t, HEAD_PAD), BF16)
    vt_spec = pl.BlockSpec((1, NUM_HEADS * VT_ROWS, tm), lambda i: (i // npb, 0, i % npb))
    vt_shape = jax.ShapeDtypeStruct((batch, NUM_HEADS * VT_ROWS, seq), BF16)
    if kv_only:
        out_specs = [head_spec, vt_spec]
        out_shape = [head_shape, vt_shape]
    else:
        pw, fw, gw = lw["wpool"].shape[1], lw["wfft"].shape[1], lw["wgate"].shape[1]
        out_specs = [head_spec, head_spec, vt_spec,
                     pl.BlockSpec((tm, pw), lambda i: (i, 0)),
                     pl.BlockSpec((tm, fw), lambda i: (i, 0)),
                     pl.BlockSpec((tm, gw), lambda i: (i, 0))]
        out_shape = [head_shape, head_shape, vt_shape,
                     jax.ShapeDtypeStruct((t, pw), F32),
                     jax.ShapeDtypeStruct((t, fw), F32),
                     jax.ShapeDtypeStruct((t, gw), BF16)]
    return pl.pallas_call(
        functools.partial(_proj_in_kernel, q_lora=q_lora, kv_lora=kv_lora, kv_only=kv_only),
        grid=(t // tm,),
        in_specs=in_specs, out_specs=out_specs, out_shape=out_shape,
        compiler_params=_cparams(("parallel",)),
        name="proj_in_kv" if kv_only else "proj_in",
    )(x, modv, g1, tabq, tabk, *weights)


def _attn_kernel(q_ref, kc_ref, vtc_ref, *rest, tk, tk_exact, n_lat_tiles):
    if n_lat_tiles:
        kl_ref, vtl_ref, o_ref = rest
    else:
        (o_ref,) = rest
    nh = q_ref.shape[0]

    def rows(hd):
        return slice(hd * VT_ROWS, (hd + 1) * VT_ROWS)

    def scores(ks):
        return tuple(_dot_nt(ks[hd], q_ref[hd]) for hd in range(nh))

    def update(ss, vts, state):
        out = []
        for hd in range(nh):
            smax = jnp.max(ss[hd], axis=0, keepdims=True)
            if state is None:
                m_new = smax
                acc = _dot(vts[hd], jnp.exp2(ss[hd] - m_new).astype(BF16))
            else:
                m_old, acc_old = state[hd]
                m_new = jnp.maximum(m_old, smax)
                pv = _dot(vts[hd], jnp.exp2(ss[hd] - m_new).astype(BF16))
                acc = acc_old * jnp.exp2(m_old - m_new) + pv
            out.append((m_new, acc))
        return tuple(out)

    def lat_k(off, n):
        return [kl_ref[hd, pl.ds(off, n), :] for hd in range(nh)]

    def lat_vt(off, n):
        return [vtl_ref[0, rows(hd), pl.ds(off, n)] for hd in range(nh)]

    def update_stale(ss, vts, state, jump):
        out = []
        for hd in range(nh):
            m_old, acc_old = state[hd]
            pv = _dot(vts[hd], jnp.exp2(ss[hd] - m_old).astype(BF16))
            smax = jnp.max(ss[hd], axis=0, keepdims=True)
            m_new = jnp.maximum(m_old, smax)
            out.append((m_new, (acc_old + pv) * jnp.exp2(m_old - m_new)))
            jump = jnp.maximum(jump, smax - m_old)
        return tuple(out), jump

    state = update(scores([kc_ref[hd] for hd in range(nh)]),
                   [vtc_ref[0, rows(hd), :] for hd in range(nh)], None)
    if n_lat_tiles:
        def fast_body(i, carry):
            off = pl.multiple_of(i * tk, tk)
            return update_stale(scores(lat_k(off, tk)), lat_vt(off, tk), *carry)
        fast, jump = lax.fori_loop(0, n_lat_tiles, fast_body,
                                   (state, jnp.zeros((1, q_ref.shape[1]), F32)))

        def exact_loop():
            def body(i, st):
                off = pl.multiple_of(i * tk_exact, tk_exact)
                return update(scores(lat_k(off, tk_exact)), lat_vt(off, tk_exact), st)
            return lax.fori_loop(0, n_lat_tiles * (tk // tk_exact), body, state)

        state = lax.cond(jnp.max(jump) > STALE_MAX_JUMP, exact_loop, lambda: fast)
    for pair in range(nh // 2):
        outs = []
        for hd in (2 * pair, 2 * pair + 1):
            acc = state[hd][1]
            outs.append(acc[:VAL_DIM] / acc[VAL_DIM:VAL_DIM + 1])
        o_ref[:, pair * 2 * VAL_DIM:(pair + 1) * 2 * VAL_DIM] = (
            jnp.concatenate(outs, axis=0).T.astype(BF16))


def attention(q, kc, vtc, kl, vtl, batch, tq=256, tk=2048):
    h, t, _ = q.shape
    lq = t // batch
    nq = lq // tq
    lc = kc.shape[1] // batch
    once = pl.Buffered(1)
    in_specs = [pl.BlockSpec((h, tq, HEAD_PAD), lambda b, i: (0, b * nq + i, 0)),
                pl.BlockSpec((h, lc, HEAD_PAD), lambda b, i: (0, b, 0), pipeline_mode=once),
                pl.BlockSpec((1, h * VT_ROWS, lc), lambda b, i: (b, 0, 0), pipeline_mode=once)]
    args = [q, kc, vtc]
    n_lat_tiles = 0
    if kl is not None:
        ll = kl.shape[1] // batch
        tk = min(tk, ll)
        n_lat_tiles = ll // tk
        in_specs += [pl.BlockSpec((h, ll, HEAD_PAD), lambda b, i: (0, b, 0), pipeline_mode=once),
                     pl.BlockSpec((1, h * VT_ROWS, ll), lambda b, i: (b, 0, 0), pipeline_mode=once)]
        args += [kl, vtl]
    return pl.pallas_call(
        functools.partial(_attn_kernel, tk=tk, tk_exact=min(tk, ATTN_EXACT_KEYS), n_lat_tiles=n_lat_tiles),
        grid=(batch, nq),
        in_specs=in_specs,
        out_specs=pl.BlockSpec((tq, h * VAL_DIM), lambda b, i: (b * nq + i, 0)),
        out_shape=jax.ShapeDtypeStruct((t, h * VAL_DIM), BF16),
        compiler_params=_cparams(("parallel", "arbitrary")),
        name="attention",
    )(*args)


def _pool_kernel(x_ref, prev_ref, next_ref, band_ref, wp_ref, sp_ref, o_ref, *, seq, tm):
    i = pl.program_id(0)
    npb = seq // tm
    pos0 = (i % npb) * tm
    has_prev = (pos0 > 0).astype(F32)
    has_next = (pos0 + tm < seq).astype(F32)
    x = x_ref[...]
    ext = jnp.concatenate([prev_ref[...] * has_prev, x, next_ref[...] * has_next], axis=0).astype(BF16)
    t = pos0 + lax.broadcasted_iota(jnp.int32, (tm, GROUP_CH), 0)
    outs = []
    for g, win in enumerate(POOL_WINS):
        left = win // 2
        right = win - 1 - left
        cnt = (jnp.minimum(t + right + 1, seq) - jnp.maximum(t - left, 0)).astype(F32)
        sl = slice(g * GROUP_CH, (g + 1) * GROUP_CH)
        wsum = _dot(band_ref[g], ext[:, sl])
        mixed = (wsum / cnt - x[:, sl]).astype(BF16)
        outs.append(_dot(mixed, wp_ref[g]))
    o_ref[...] = (jnp.concatenate(outs, axis=1) * sp_ref[...]).astype(BF16)


def _pool_bands(tm):
    r = np.arange(tm)[:, None]
    c = np.arange(tm + 2 * POOL_HALO)[None, :]
    d = c - POOL_HALO - r
    bands = [((d >= -(w // 2)) & (d <= w - 1 - w // 2)) for w in POOL_WINS]
    return jnp.asarray(np.stack(bands).astype(np.float32), dtype=BF16)


def pool_mixer(p, wp, sp, batch, tm=256):
    t, w = p.shape
    seq = t // batch
    tm = min(tm, seq)
    hb = tm // POOL_HALO
    nh = t // POOL_HALO
    return pl.pallas_call(
        functools.partial(_pool_kernel, seq=seq, tm=tm),
        grid=(t // tm,),
        in_specs=[pl.BlockSpec((tm, w), lambda i: (i, 0)),
                  pl.BlockSpec((POOL_HALO, w), lambda i: (jnp.maximum(i * hb - 1, 0), 0)),
                  pl.BlockSpec((POOL_HALO, w), lambda i: (jnp.minimum((i + 1) * hb, nh - 1), 0)),
                  _full((len(POOL_WINS), tm, tm + 2 * POOL_HALO)),
                  _full(wp.shape), _full(sp.shape)],
        out_specs=pl.BlockSpec((tm, w), lambda i: (i, 0)),
        out_shape=jax.ShapeDtypeStruct((t, w), BF16),
        compiler_params=_cparams(("parallel",)),
        name="pool_mixer",
    )(p, p, p, _pool_bands(tm), wp, sp)


def _dft_outer_kernel(f_ref, x_ref, o_ref):
    _, n1, grp, w = x_ref.shape
    out = _dot(f_ref[...], x_ref[0].reshape(n1 * grp, w).astype(BF16))
    o_ref[0] = out.reshape(o_ref.shape[1], grp, w)


def _dft_inner_kernel(a_ref, g_ref, cs_ref, o_ref, *, scale):
    n2 = g_ref.shape[1] // 2
    for k in range(g_ref.shape[0]):
        rhs = a_ref[0, :, k].reshape(2 * n2, a_ref.shape[-1]).astype(BF16)
        p = _dot(g_ref[k], rhs)
        outs = []
        for g in range(p.shape[1] // GROUP_CH):
            sl = slice(g * GROUP_CH, (g + 1) * GROUP_CH)
            lhs = jnp.concatenate([p[:n2, sl], p[n2:, sl]], axis=1).astype(BF16)
            outs.append(_dot(lhs, cs_ref[...]))
        o_ref[0, :, k, :] = jnp.concatenate(outs, axis=1) * scale


def _dft_tables(seq):
    n2 = seq if seq <= 2 * DFT_N2 else DFT_N2
    n1 = seq // n2
    k2 = jnp.arange(n2, dtype=jnp.int32)[:, None]
    t2 = jnp.arange(n2, dtype=jnp.int32)[None, :]
    k1 = jnp.arange(n1, dtype=jnp.int32)[:, None, None]
    ang_a = ((t2 * k2) % n2).astype(F32) * (2.0 * math.pi / n2)
    ang_b = ((t2 * k1) % seq).astype(F32) * (2.0 * math.pi / seq)
    ca, sa, cb, sb = jnp.cos(ang_a), jnp.sin(ang_a), jnp.cos(ang_b), jnp.sin(ang_b)
    gr = ca * cb - sa * sb
    gi = -(sa * cb + ca * sb)
    gmat = jnp.concatenate([jnp.concatenate([gr, -gi], axis=2),
                            jnp.concatenate([gi, gr], axis=2)], axis=1).astype(BF16)
    r = jnp.arange(2 * n1 * DFT_GROUP, dtype=jnp.int32)[:, None]
    q = jnp.arange(n1 * DFT_GROUP, dtype=jnp.int32)[None, :]
    ang1 = ((((r // DFT_GROUP) % n1) * (q // DFT_GROUP)) % n1).astype(F32) * (2.0 * math.pi / n1)
    f1 = jnp.where(r // DFT_GROUP >= n1, -jnp.sin(ang1), jnp.cos(ang1))
    f1 = jnp.where(r % DFT_GROUP == q % DFT_GROUP, f1, 0.0).astype(BF16)
    c = jnp.arange(GROUP_CH, dtype=jnp.int32)
    angc = ((c[:, None] * c[None, :]) % GROUP_CH).astype(F32) * (2.0 * math.pi / GROUP_CH)
    cs = jnp.concatenate([jnp.cos(angc), jnp.sin(angc)], axis=0).astype(BF16)
    return gmat, f1, cs


def fourier_mixer(f, batch, tables):
    gmat, f1, cs = tables
    t, w = f.shape
    seq = t // batch
    n1, n2x2, _ = gmat.shape
    n2 = n2x2 // 2
    scale = 1.0 / math.sqrt(seq * GROUP_CH)
    if n1 > 1:
        a = pl.pallas_call(
            _dft_outer_kernel,
            grid=(batch, n2 // DFT_GROUP),
            in_specs=[_full(f1.shape), pl.BlockSpec((1, n1, DFT_GROUP, w), lambda b, j: (b, 0, j, 0))],
            out_specs=pl.BlockSpec((1, 2 * n1, DFT_GROUP, w), lambda b, j: (b, 0, j, 0)),
            out_shape=jax.ShapeDtypeStruct((batch, 2 * n1, n2, w), F32),
            compiler_params=_cparams(("parallel", "parallel")),
            name="dft_outer",
        )(f1, f.reshape(batch, n1, n2, w))
        a = a.reshape(batch, 2, n1, n2, w)
    else:
        fr = f.reshape(batch, 1, 1, n2, w)
        a = jnp.concatenate([fr, jnp.zeros_like(fr)], axis=1)
    kg = min(DFT_GROUP, n1)
    out = pl.pallas_call(
        functools.partial(_dft_inner_kernel, scale=scale),
        grid=(batch, n1 // kg),
        in_specs=[pl.BlockSpec((1, 2, kg, n2, w), lambda b, k: (b, 0, k, 0, 0)),
                  pl.BlockSpec((kg, 2 * n2, 2 * n2), lambda b, k: (k, 0, 0)),
                  _full(cs.shape)],
        out_specs=pl.BlockSpec((1, n2, kg, w), lambda b, k: (b, 0, k, 0)),
        out_shape=jax.ShapeDtypeStruct((batch, n2, n1, w), F32),
        compiler_params=_cparams(("parallel", "parallel")),
        name="dft_inner",
    )(a, gmat, cs)
    return out.reshape(t, w)


def _merge_kernel(a_ref, p_ref, f_ref, g_ref, x_ref, mod_ref, woa_ref, wob_ref, woc_ref, wout_ref,
                  g2_ref, xo_ref, h_ref):
    d = x_ref.shape[1]
    gate = g_ref[...].astype(F32)
    y = (gate[:, :d] * _dot(a_ref[...], woa_ref[...])
         + gate[:, d:2 * d] * _dot(p_ref[...], wob_ref[...])
         + gate[:, 2 * d:] * _dot(f_ref[...].astype(BF16), woc_ref[...]))
    mod = mod_ref[0]
    xn = x_ref[...] + mod[0:1] * _dot(y.astype(BF16), wout_ref[...])
    xo_ref[...] = xn
    h_ref[...] = (_rms(xn, g2_ref[...]) * (1.0 + mod[2:3]) + mod[1:2]).astype(BF16)


def merge(attn, pooled, four, gate, x, modv, lw, g2, batch, tm=256):
    t, d = x.shape
    npb = (t // batch) // tm
    row = lambda w: pl.BlockSpec((tm, w), lambda i: (i, 0))
    weights = [lw["woa"], lw["wob"], lw["woc"], lw["wout"], g2]
    return pl.pallas_call(
        _merge_kernel,
        grid=(t // tm,),
        in_specs=[row(attn.shape[1]), row(pooled.shape[1]), row(four.shape[1]), row(gate.shape[1]),
                  row(d), pl.BlockSpec((1, 3, d), lambda i: (i // npb, 0, 0))]
                 + [_full(w.shape) for w in weights],
        out_specs=[row(d), row(d)],
        out_shape=[jax.ShapeDtypeStruct((t, d), F32), jax.ShapeDtypeStruct((t, d), BF16)],
        compiler_params=_cparams(("parallel",)),
        name="merge",
    )(attn, pooled, four, gate, x, modv, *weights)


def _ffn_kernel(h_ref, x_ref, gt_ref, w1_ref, w3_ref, w2_ref, o_ref):
    h = h_ref[...]
    dff = w1_ref.shape[1]
    cut = (dff // (2 * FFN_COL_TILE)) * FFN_COL_TILE
    y = None
    for lo, hi in ((0, cut), (cut, dff)) if 0 < cut < dff else ((0, dff),):
        a = _dot(h, w1_ref[:, lo:hi])
        b = _dot(h, w3_ref[:, lo:hi])
        g = (a * jax.nn.sigmoid(a) * b).astype(BF16)
        part = _dot(g, w2_ref[lo:hi, :])
        y = part if y is None else y + part
    o_ref[...] = x_ref[...] + gt_ref[0] * y


def ffn(h, x, gt, w1, w3, w2, batch, tm=256):
    t, d = x.shape
    npb = (t // batch) // tm
    return pl.pallas_call(
        _ffn_kernel,
        grid=(t // tm,),
        in_specs=[pl.BlockSpec((tm, d), lambda i: (i, 0)),
                  pl.BlockSpec((tm, d), lambda i: (i, 0)),
                  pl.BlockSpec((1, 1, d), lambda i: (i // npb, 0, 0)),
                  _full(w1.shape), _full(w3.shape), _full(w2.shape)],
        out_specs=pl.BlockSpec((tm, d), lambda i: (i, 0)),
        out_shape=jax.ShapeDtypeStruct((t, d), F32),
        compiler_params=_cparams(("parallel",)),
        name="ffn",
    )(h, x, gt, w1, w3, w2)


def _router_kernel(x_ref, mod_ref, g2_ref, wr_ref, br_ref, tri_ref, comb_ref, pos_ref, cum_ref, cnt_sc):
    @pl.when(pl.program_id(0) == 0)
    def _():
        cnt_sc[...] = jnp.zeros_like(cnt_sc)

    mod = mod_ref[0]
    h = _rms(x_ref[...], g2_ref[...]) * (1.0 + mod[1:2]) + mod[0:1]
    h_hi = h.astype(BF16)
    h_lo = (h - h_hi.astype(F32)).astype(BF16)
    w = wr_ref[...]
    w_hi = w.astype(BF16)
    w_lo = (w - w_hi.astype(F32)).astype(BF16)
    ne = w.shape[0]
    part = _dot_nt(jnp.concatenate([w_hi, w_lo], axis=0), h_hi)
    logits = part[:ne] + part[ne:] + _dot_nt(w_hi, h_lo) + br_ref[...]
    ne = logits.shape[0]
    eidx = lax.broadcasted_iota(jnp.int32, logits.shape, 0)
    m1 = jnp.max(logits, axis=0, keepdims=True)
    i1 = jnp.min(jnp.where(logits == m1, eidx, ne), axis=0, keepdims=True)
    sel1 = eidx == i1
    rest = jnp.where(sel1, -jnp.inf, logits)
    m2 = jnp.max(rest, axis=0, keepdims=True)
    i2 = jnp.min(jnp.where(rest == m2, eidx, ne), axis=0, keepdims=True)
    sel2 = eidx == i2
    e2 = jnp.exp(m2 - m1)
    w1 = 1.0 / (1.0 + e2)
    comb_ref[...] = jnp.where(sel1, w1, 0.0) + jnp.where(sel2, e2 * w1, 0.0)
    self = jnp.where(sel1 | sel2, 1.0, 0.0)
    before = cnt_sc[...][:, 0:1]
    rank = before + _dot(self.astype(BF16), tri_ref[...])
    pos_ref[...] = jnp.where(self > 0.0, rank, -1.0).astype(jnp.int32)
    after = before + jnp.sum(self, axis=1, keepdims=True)
    cnt_sc[...] = jnp.broadcast_to(after, cnt_sc.shape)
    cum_ref[...] = jnp.broadcast_to(after, cum_ref.shape).astype(jnp.int32)


def router(x, modv, g2, wr_t, br, batch, tm=256):
    t, d = x.shape
    ne = wr_t.shape[0]
    npb = (t // batch) // tm
    nt = t // tm
    tri = jnp.asarray(np.triu(np.ones((tm, tm), np.float32), 1), dtype=BF16)
    return pl.pallas_call(
        _router_kernel,
        grid=(nt,),
        in_specs=[pl.BlockSpec((tm, d), lambda i: (i, 0)),
                  pl.BlockSpec((1, 2, d), lambda i: (i // npb, 0, 0)),
                  _full(g2.shape), _full(wr_t.shape), _full(br.shape), _full(tri.shape)],
        out_specs=[pl.BlockSpec((ne, tm), lambda i: (0, i)),
                   pl.BlockSpec((ne, tm), lambda i: (0, i)),
                   pl.BlockSpec((ne, LANES), lambda i: (0, i))],
        out_shape=[jax.ShapeDtypeStruct((ne, t), F32), jax.ShapeDtypeStruct((ne, t), jnp.int32),
                   jax.ShapeDtypeStruct((ne, nt * LANES), jnp.int32)],
        scratch_shapes=[pltpu.VMEM((ne, LANES), F32)],
        compiler_params=_cparams(("arbitrary",)),
        name="router",
    )(x, modv, g2, wr_t, br, tri)


def _moe_kernel(cum_ref, h_ref, posr_ref, pc_ref, *rest, nsub, cum0, first, col):
    if first:
        w1_ref, w3_ref, w2_ref, o_ref, hc_sc, ys_sc = rest
    else:
        yin_ref, w1_ref, w3_ref, w2_ref, o_ref, hc_sc, ys_sc = rest
    c = pl.program_id(0)

    def cum(i):
        return cum_ref[cum0 + c * nsub + i]

    base = cum(0)
    total = cum(nsub) - base
    nfull = total // MOE_ROWS
    rem = total - nfull * MOE_ROWS
    nb = nfull + (rem > 0).astype(jnp.int32)
    nwin = min(MOE_WINDOW, nsub)

    def expert_block(j, nrows):
        off = base + j * MOE_ROWS
        s_lo = sum((cum(s + 1) <= off).astype(jnp.int32) for s in range(nsub))
        s_hi = sum((cum(s) < off + nrows).astype(jnp.int32) for s in range(nsub))

        def compact(win, width):
            iota = lax.broadcasted_iota(jnp.int32, (nrows, width), 0)
            sel = jnp.where(posr_ref[0, :, win] - off == iota, 1.0, 0.0).astype(BF16)
            hc_sc[:nrows] = _dot(sel, h_ref[win, :]).astype(BF16)

        @pl.when(s_hi - s_lo <= nwin)
        def _():
            start = pl.multiple_of(jnp.minimum(s_lo, nsub - nwin) * MOE_SUB, MOE_SUB)
            compact(pl.ds(start, nwin * MOE_SUB), nwin * MOE_SUB)

        if nwin < nsub:
            @pl.when(s_hi - s_lo > nwin)
            def _():
                compact(slice(None), nsub * MOE_SUB)

        hc = hc_sc[:nrows]
        a = _dot(hc, w1_ref[0])
        b = _dot(hc, w3_ref[0])
        g = (a * jax.nn.sigmoid(a) * b).astype(BF16)
        ys_sc[j, :nrows] = _dot(g, w2_ref[0]).astype(BF16)
        if nrows < MOE_ROWS:
            ys_sc[j, nrows:] = jnp.zeros((MOE_ROWS - nrows, ys_sc.shape[2]), BF16)

    def full_block(j, carry):
        expert_block(j, MOE_ROWS)
        return carry

    lax.fori_loop(0, nfull, full_block, 0)

    @pl.when(rem > MOE_ROWS // 2)
    def _():
        expert_block(nfull, MOE_ROWS)

    @pl.when((rem > 0) & (rem <= MOE_ROWS // 2))
    def _():
        expert_block(nfull, MOE_ROWS // 2)

    ys_sc[nb] = jnp.zeros(ys_sc.shape[1:], BF16)
    ys_sc[nb + 1] = jnp.zeros(ys_sc.shape[1:], BF16)
    ne = pc_ref.shape[1] // 2
    liota = lax.broadcasted_iota(jnp.int32, (MOE_SUB, 2 * MOE_ROWS), 1).astype(F32)
    for s in range(nsub):
        sub = slice(s * MOE_SUB, (s + 1) * MOE_SUB)
        pos = pc_ref[sub, col:col + 1]
        j_lo = jnp.minimum((cum(s) - base) // MOE_ROWS, nb)
        off = (base + j_lo * MOE_ROWS).astype(F32)
        sel = jnp.where(pos - off == liota, 1.0, 0.0).astype(BF16)
        rows = ys_sc[pl.ds(j_lo, 2)].reshape(2 * MOE_ROWS, ys_sc.shape[2])
        y = pc_ref[sub, ne + col:ne + col + 1] * _dot(sel, rows)
        if not first:
            y = yin_ref[sub, :].astype(F32) + y
        o_ref[sub, :] = y.astype(BF16)


def moe(h, comb_t, pos_t, cum_incl, w1, w3, w2):
    t, d = h.shape
    ne = comb_t.shape[0]
    tc = min(MOE_CHUNK, t)
    assert t % tc == 0 and tc % MOE_SUB == 0
    nsub = tc // MOE_SUB
    nt = t // MOE_SUB
    cum = jnp.concatenate([jnp.zeros((ne, 1), jnp.int32), cum_incl[:, ::LANES]], axis=1)
    cum = cum.reshape(-1)
    posr = pos_t.reshape(ne, 1, t)
    pc = jnp.concatenate([pos_t.astype(F32).T, comb_t.T], axis=1)
    once = pl.Buffered(1)
    y = None
    for e in range(ne):
        first = y is None
        chunk = pl.BlockSpec((tc, d), lambda c, cum: (c, 0))
        in_specs = [chunk,
                    pl.BlockSpec((1, 1, tc), lambda c, cum, e=e: (e, 0, c)),
                    pl.BlockSpec((tc, 2 * ne), lambda c, cum: (c, 0))]
        in_specs += [] if first else [chunk]
        in_specs += [pl.BlockSpec((1,) + w.shape[1:], lambda c, cum, e=e: (e, 0, 0), pipeline_mode=once)
                     for w in (w1, w3, w2)]
        grid_spec = pltpu.PrefetchScalarGridSpec(
            num_scalar_prefetch=1, grid=(t // tc,), in_specs=in_specs, out_specs=chunk,
            scratch_shapes=[pltpu.VMEM((MOE_ROWS, d), BF16),
                            pltpu.VMEM((tc // MOE_ROWS + 2, MOE_ROWS, d), BF16)])
        y = pl.pallas_call(
            functools.partial(_moe_kernel, nsub=nsub, cum0=e * (nt + 1), first=first, col=e),
            grid_spec=grid_spec,
            out_shape=jax.ShapeDtypeStruct((t, d), BF16),
            input_output_aliases={} if first else {4: 0},
            compiler_params=pltpu.CompilerParams(dimension_semantics=("arbitrary",),
                                                 vmem_limit_bytes=MOE_VMEM_LIMIT),
            name="moe_expert",
        )(*([cum, h, posr, pc] + ([] if first else [y]) + [w1, w3, w2]))
    return y


def _residual_kernel(x_ref, y_ref, gt_ref, g_ref, o_ref, *, norm):
    x = x_ref[...] + gt_ref[0] * y_ref[...].astype(F32)
    o_ref[...] = _rms(x, g_ref[...]) if norm else x


def residual(x, y, gt, g, batch, norm, tm=512):
    t, d = x.shape
    tm = min(tm, t // batch)
    npb = (t // batch) // tm
    return pl.pallas_call(
        functools.partial(_residual_kernel, norm=norm),
        grid=(t // tm,),
        in_specs=[pl.BlockSpec((tm, d), lambda i: (i, 0)), pl.BlockSpec((tm, d), lambda i: (i, 0)),
                  pl.BlockSpec((1, 1, d), lambda i: (i // npb, 0, 0)), _full(g.shape)],
        out_specs=pl.BlockSpec((tm, d), lambda i: (i, 0)),
        out_shape=jax.ShapeDtypeStruct((t, d), F32),
        compiler_params=_cparams(("parallel",)),
        name="residual_norm" if norm else "residual",
    )(x, y, gt, g)


def _final_norm_kernel(x_ref, g_ref, o_ref):
    o_ref[...] = _rms(x_ref[...], g_ref[...])


def final_norm(x, g, tm=512):
    t, d = x.shape
    return pl.pallas_call(
        _final_norm_kernel,
        grid=(t // tm,),
        in_specs=[pl.BlockSpec((tm, d), lambda i: (i, 0)), _full(g.shape)],
        out_specs=pl.BlockSpec((tm, d), lambda i: (i, 0)),
        out_shape=jax.ShapeDtypeStruct((t, d), F32),
        compiler_params=_cparams(("parallel",)),
        name="final_norm",
    )(x, g)


def _rope_tables(seq, rotate):
    half = ROPE_DIM // 2
    nf = ROPE_DIM // 4
    if rotate:
        t = jnp.arange(seq, dtype=jnp.int32)
        rows = (t // GRID_WIDTH).astype(F32)
        cols = (t % GRID_WIDTH).astype(F32)
        inv = ROPE_THETA ** (-jnp.arange(0, half, 2, dtype=F32) / half)
        ang = jnp.stack([rows[:, None] * inv, cols[:, None] * inv], axis=1)
        cos, sin = jnp.cos(ang), jnp.sin(ang)
    else:
        cos = jnp.ones((seq, 2, nf), F32)
        sin = jnp.zeros((seq, 2, nf), F32)
    cpat = jnp.concatenate([cos, cos], axis=2).reshape(seq, ROPE_DIM)
    spat = jnp.concatenate([-sin, sin], axis=2).reshape(seq, ROPE_DIM)
    return cpat, spat


def _rope_inputs(seq, rotate):
    cpat, spat = _rope_tables(seq, rotate)
    tabq = jnp.concatenate([jnp.tile(cpat, (1, NUM_HEADS)), jnp.tile(spat, (1, NUM_HEADS))], axis=1)
    tabk = jnp.concatenate([cpat, spat, jnp.zeros((seq, LANES - 2 * ROPE_DIM), F32)], axis=1)
    return tabq, tabk


def _swap_perm():
    j = np.arange(ROPE_DIM)
    half = ROPE_DIM // 4
    return np.where((j % (2 * half)) < half, j + half, j - half)


def _placement(shared):
    nsrc = ROPE_DIM if shared else NUM_HEADS * ROPE_DIM
    e = np.zeros((LANES if shared else 2 * nsrc, NUM_HEADS * HEAD_PAD), np.float32)
    for hd in range(NUM_HEADS):
        for j in range(ROPE_DIM):
            col = hd * HEAD_PAD + NOPE_DIM + j
            src = j if shared else hd * ROPE_DIM + j
            e[src, col] = 1.0
            e[nsrc + src, col] = 1.0
    return jnp.asarray(e, dtype=BF16)


def _prep_layer(w_in, b_gate, g_q, w_uq, g_kv, w_ukv, w_oa, w_ob, w_oc, w_out, w_pool, s_pool):
    d = w_in.shape[0]
    q_lora, kv_lora = g_q.shape[0], g_kv.shape[0]
    perm = _swap_perm()
    off_kr = q_lora + kv_lora
    off_pool = off_kr + ROPE_DIM
    pool_w = w_pool.shape[0] * w_pool.shape[1]
    off_fft = off_pool + pool_w
    off_gate = w_in.shape[1] - NUM_BRANCH * d
    kr = w_in[:, off_kr:off_pool]
    wa = jnp.concatenate([w_in[:, :off_kr], kr, kr[:, perm],
                          jnp.zeros((d, LANES - 2 * ROPE_DIM), F32)], axis=1)
    qscale = (NOPE_DIM + ROPE_DIM) ** -0.5 * math.log2(math.e)
    wq = w_uq * qscale
    wqn = jnp.pad(wq[:, :, :NOPE_DIM], ((0, 0), (0, 0), (0, HEAD_PAD - NOPE_DIM)))
    wqr = wq[:, :, NOPE_DIM:]
    wqr = jnp.concatenate([wqr.reshape(q_lora, -1), wqr[:, :, perm].reshape(q_lora, -1)], axis=1)
    wkn = jnp.pad(w_ukv[:, :, :NOPE_DIM], ((0, 0), (0, 0), (0, HEAD_PAD - NOPE_DIM)))
    wv = jnp.pad(w_ukv[:, :, NOPE_DIM:], ((0, 0), (0, 0), (0, VT_ROWS - VAL_DIM)))
    vone = np.zeros((NUM_HEADS, VT_ROWS, 1), np.float32)
    vone[:, VAL_DIM] = 1.0
    return dict(
        wa=wa.astype(BF16), wpool=w_in[:, off_pool:off_fft].astype(BF16),
        wfft=w_in[:, off_fft:off_gate].astype(BF16), wgate=w_in[:, off_gate:].astype(BF16),
        bgate=b_gate.reshape(1, -1),
        gq=g_q.reshape(1, -1), wqn=wqn.reshape(q_lora, -1).astype(BF16), wqr=wqr.astype(BF16),
        eq=_placement(False), gkv=g_kv.reshape(1, -1),
        wkn=wkn.reshape(kv_lora, -1).astype(BF16), ek=_placement(True),
        wvt=wv.reshape(kv_lora, -1).T.astype(BF16), vone=jnp.asarray(vone.reshape(-1, 1)),
        woa=w_oa.astype(BF16), wob=w_ob.astype(BF16), woc=w_oc.astype(BF16), wout=w_out.astype(BF16),
        wp=w_pool.astype(BF16), sp=s_pool.reshape(1, -1),
    )


def _token_mixer(xs, modv1, modv2, lw, g1, g2, ropes, dft, batch, kv_ctx, tm):
    tabq, tabk = ropes
    q, k, vt, pool_in, fft_in, gate = proj_in(xs, modv1, g1, tabq, tabk, lw, batch, False, tm)
    if kv_ctx is None:
        attn = attention(q, k, vt, None, None, batch)
    else:
        attn = attention(q, kv_ctx[0], kv_ctx[1], k, vt, batch)
    pooled = pool_mixer(pool_in, lw["wp"], lw["sp"], batch)
    four = fourier_mixer(fft_in, batch, dft)
    x_new, h2 = merge(attn, pooled, four, gate, xs, modv2, lw, g2, batch, tm)
    return x_new, h2, (k, vt)


def kernel(x, c, ctx, c_ctx, w_mod, b_mod, g_norm1, w_in, b_gate, g_q, w_uq, g_kv, w_ukv, w_pool, s_pool,
           w_oa, w_ob, w_oc, w_out, g_norm2, ffn_w1, ffn_w3, ffn_w2, moe_router, moe_bias,
           moe_w1, moe_w3, moe_w2, g_final):
    batch, seq, d = x.shape
    n_ctx = ctx.shape[1]
    depth = w_mod.shape[0]
    n_exp = moe_router.shape[-1] if moe_router.shape[0] else 0
    xl = x.reshape(batch * seq, d)
    xc = ctx.reshape(batch * n_ctx, d)

    cvec = jnp.concatenate([c, c_ctx[None], jnp.zeros((8 - batch - 1, d), F32)], axis=0)
    mods = modulation(cvec, w_mod, b_mod)

    rope_l = _rope_inputs(seq, True)
    rope_c = _rope_inputs(n_ctx, False)
    dft_l = _dft_tables(seq)
    dft_c = _dft_tables(n_ctx)

    for layer in range(depth):
        last = layer == depth - 1
        lw = _prep_layer(w_in[layer], b_gate[layer], g_q[layer], w_uq[layer], g_kv[layer], w_ukv[layer],
                         w_oa[layer], w_ob[layer], w_oc[layer], w_out[layer], w_pool[layer], s_pool[layer])
        g1 = g_norm1[layer].reshape(1, d)
        g2 = g_norm2[layer].reshape(1, d)
        ml = mods[layer, :batch].reshape(batch, 6, d)
        mc = jnp.broadcast_to(mods[layer, batch].reshape(1, 6, d), (batch, 6, d))

        if last:
            k_c, vt_c = proj_in(xc, mc[:, 0:2], g1, rope_c[0], rope_c[1], lw, batch, True)
            kv_c = (k_c, vt_c)
        else:
            xc_new, h2_c, kv_c = _token_mixer(xc, mc[:, 0:2], mc[:, 2:5], lw, g1, g2, rope_c, dft_c,
                                              batch, None, 256)
        xl, h2_l, _ = _token_mixer(xl, ml[:, 0:2], ml[:, 2:5], lw, g1, g2, rope_l, dft_l, batch, kv_c, 512)
        if not last:
            xc = xc_new

        streams = [(xl, h2_l, ml)]
        if not last:
            streams.append((xc, h2_c, mc))
        fuse_final = last and layer % 2 == 1
        new = []
        for xs, h2, mm in streams:
            gt2 = mm[:, 5:6]
            li = layer // 2
            if layer % 2 == 0:
                xs = ffn(h2, xs, gt2, ffn_w1[li].astype(BF16), ffn_w3[li].astype(BF16),
                         ffn_w2[li].astype(BF16), batch)
            else:
                comb_t, pos_t, cum_incl = router(xs, mm[:, 3:5], g2, moe_router[li].T,
                                                 moe_bias[li].reshape(n_exp, 1), batch)
                y = moe(h2, comb_t, pos_t, cum_incl, moe_w1[li].astype(BF16), moe_w3[li].astype(BF16),
                        moe_w2[li].astype(BF16))
                xs = residual(xs, y, gt2, g_final.reshape(1, d), batch, fuse_final)
            new.append(xs)
        xl = new[0]
        if not last:
            xc = new[1]

    if not fuse_final:
        xl = final_norm(xl, g_final.reshape(1, d))
    return xl.reshape(batch, seq, d)
```
